```python
import math
import jax
import jax.numpy as jnp
from jax import lax
import numpy as np

D_MODEL = 1024
BATCH = 8
SEQ = 4096
DEPTH = 4

HEAD_DIM = 64
NSA_HEADS = 8
NSA_KV_GROUPS = 2
NSA_HPG = NSA_HEADS // NSA_KV_GROUPS
CMP_STRIDE = 16
CMP_BLOCK = 2 * CMP_STRIDE
CMP_HIDDEN = 256
SEL_BLOCK = 64
SEL_TOPK = 16
SEL_LOCAL = 2
WINDOW = 512
NSA_QBLOCK = 64
DN_HEADS = 8
DN_CONV = 4
DN_CHUNK = 64
REL_BUCKETS = 32
REL_MAX_DIST = 1024
N_EXPERTS = 16
N_GROUPS = 4
EXPERTS_PER_GROUP = N_EXPERTS // N_GROUPS
TOPK_GROUPS = 1
TOPK = 2
D_EXPERT = 512
MOE_BLOCK = 256

NORM_EPS = 1e-6
FORCE_SCORE = 1e9
NEG = -1e30

NSA_Q = NSA_HEADS * HEAD_DIM
NSA_KV = NSA_KV_GROUPS * HEAD_DIM
DN_W = DN_HEADS * HEAD_DIM
IN_SIZES = (NSA_Q, 6 * NSA_KV, 3 * NSA_HEADS, 3 * DN_W, DN_HEADS, DN_HEADS, DN_W, 2 * D_MODEL)
IN_SPLITS = tuple(sum(IN_SIZES[:i + 1]) for i in range(len(IN_SIZES) - 1))
D_IN = sum(IN_SIZES)

kernel_name = 'hybrid_nsa_deltanet_grouped_moe_adaln'


def _rms(x, g):
    xf = x.astype(jnp.float32)
    y = xf * lax.rsqrt(jnp.mean(xf * xf, axis=-1, keepdims=True) + NORM_EPS)
    return (y * g.astype(jnp.float32)).astype(x.dtype)


def _l2n(x):
    xf = x.astype(jnp.float32)
    return xf * lax.rsqrt(jnp.sum(xf * xf, axis=-1, keepdims=True) + NORM_EPS)


def _masked_softmax(logits, mask):
    p = jax.nn.softmax(jnp.where(mask, logits.astype(jnp.float32), NEG), axis=-1)
    return jnp.where(mask, p, 0.0)


def _rel_bucket(dist):
    exact = REL_BUCKETS // 2
    dist = jnp.maximum(dist, 0)
    far = jnp.maximum(dist, exact).astype(jnp.float32)
    large = exact + (jnp.log(far / exact) / math.log(REL_MAX_DIST / exact) * (REL_BUCKETS - exact)).astype(jnp.int32)
    return jnp.where(dist < exact, dist, jnp.minimum(large, REL_BUCKETS - 1))


def _compress(t, pos, w1, w2):
    b, s, g, d = t.shape
    ch = t.reshape(b, s // CMP_STRIDE, CMP_STRIDE, g, d)
    blk = jnp.concatenate([ch[:, :-1], ch[:, 1:]], axis=2) + pos[:, None, :]
    flat = jnp.swapaxes(blk, 2, 3).reshape(b, s // CMP_STRIDE - 1, g, CMP_BLOCK * d)
    return jax.nn.gelu(flat @ w1) @ w2


def _nsa(q, kv, gate, rel_bias, qk_g, cmp_pos, cmp_w1, cmp_w2):
    b, s, _ = q.shape
    G, HPG, dh, QB = NSA_KV_GROUPS, NSA_HPG, HEAD_DIM, NSA_QBLOCK
    n_cmp = s // CMP_STRIDE - 1
    n_blk = s // SEL_BLOCK
    n_sel = min(SEL_TOPK, n_blk)
    n_qb = s // QB
    q = _rms(q.reshape(b, s, G, HPG, dh), qk_g[0]) * (dh ** -0.5)
    kv = kv.reshape(b, s, 6, G, dh)
    k_cmp = _rms(_compress(kv[:, :, 0], cmp_pos[0], cmp_w1[0], cmp_w2[0]), qk_g[1])
    v_cmp = _compress(kv[:, :, 1], cmp_pos[1], cmp_w1[1], cmp_w2[1])
    k_slc = _rms(kv[:, :, 2], qk_g[2]).reshape(b, n_blk, SEL_BLOCK, G, dh).transpose(0, 3, 1, 2, 4)
    v_slc = kv[:, :, 3].reshape(b, n_blk, SEL_BLOCK, G, dh).transpose(0, 3, 1, 2, 4)
    pad = ((0, 0), (WINDOW, 0), (0, 0), (0, 0))
    k_win = jnp.pad(_rms(kv[:, :, 4], qk_g[3]), pad)
    v_win = jnp.pad(kv[:, :, 5], pad)
    gate = jax.nn.sigmoid(gate).reshape(b, s, G, HPG, 3)
    bias_g = rel_bias.reshape(REL_BUCKETS, G, HPG)
    ratio, nsub = SEL_BLOCK // CMP_STRIDE, CMP_BLOCK // CMP_STRIDE
    delta = jnp.arange(n_cmp)[:, None] - ratio * jnp.arange(n_blk)[None, :]
    m_idx = delta[..., None] + jnp.arange(nsub)
    overlap = jnp.sum((m_idx >= 0) & (m_idx < ratio), axis=-1).astype(jnp.float32)
    cmp_end = jnp.arange(n_cmp) * CMP_STRIDE + CMP_BLOCK - 1
    b_ix = jnp.arange(b)[:, None, None, None]
    g_ix = jnp.arange(G)[None, None, :, None]

    def head_bias(dist):
        return rel_bias[_rel_bucket(dist)].reshape(dist.shape[0], dist.shape[1], G, HPG).transpose(0, 2, 3, 1)

    def block(args):
        qb, gb, blk = args
        t = blk * QB + jnp.arange(QB)
        dist_c = t[:, None] - cmp_end[None, :]
        s_c = jnp.einsum('bqghd,bngd->bqghn', qb, k_cmp) + head_bias(dist_c)
        p_c = _masked_softmax(s_c, (dist_c >= 0)[:, None, None, :])
        o_c = jnp.einsum('bqghn,bngd->bqghd', p_c.astype(v_cmp.dtype), v_cmp)
        imp = jnp.einsum('bqghn,nj->bqgj', p_c, overlap)
        cur = t // SEL_BLOCK
        jb = jnp.arange(n_blk)[None, :]
        valid = (jb <= cur[:, None])[:, None, :]
        forced = valid & ((jb == 0) | (jb > cur[:, None] - SEL_LOCAL))[:, None, :]
        score = jnp.where(forced, FORCE_SCORE, jnp.where(valid, imp, -1.0))
        _, idx = lax.top_k(score, n_sel)
        kb = k_slc[b_ix, g_ix, idx]
        vb = v_slc[b_ix, g_ix, idx]
        pos = idx[..., None] * SEL_BLOCK + jnp.arange(SEL_BLOCK)
        dist_s = t[None, :, None, None, None] - pos
        bias_s = jnp.moveaxis(bias_g[_rel_bucket(dist_s), g_ix[..., None]], -1, 3)
        s_s = jnp.einsum('bqghd,bqgnkd->bqghnk', qb, kb) + bias_s
        mask_s = (dist_s >= 0).reshape(b, QB, G, 1, n_sel * SEL_BLOCK)
        p_s = _masked_softmax(s_s.reshape(b, QB, G, HPG, n_sel * SEL_BLOCK), mask_s)
        o_s = jnp.einsum('bqghm,bqgmd->bqghd', p_s.astype(vb.dtype), vb.reshape(b, QB, G, n_sel * SEL_BLOCK, dh))
        kw = lax.dynamic_slice_in_dim(k_win, blk * QB, WINDOW + QB, axis=1)
        vw = lax.dynamic_slice_in_dim(v_win, blk * QB, WINDOW + QB, axis=1)
        kpos = blk * QB - WINDOW + jnp.arange(WINDOW + QB)
        dist_w = t[:, None] - kpos[None, :]
        mask_w = (dist_w >= 0) & (dist_w < WINDOW) & (kpos[None, :] >= 0)
        s_w = jnp.einsum('bqghd,bkgd->bqghk', qb, kw) + head_bias(dist_w)
        p_w = _masked_softmax(s_w, mask_w[:, None, None, :])
        o_w = jnp.einsum('bqghk,bkgd->bqghd', p_w.astype(vw.dtype), vw)
        return gb[..., 0:1] * o_c + gb[..., 1:2] * o_s + gb[..., 2:3] * o_w

    q_blocks = jnp.moveaxis(q.reshape(b, n_qb, QB, G, HPG, dh), 1, 0)
    g_blocks = jnp.moveaxis(gate.reshape(b, n_qb, QB, G, HPG, 3), 1, 0)
    o = lax.map(block, (q_blocks, g_blocks, jnp.arange(n_qb)))
    return jnp.moveaxis(o, 0, 1).reshape(b, s, NSA_Q)


def _gated_deltanet(qkv, beta_raw, a_raw, z, conv_w, a_log, dt_bias, norm_g):
    b, s, ch = qkv.shape
    H, d, C = DN_HEADS, HEAD_DIM, DN_CHUNK
    n_ch = s // C
    qkv = jax.nn.silu(lax.conv_general_dilated(
        qkv, conv_w.reshape(DN_CONV, 1, ch), window_strides=(1,), padding=[(DN_CONV - 1, 0)],
        dimension_numbers=('NWC', 'WIO', 'NWC'), feature_group_count=ch))
    qkv = qkv.reshape(b, s, 3, H, d)
    q = _l2n(qkv[:, :, 0]) * (d ** -0.5)
    k = _l2n(qkv[:, :, 1])
    v = qkv[:, :, 2].astype(jnp.float32)
    beta = jax.nn.sigmoid(beta_raw.astype(jnp.float32))
    g = -jnp.exp(a_log.astype(jnp.float32)) * jax.nn.softplus(a_raw.astype(jnp.float32) + dt_bias.astype(jnp.float32))

    def chunks(t):
        return jnp.moveaxis(t.reshape(b, n_ch, C, H, *t.shape[3:]), 3, 1)

    q, k, v, beta, g = chunks(q), chunks(k), chunks(v), chunks(beta), chunks(g)
    g_cum = jnp.cumsum(g, axis=-1)
    causal = jnp.tril(jnp.ones((C, C), dtype=bool))
    strict = jnp.tril(jnp.ones((C, C), dtype=bool), -1)
    decay = jnp.where(causal, jnp.exp(jnp.where(causal, g_cum[..., :, None] - g_cum[..., None, :], 0.0)), 0.0)
    k_beta = k * beta[..., None]
    lower = jnp.where(strict, jnp.einsum('bhnid,bhnjd->bhnij', k_beta, k) * decay, 0.0)
    eye = jnp.eye(C, dtype=jnp.float32)
    t_inv = lax.linalg.triangular_solve(lower + eye, jnp.broadcast_to(eye, lower.shape),
                                        left_side=True, lower=True, unit_diagonal=True)
    u = t_inv @ (v * beta[..., None])
    w = t_inv @ (k_beta * jnp.exp(g_cum)[..., None])
    attn = jnp.where(causal, jnp.einsum('bhnid,bhnjd->bhnij', q, k) * decay, 0.0)
    g_last = g_cum[..., -1:]
    q_dec = q * jnp.exp(g_cum)[..., None]
    k_dec = k * jnp.exp(g_last - g_cum)[..., None]

    def step(state, xs):
        qd, kd, uc, wc, ac, gl = xs
        v_new = uc - wc @ state
        out = qd @ state + ac @ v_new
        state = state * jnp.exp(gl)[..., None] + jnp.swapaxes(kd, -1, -2) @ v_new
        return state, out

    xs = tuple(jnp.moveaxis(t, 2, 0) for t in (q_dec, k_dec, u, w, attn, g_last))
    _, o = lax.scan(step, jnp.zeros((b, H, d, d), jnp.float32), xs)
    o = jnp.moveaxis(o, 0, 2).reshape(b, H, s, d).transpose(0, 2, 1, 3)
    o = _rms(o, norm_g) * jax.nn.silu(z.astype(jnp.float32)).reshape(b, s, H, d)
    return o.reshape(b, s, DN_W).astype(z.dtype)


def _mixer(h, w_in, rel_bias, qk_g, cmp_pos, cmp_w1, cmp_w2, conv_w, a_log, dt_bias, dn_norm_g,
           w_br_a, w_br_b, w_out):
    b, s, _ = h.shape
    q_a, kv_a, gate_a, qkv_b, beta_b, a_b, z_b, merge = jnp.split(h @ w_in, IN_SPLITS, axis=-1)
    y_a = _nsa(q_a, kv_a, gate_a, rel_bias, qk_g, cmp_pos, cmp_w1, cmp_w2)
    y_b = _gated_deltanet(qkv_b, beta_b, a_b, z_b, conv_w, a_log, dt_bias, dn_norm_g)
    merge = jax.nn.sigmoid(merge).reshape(b, s, 2, D_MODEL)
    y = merge[:, :, 0] * (y_a @ w_br_a) + merge[:, :, 1] * (y_b @ w_br_b)
    return y @ w_out


def _moe(h, router_w, router_b, w_gate, w_up, w_down):
    b, s, D = h.shape
    T = b * s
    xt = h.reshape(T, D)
    scores = jax.nn.sigmoid((xt @ router_w).astype(jnp.float32))
    sel = scores + router_b.astype(jnp.float32)
    grp_score = lax.top_k(sel.reshape(T, N_GROUPS, EXPERTS_PER_GROUP), 2)[0].sum(-1)
    _, top_grp = lax.top_k(grp_score, TOPK_GROUPS)
    grp_mask = jnp.any(top_grp[..., None] == jnp.arange(N_GROUPS), axis=1)
    masked = jnp.where(jnp.repeat(grp_mask, EXPERTS_PER_GROUP, axis=1), sel, -jnp.inf)
    _, e_idx = lax.top_k(masked, TOPK)
    wts = jnp.take_along_axis(scores, e_idx, axis=1)
    wts = wts / jnp.sum(wts, axis=-1, keepdims=True)
    A = T * TOPK
    e_flat = e_idx.reshape(A)
    order = jnp.argsort(e_flat)
    e_s = e_flat[order]
    tok_s = jnp.repeat(jnp.arange(T, dtype=jnp.int32), TOPK)[order]
    w_s = wts.reshape(A)[order]
    counts = jnp.zeros((N_EXPERTS,), jnp.int32).at[e_flat].add(1)
    padded = (counts + MOE_BLOCK - 1) // MOE_BLOCK * MOE_BLOCK
    p_end = jnp.cumsum(padded)
    dest = (p_end - padded)[e_s] + jnp.arange(A) - (jnp.cumsum(counts) - counts)[e_s]
    P = A + N_EXPERTS * MOE_BLOCK
    NB = P // MOE_BLOCK
    buf_tok = jnp.full((P,), T, jnp.int32).at[dest].set(tok_s)
    buf_w = jnp.zeros((P,), h.dtype).at[dest].set(w_s.astype(h.dtype))
    blk_e = jnp.minimum(jnp.searchsorted(p_end, jnp.arange(NB) * MOE_BLOCK, side='right'), N_EXPERTS - 1)
    xb = jnp.concatenate([xt, jnp.zeros((1, D), h.dtype)], axis=0)[buf_tok].reshape(NB, MOE_BLOCK, D)

    def expert_block(args):
        xblk, e = args
        return (jax.nn.silu(xblk @ w_gate[e]) * (xblk @ w_up[e])) @ w_down[e]

    yb = lax.map(expert_block, (xb, blk_e)).reshape(P, D) * buf_w[:, None]
    out = jnp.zeros((T + 1, D), h.dtype).at[buf_tok].add(yb)[:T]
    return out.reshape(b, s, D)


def setup_inputs(seed: int = 0) -> dict:
    key = jax.random.key(seed)
    ks = jax.random.split(key, 24)
    L, D, E, F = DEPTH, D_MODEL, N_EXPERTS, D_EXPERT

    def nrm(k, shape, scale):
        return jax.random.normal(k, shape, jnp.float32) * scale

    dt = jnp.exp(jax.random.uniform(ks[16], (L, DN_HEADS), jnp.float32, math.log(1e-3), math.log(1e-1)))
    return {
        'x': nrm(ks[0], (BATCH, SEQ, D), 1.0),
        'c': nrm(ks[1], (BATCH, D), 1.0),
        'rel_bias': nrm(ks[2], (REL_BUCKETS, NSA_HEADS), 0.2),
        'router_w': nrm(ks[3], (D, E), D ** -0.5),
        'router_b': nrm(ks[4], (E,), 0.01),
        'ada_w': nrm(ks[5], (L, D, 6 * D), 0.2 * D ** -0.5),
        'ada_b': nrm(ks[6], (L, 6 * D), 0.02),
        'norm1_g': 1.0 + nrm(ks[7], (L, D), 0.02),
        'norm2_g': 1.0 + nrm(ks[8], (L, D), 0.02),
        'w_in': nrm(ks[9], (L, D, D_IN), D ** -0.5),
        'qk_norm_g': 1.0 + nrm(ks[10], (L, 4, HEAD_DIM), 0.02),
        'cmp_pos': nrm(ks[11], (L, 2, CMP_BLOCK, HEAD_DIM), 0.1),
        'cmp_w1': nrm(ks[12], (L, 2, CMP_BLOCK * HEAD_DIM, CMP_HIDDEN), (CMP_BLOCK * HEAD_DIM) ** -0.5),
        'cmp_w2': nrm(ks[13], (L, 2, CMP_HIDDEN, HEAD_DIM), CMP_HIDDEN ** -0.5),
        'dn_conv_w': nrm(ks[14], (L, DN_CONV, 3 * DN_W), DN_CONV ** -0.5),
        'dn_a_log': jnp.log(jax.random.uniform(ks[15], (L, DN_HEADS), jnp.float32, 1.0, 16.0)),
        'dn_dt_bias': dt + jnp.log(-jnp.expm1(-dt)),
        'dn_norm_g': 1.0 + nrm(ks[17], (L, HEAD_DIM), 0.02),
        'w_branch_a': nrm(ks[18], (L, NSA_Q, D), NSA_Q ** -0.5),
        'w_branch_b': nrm(ks[19], (L, DN_W, D), DN_W ** -0.5),
        'w_out': nrm(ks[20], (L, D, D), D ** -0.5),
        'moe_w_gate': nrm(ks[21], (L, E, D, F), D ** -0.5),
        'moe_w_up': nrm(ks[22], (L, E, D, F), D ** -0.5),
        'moe_w_down': nrm(ks[23], (L, E, F, D), F ** -0.5),
    }


def reference(x, c, rel_bias, router_w, router_b, ada_w, ada_b, norm1_g, norm2_g, w_in, qk_norm_g,
              cmp_pos, cmp_w1, cmp_w2, dn_conv_w, dn_a_log, dn_dt_bias, dn_norm_g, w_branch_a,
              w_branch_b, w_out, moe_w_gate, moe_w_up, moe_w_down):
    c_act = jax.nn.silu(c)
    for l in range(DEPTH):
        mod = (c_act @ ada_w[l] + ada_b[l])[:, None, :]
        sh1, sc1, g1, sh2, sc2, g2 = jnp.split(mod, 6, axis=-1)
        h = _rms(x, norm1_g[l]) * (1.0 + sc1) + sh1
        x = x + g1 * _mixer(h, w_in[l], rel_bias, qk_norm_g[l], cmp_pos[l], cmp_w1[l], cmp_w2[l],
                            dn_conv_w[l], dn_a_log[l], dn_dt_bias[l], dn_norm_g[l],
                            w_branch_a[l], w_branch_b[l], w_out[l])
        h = _rms(x, norm2_g[l]) * (1.0 + sc2) + sh2
        x = x + g2 * _moe(h, router_w, router_b, moe_w_gate[l], moe_w_up[l], moe_w_down[l])
    return x
```

```python
import functools
import math

import numpy as np
import jax
import jax.numpy as jnp
from jax import lax
from jax.experimental import pallas as pl
from jax.experimental.pallas import tpu as pltpu

F32 = jnp.float32
BF16 = jnp.bfloat16

D_MODEL = 1024
DEPTH = 4
HEAD_DIM = 64
NSA_HEADS = 8
NSA_GROUPS = 2
NSA_HPG = NSA_HEADS // NSA_GROUPS
CMP_STRIDE = 16
CMP_BLOCK = 32
CMP_HIDDEN = 256
SEL_BLOCK = 64
SEL_TOPK = 16
SEL_LOCAL = 2
WINDOW = 512
DN_HEADS = 8
DN_CONV = 4
DN_CHUNK = 64
REL_BUCKETS = 32
REL_MAX_DIST = 1024
N_EXPERTS = 16
N_GROUPS = 4
EXPERTS_PER_GROUP = N_EXPERTS // N_GROUPS
TOPK = 2
D_EXPERT = 512
NORM_EPS = 1e-6
FORCE_SCORE = 1e9
NEG = -1e30

NSA_Q = NSA_HEADS * HEAD_DIM
NSA_KV = NSA_GROUPS * HEAD_DIM
DN_W = DN_HEADS * HEAD_DIM
IN_SIZES = (NSA_Q, 6 * NSA_KV, 3 * NSA_HEADS, 3 * DN_W, DN_HEADS, DN_HEADS, DN_W, 2 * D_MODEL)
IN_OFFS = tuple(int(v) for v in np.cumsum((0,) + IN_SIZES))

LANES = 128
VMEM_LIMIT = 56 * 1024 * 1024

TM_PROJ = 512
TM_MERGE = 512
TQ = 128
TK_SEL = 512
CT_DN = 256
TM_MOE = 1024
BLK_MOE = 128
ROW_ALIGN = 8

_SEG_Q = (0, 512)
_SEG_CMP = (512, 768)
_SEG_K = (768, 1024)
_SEG_V = (1024, 1536)
_SEG_DN = (1536, 3072)
_SEG_Z = (3072, 3584)
_SEG_MG = (3584, 5632)
_SEG_SM = (5632, 5760)
N_PROJ = 5760
SM_GATE, SM_BETA, SM_A = 0, 24, 32


def _proj_columns():
    o = IN_OFFS
    kv = o[1]
    cols = list(range(o[0], o[1]))
    cols += list(range(kv, kv + 256))
    cols += list(range(kv + 256, kv + 384)) + list(range(kv + 512, kv + 640))
    for base in (kv + 384, kv + 640):
        cols += list(range(base, base + 64)) + [-1] * 128 + list(range(base + 64, base + 128))
    cols += list(range(o[3], o[4]))
    cols += list(range(o[6], o[7]))
    cols += list(range(o[7], o[8]))
    cols += list(range(o[2], o[3])) + list(range(o[4], o[5])) + list(range(o[5], o[6]))
    cols += [-1] * (N_PROJ - len(cols))
    return np.asarray(cols, np.int32)


def _rel_bucket_table(n):
    exact = REL_BUCKETS // 2
    d = np.arange(n, dtype=np.int64)
    far = np.maximum(d, exact).astype(np.float64)
    large = exact + (np.log(far / exact) / math.log(REL_MAX_DIST / exact) * (REL_BUCKETS - exact)).astype(np.int64)
    return np.where(d < exact, d, np.minimum(large, REL_BUCKETS - 1)).astype(np.int32)


def _cparams(sem, vmem=VMEM_LIMIT):
    return pltpu.CompilerParams(dimension_semantics=sem, vmem_limit_bytes=vmem)


def _bdot(a, b):
    return jnp.dot(a, b, preferred_element_type=F32)


def _dot_nt(a, b):
    return lax.dot_general(a, b, (((1,), (1,)), ((), ())), preferred_element_type=F32)


def _dot_tn(a, b):
    return lax.dot_general(a, b, (((0,), (0,)), ((), ())), preferred_element_type=F32)


def _split3(x):
    h = x.astype(BF16)
    r = x - h.astype(F32)
    m = r.astype(BF16)
    l = (r - m.astype(F32)).astype(BF16)
    return h, m, l


def _seg64_sumsq(x):
    rows, width = x.shape
    low = lax.broadcasted_iota(jnp.int32, (rows, LANES), 1) < 64
    outs = []
    for c in range(width // LANES):
        sq = x[:, c * LANES:(c + 1) * LANES]
        sq = sq * sq
        s_lo = jnp.sum(jnp.where(low, sq, 0.0), axis=-1, keepdims=True)
        s_hi = jnp.sum(jnp.where(low, 0.0, sq), axis=-1, keepdims=True)
        outs.append(jnp.where(low, s_lo, s_hi))
    return outs[0] if len(outs) == 1 else jnp.concatenate(outs, axis=1)


def _ada_kernel(c_ref, w_ref, b_ref, o_ref):
    ca = jax.nn.silu(c_ref[...]).astype(BF16)
    o_ref[0] = _bdot(ca, w_ref[0].astype(BF16)) + b_ref[0]


def _ada_mod(c, ada_w, ada_b):
    L, D, N = ada_w.shape
    B = c.shape[0]
    tn = 1536
    out = pl.pallas_call(
        _ada_kernel,
        grid=(L, N // tn),
        in_specs=[pl.BlockSpec((B, D), lambda l, j: (0, 0)),
                  pl.BlockSpec((1, D, tn), lambda l, j: (l, 0, j)),
                  pl.BlockSpec((1, 1, tn), lambda l, j: (l, 0, j))],
        out_specs=pl.BlockSpec((1, B, tn), lambda l, j: (l, 0, j)),
        out_shape=jax.ShapeDtypeStruct((L, B, N), F32),
        compiler_params=_cparams(("arbitrary", "arbitrary")),
        name="ada_mod",
    )(c, ada_w, ada_b.reshape(L, 1, N))
    return out.reshape(L, B, 6, D)


def _inproj_kernel(x_ref, mod_ref, ng_ref, w_ref, qg_ref, kg_ref,
                   q_out, cmp_out, k_out, v_out, dn_out, z_out, mg_out, sm_out):
    x = x_ref[...]
    ms = jnp.mean(x * x, axis=-1, keepdims=True)
    h = x * lax.rsqrt(ms + NORM_EPS) * ng_ref[...]
    h = h * (1.0 + mod_ref[0, 1:2, :]) + mod_ref[0, 0:1, :]
    hb = h.astype(BF16)

    def seg(ab):
        return _bdot(hb, w_ref[:, ab[0]:ab[1]])

    q = seg(_SEG_Q)
    q = q * lax.rsqrt(_seg64_sumsq(q) * (1.0 / HEAD_DIM) + NORM_EPS) * qg_ref[...]
    q_out[...] = (q * (HEAD_DIM ** -0.5)).astype(BF16)
    cmp_out[...] = seg(_SEG_CMP)
    k = seg(_SEG_K)
    k = k * lax.rsqrt(_seg64_sumsq(k) * (1.0 / HEAD_DIM) + NORM_EPS) * kg_ref[...]
    k_out[...] = k.astype(BF16)
    v = seg(_SEG_V)
    lane = lax.broadcasted_iota(jnp.int32, v.shape, 1) % 256
    ones = jnp.where((lane >= 64) & (lane < 192), 1.0, 0.0)
    v_out[...] = (v + ones).astype(BF16)
    dn_out[...] = seg(_SEG_DN)
    z_out[...] = seg(_SEG_Z)
    mg_out[...] = seg(_SEG_MG)
    sm_out[...] = seg(_SEG_SM)


def _in_proj(x2, mod_l, norm_g, w_proj, q_gain, k_gain, seq):
    T, D = x2.shape
    tm = TM_PROJ
    tpb = seq // tm
    widths = [(512, BF16), (256, F32), (256, BF16), (512, BF16), (1536, F32), (512, F32), (2048, F32), (128, F32)]
    return pl.pallas_call(
        _inproj_kernel,
        grid=(T // tm,),
        in_specs=[pl.BlockSpec((tm, D), lambda i: (i, 0)),
                  pl.BlockSpec((1, 6, D), lambda i: (i // tpb, 0, 0)),
                  pl.BlockSpec((1, D), lambda i: (0, 0)),
                  pl.BlockSpec((D, N_PROJ), lambda i: (0, 0)),
                  pl.BlockSpec((1, 512), lambda i: (0, 0)),
                  pl.BlockSpec((1, 256), lambda i: (0, 0))],
        out_specs=[pl.BlockSpec((tm, w), lambda i: (i, 0)) for w, _ in widths],
        out_shape=[jax.ShapeDtypeStruct((T, w), dt) for w, dt in widths],
        compiler_params=_cparams(("arbitrary",)),
        name="in_proj",
    )(x2, mod_l, norm_g, w_proj, q_gain, k_gain)


def _compress_kernel(kraw_ref, vraw_ref, pos_ref, w1_ref, w2_ref, kg_ref, kc_out, vc_out, *, n_chunks):
    for j, raw_ref, out_ref in ((0, kraw_ref, kc_out), (1, vraw_ref, vc_out)):
        top = [jnp.zeros((n_chunks, CMP_HIDDEN), F32) for _ in range(NSA_GROUPS)]
        bot = [jnp.zeros((n_chunks, CMP_HIDDEN), F32) for _ in range(NSA_GROUPS)]
        for r in range(CMP_STRIDE):
            xr = raw_ref[0, pl.ds(r, n_chunks, stride=CMP_STRIDE), :]
            x_top = (xr + pos_ref[j, r:r + 1, :]).astype(BF16)
            x_bot = (xr + pos_ref[j, CMP_STRIDE + r:CMP_STRIDE + r + 1, :]).astype(BF16)
            for g in range(NSA_GROUPS):
                ls = slice(g * HEAD_DIM, (g + 1) * HEAD_DIM)
                top[g] = top[g] + _bdot(x_top[:, ls], w1_ref[j, r * HEAD_DIM:(r + 1) * HEAD_DIM, :])
                bot[g] = bot[g] + _bdot(x_bot[:, ls], w1_ref[j, (CMP_STRIDE + r) * HEAD_DIM:(CMP_STRIDE + r + 1) * HEAD_DIM, :])
        outs = []
        for g in range(NSA_GROUPS):
            hid = top[g] + pltpu.roll(bot[g], n_chunks - 1, 0)
            o = _bdot(jax.nn.gelu(hid).astype(BF16), w2_ref[j])
            if j == 0:
                ms = jnp.mean(o * o, axis=-1, keepdims=True)
                o = o * lax.rsqrt(ms + NORM_EPS) * kg_ref[...]
            outs.append(o)
        out_ref[0] = jnp.concatenate(outs, axis=1).astype(BF16)


def _compress(cmp_raw, pos128, w1, w2, k_gain):
    B, S, _ = cmp_raw.shape
    nc = S // CMP_STRIDE
    return pl.pallas_call(
        functools.partial(_compress_kernel, n_chunks=nc),
        grid=(B,),
        in_specs=[pl.BlockSpec((1, S, LANES), lambda b: (b, 0, 0)),
                  pl.BlockSpec((1, S, LANES), lambda b: (b, 0, 1)),
                  pl.BlockSpec((2, CMP_BLOCK, LANES), lambda b: (0, 0, 0)),
                  pl.BlockSpec((2, CMP_BLOCK * HEAD_DIM, CMP_HIDDEN), lambda b: (0, 0, 0)),
                  pl.BlockSpec((2, CMP_HIDDEN, HEAD_DIM), lambda b: (0, 0, 0)),
                  pl.BlockSpec((1, HEAD_DIM), lambda b: (0, 0))],
        out_specs=[pl.BlockSpec((1, nc, LANES), lambda b: (b, 0, 0))] * 2,
        out_shape=[jax.ShapeDtypeStruct((B, nc, LANES), BF16)] * 2,
        compiler_params=_cparams(("arbitrary",)),
        name="nsa_compress",
    )(cmp_raw, cmp_raw, pos128, w1, w2, k_gain)


def _nsa_tables(rel_bias, seq):
    nq = seq // TQ
    ncp = seq // CMP_STRIDE
    bucket = jnp.asarray(_rel_bucket_table(seq + WINDOW + TQ))
    rb = rel_bias.astype(F32)
    q = np.arange(TQ)[:, None]
    k = np.arange(LANES)[None, :]

    def lookup(dist, valid):
        b = bucket[np.clip(dist, 0, None)]
        vals = rb[b]
        vals = jnp.where(jnp.asarray(valid)[..., None], vals, NEG)
        vals = jnp.moveaxis(vals, -1, 0)
        return vals.reshape(NSA_GROUPS, NSA_HPG * dist.shape[0], dist.shape[1])

    dlim = int(np.argmax(_rel_bucket_table(seq + WINDOW + TQ) == REL_BUCKETS - 1))
    dt = min(nq - 1, -(-(dlim + TQ - 1) // TQ))
    sel = [jnp.full((NSA_GROUPS, NSA_HPG * TQ, LANES), NEG, F32)]
    for d in range(dt + 1):
        dist = TQ * d + q - k
        sel.append(lookup(dist, dist >= 0))
    sel_tab = jnp.stack(sel)
    win = []
    for c in range(WINDOW // LANES + 1):
        dist = q + WINDOW - LANES * c - k
        win.append(lookup(dist, (dist >= 0) & (dist < WINDOW)))
    win_tab = jnp.stack(win)
    n = np.arange(ncp)[None, :]
    tt = np.arange(seq)[:, None]
    dist = tt - (n * CMP_STRIDE + CMP_BLOCK - 1)
    cmp_tab = lookup(dist, (dist >= 0) & (n < ncp - 1))
    cmp_tab = cmp_tab.reshape(NSA_GROUPS, NSA_HPG, nq, TQ, ncp).transpose(2, 0, 1, 3, 4)
    cmp_tab = cmp_tab.reshape(nq, NSA_GROUPS, NSA_HPG * TQ, ncp)
    return sel_tab, win_tab, cmp_tab, dt


def _nsa_static(seq):
    nb = seq // SEL_BLOCK
    ncp = seq // CMP_STRIDE
    ratio, nsub = SEL_BLOCK // CMP_STRIDE, CMP_BLOCK // CMP_STRIDE
    delta = np.arange(ncp)[:, None] - ratio * np.arange(nb)[None, :]
    m_idx = delta[..., None] + np.arange(nsub)
    overlap = np.sum((m_idx >= 0) & (m_idx < ratio), axis=-1).astype(np.float32)
    overlap[ncp - 1, :] = 0.0
    expand = (np.arange(nb)[:, None] == (np.arange(seq)[None, :] // SEL_BLOCK)).astype(np.float32)
    return jnp.asarray(overlap.T, BF16), jnp.asarray(expand, BF16)


def _nsa_kernel(q_ref, sm_ref, kc_ref, vc_ref, k_ref, v_ref, bc_ref, ws_ref, ww_ref, ovt_ref, ex_ref,
                o_ref, m_s, acc_s, m_w, acc_w, *, n_blocks, n_sel, dt):
    i = pl.program_id(1)
    t0 = i * TQ
    rows = NSA_HPG * TQ
    half = lax.broadcasted_iota(jnp.int32, (TQ, LANES), 1) // HEAD_DIM
    q = q_ref[0].astype(F32)
    gates = jax.nn.sigmoid(sm_ref[0])
    jb = lax.broadcasted_iota(jnp.int32, (n_blocks, TQ), 0)
    cur = (t0 + lax.broadcasted_iota(jnp.int32, (n_blocks, TQ), 1)) // SEL_BLOCK
    valid = jb <= cur
    forced = valid & ((jb == 0) | (jb > cur - SEL_LOCAL))
    sub = TK_SEL // LANES
    y_heads = []

    def online_update(s, v_tile, m_ref, acc_ref):
        m_old = m_ref[...]
        m_new = jnp.maximum(m_old, jnp.max(s, axis=-1, keepdims=True))
        p = jnp.exp(s - m_new)
        acc_ref[...] = jnp.exp(m_old - m_new) * acc_ref[...] + _bdot(p.astype(BF16), v_tile)
        m_ref[...] = m_new

    for g in range(NSA_GROUPS):
        stack = []
        for h in range(NSA_HPG):
            hd = NSA_HPG * g + h
            blk = q[:, (hd // 2) * LANES:(hd // 2 + 1) * LANES]
            if hd % 2 != g:
                blk = pltpu.roll(blk, HEAD_DIM, 1)
            stack.append(jnp.where(half == g, blk, 0.0))
        qg = jnp.concatenate(stack, axis=0).astype(BF16)

        bc = bc_ref[0, g]
        sc = _dot_nt(qg, kc_ref[0]) + bc
        ec = jnp.exp(sc - jnp.max(sc, axis=-1, keepdims=True))
        ec = jnp.where(bc > 0.5 * NEG, ec, 0.0)
        lc = jnp.sum(ec, axis=-1, keepdims=True)
        pc = ec / jnp.where(lc > 0.0, lc, 1.0)
        o_c = _bdot(pc.astype(BF16), vc_ref[0])

        psum = pc[0:TQ] + pc[TQ:2 * TQ] + pc[2 * TQ:3 * TQ] + pc[3 * TQ:4 * TQ]
        p_hi = psum.astype(BF16)
        p_lo = (psum - p_hi.astype(F32)).astype(BF16)
        imp = _dot_nt(ovt_ref[...], p_hi) + _dot_nt(ovt_ref[...], p_lo)
        score = jnp.where(forced, FORCE_SCORE, jnp.where(valid, imp, -1.0))

        def pick(_, carry):
            sc_, sel_ = carry
            top = jnp.max(sc_, axis=0, keepdims=True)
            first = jnp.min(jnp.where(sc_ == top, jb, n_blocks), axis=0, keepdims=True)
            hit = jb == first
            return jnp.where(hit, -3e38, sc_), jnp.where(hit, 1.0, sel_)

        _, sel = lax.fori_loop(0, n_sel, pick, (score, jnp.zeros((n_blocks, TQ), F32)))
        sel_q = jnp.transpose(sel).astype(BF16)

        m_s[...] = jnp.full((rows, 1), -3e38, F32)
        acc_s[...] = jnp.zeros((rows, LANES), F32)

        def sel_body(jt, carry):
            ks = pl.multiple_of(jt * TK_SEL, TK_SEL)
            s = _dot_nt(qg, k_ref[0, pl.ds(ks, TK_SEL), 0:LANES])
            selx = _bdot(sel_q, ex_ref[:, pl.ds(ks, TK_SEL)])
            madd = (selx - 1.0) * (-NEG)
            parts = []
            for c in range(sub):
                d = i - (jt * sub + c)
                b = ws_ref[jnp.clip(d, -1, dt) + 1, g]
                sc_ = s[:, c * LANES:(c + 1) * LANES] + b
                sc_ = sc_.reshape(NSA_HPG, TQ, LANES) + madd[:, c * LANES:(c + 1) * LANES][None]
                parts.append(sc_.reshape(rows, LANES))
            online_update(jnp.concatenate(parts, axis=1),
                          v_ref[0, pl.ds(ks, TK_SEL), g * LANES:(g + 1) * LANES], m_s, acc_s)
            return carry

        lax.fori_loop(0, (t0 + TQ + TK_SEL - 1) // TK_SEL, sel_body, 0)

        m_w[...] = jnp.full((rows, 1), -3e38, F32)
        acc_w[...] = jnp.zeros((rows, LANES), F32)
        nwt = WINDOW // LANES

        def win_tile(c):
            ks = pl.multiple_of(t0 - WINDOW + c * LANES, LANES)
            s = _dot_nt(qg, k_ref[0, pl.ds(ks, LANES), LANES:2 * LANES]) + ww_ref[c, g]
            online_update(s, v_ref[0, pl.ds(ks, LANES), (2 + g) * LANES:(3 + g) * LANES], m_w, acc_w)

        win_tile(nwt)
        for c in range(nwt):
            pl.when(t0 - WINDOW + c * LANES >= 0)(functools.partial(win_tile, c))

        def finish(acc):
            out = acc / pltpu.roll(acc, HEAD_DIM, 1)
            return jnp.where(jnp.concatenate([half] * NSA_HPG, axis=0) == g, out, 0.0)

        o_s = finish(acc_s[...])
        o_w = finish(acc_w[...])
        for h in range(NSA_HPG):
            hd = NSA_HPG * g + h
            rs = slice(h * TQ, (h + 1) * TQ)
            y = (gates[:, 3 * hd:3 * hd + 1] * o_c[rs] + gates[:, 3 * hd + 1:3 * hd + 2] * o_s[rs]
                 + gates[:, 3 * hd + 2:3 * hd + 3] * o_w[rs])
            y = jnp.where(half == g, y, 0.0)
            if hd % 2 != g:
                y = pltpu.roll(y, HEAD_DIM, 1)
            y_heads.append(y)

    o_ref[0] = jnp.concatenate([y_heads[2 * c] + y_heads[2 * c + 1] for c in range(NSA_HEADS // 2)],
                               axis=1).astype(BF16)


def _nsa_attention(q, sm, kc, vc, k_sw, v_sw, tabs, statics):
    B, S, _ = q.shape
    sel_tab, win_tab, cmp_tab, dt = tabs
    ovt, expand = statics
    nq = S // TQ
    ncp = S // CMP_STRIDE
    nb = S // SEL_BLOCK
    rows = NSA_HPG * TQ
    const = lambda nd: (lambda b, i: (0,) * nd)
    return pl.pallas_call(
        functools.partial(_nsa_kernel, n_blocks=nb, n_sel=min(SEL_TOPK, nb), dt=dt),
        grid=(B, nq),
        in_specs=[pl.BlockSpec((1, TQ, NSA_Q), lambda b, i: (b, i, 0)),
                  pl.BlockSpec((1, TQ, LANES), lambda b, i: (b, i, 0)),
                  pl.BlockSpec((1, ncp, LANES), lambda b, i: (b, 0, 0)),
                  pl.BlockSpec((1, ncp, LANES), lambda b, i: (b, 0, 0)),
                  pl.BlockSpec((1, S, 256), lambda b, i: (b, 0, 0)),
                  pl.BlockSpec((1, S, 512), lambda b, i: (b, 0, 0)),
                  pl.BlockSpec((1, NSA_GROUPS, rows, ncp), lambda b, i: (i, 0, 0, 0)),
                  pl.BlockSpec(sel_tab.shape, const(4)),
                  pl.BlockSpec(win_tab.shape, const(4)),
                  pl.BlockSpec(ovt.shape, const(2)),
                  pl.BlockSpec(expand.shape, const(2))],
        out_specs=pl.BlockSpec((1, TQ, NSA_Q), lambda b, i: (b, i, 0)),
        out_shape=jax.ShapeDtypeStruct((B, S, NSA_Q), BF16),
        scratch_shapes=[pltpu.VMEM((rows, 1), F32), pltpu.VMEM((rows, LANES), F32),
                        pltpu.VMEM((rows, 1), F32), pltpu.VMEM((rows, LANES), F32)],
        compiler_params=_cparams(("arbitrary", "arbitrary")),
        name="nsa_attention",
    )(q, sm, kc, vc, k_sw, v_sw, cmp_tab, sel_tab, win_tab, ovt, expand)


def _tri_inverse(low):
    c = low.shape[0]

    def mm(a, b):
        ah, am, _ = _split3(a)
        bh, bm, _ = _split3(b)
        return _bdot(ah, bh) + (_bdot(ah, bm) + _bdot(am, bh))

    eye = (lax.broadcasted_iota(jnp.int32, (c, c), 0) == lax.broadcasted_iota(jnp.int32, (c, c), 1)).astype(F32)
    x = eye - low
    p = mm(low, low)
    steps = int(math.log2(c)) - 1
    for s in range(steps):
        x = x + mm(x, p)
        if s + 1 < steps:
            p = mm(p, p)
    return x


def _dn_kernel(qkv_ref, sm_ref, z_ref, cw_ref, alog_ref, dtb_ref, ng_ref, o_ref,
               xbuf, state, q_s, k_s, v_s, b_s, gc_s, out_s):
    j = pl.program_id(1)
    ct = CT_DN
    c = DN_CHUNK

    @pl.when(j == 0)
    def _():
        xbuf[0:8, :] = jnp.zeros((8, 3 * DN_W), F32)
        state[...] = jnp.zeros(state.shape, F32)

    xbuf[8:8 + ct, :] = qkv_ref[0]
    acc = cw_ref[0:1, :] * xbuf[5:5 + ct, :]
    for tap in range(1, DN_CONV):
        acc = acc + cw_ref[tap:tap + 1, :] * xbuf[5 + tap:5 + tap + ct, :]
    xbuf[0:8, :] = xbuf[ct:ct + 8, :]
    y = jax.nn.silu(acc)
    qh = y[:, 0:DN_W]
    kh = y[:, DN_W:2 * DN_W]
    q_s[...] = qh * lax.rsqrt(_seg64_sumsq(qh) + NORM_EPS) * (HEAD_DIM ** -0.5)
    k_s[...] = kh * lax.rsqrt(_seg64_sumsq(kh) + NORM_EPS)
    v_s[...] = y[:, 2 * DN_W:3 * DN_W]
    sm = sm_ref[0]
    b_s[...] = jax.nn.sigmoid(sm)
    gdec = -jnp.exp(alog_ref[...]) * jax.nn.softplus(sm + dtb_ref[...])

    tri_r = lax.broadcasted_iota(jnp.int32, (c, c), 0)
    tri_c = lax.broadcasted_iota(jnp.int32, (c, c), 1)
    causal = tri_r >= tri_c
    strict = tri_r > tri_c
    tril = causal.astype(BF16)
    for ch in range(ct // c):
        g_hi, g_mid, g_lo = _split3(gdec[ch * c:(ch + 1) * c])
        gc_s[ch * c:(ch + 1) * c, :] = _bdot(tril, g_hi) + (_bdot(tril, g_mid) + _bdot(tril, g_lo))

    def chunk(ch, carry):
        r0 = pl.multiple_of(ch * c, c)
        gc_all = gc_s[pl.ds(r0, c), :]
        gc_t = jnp.transpose(gc_all)
        beta_all = b_s[pl.ds(r0, c), :]
        for h in range(DN_HEADS):
            ls = slice(h * HEAD_DIM, (h + 1) * HEAD_DIM)
            qh_ = q_s[pl.ds(r0, c), ls]
            kh_ = k_s[pl.ds(r0, c), ls]
            vh_ = v_s[pl.ds(r0, c), ls]
            beta = beta_all[:, SM_BETA + h:SM_BETA + h + 1]
            gcol = gc_all[:, SM_A + h:SM_A + h + 1]
            grow = gc_t[SM_A + h:SM_A + h + 1, :]
            decay = jnp.where(causal, jnp.exp(jnp.where(causal, gcol - grow, 0.0)), 0.0)
            eg = jnp.exp(gcol)
            kb = kh_ * beta
            kbf = kh_.astype(BF16)
            low = jnp.where(strict, _dot_nt(kb.astype(BF16), kbf) * decay, 0.0)
            t_inv = _tri_inverse(low).astype(BF16)
            u = _bdot(t_inv, (vh_ * beta).astype(BF16))
            w = _bdot(t_inv, (kb * eg).astype(BF16))
            attn = jnp.where(causal, _dot_nt(qh_.astype(BF16), kbf) * decay, 0.0)
            g_last = gcol[c - 1:c, :]
            st = state[h]
            stb = st.astype(BF16)
            v_new = u - _bdot(w.astype(BF16), stb)
            out = _bdot((qh_ * eg).astype(BF16), stb) + _bdot(attn.astype(BF16), v_new.astype(BF16))
            k_dec = kh_ * jnp.exp(g_last - gcol)
            state[h] = st * jnp.exp(g_last) + _dot_tn(k_dec.astype(BF16), v_new.astype(BF16))
            out_s[pl.ds(r0, c), ls] = out
        return carry

    lax.fori_loop(0, ct // c, chunk, 0)

    o = out_s[...]
    o = o * lax.rsqrt(_seg64_sumsq(o) * (1.0 / HEAD_DIM) + NORM_EPS) * ng_ref[...]
    o_ref[0] = (o * jax.nn.silu(z_ref[0])).astype(BF16)


def _deltanet(dn_qkv, sm, z, conv_w, alog128, dtb128, ng512):
    B, S, _ = dn_qkv.shape
    ct = CT_DN
    return pl.pallas_call(
        _dn_kernel,
        grid=(B, S // ct),
        in_specs=[pl.BlockSpec((1, ct, 3 * DN_W), lambda b, j: (b, j, 0)),
                  pl.BlockSpec((1, ct, LANES), lambda b, j: (b, j, 0)),
                  pl.BlockSpec((1, ct, DN_W), lambda b, j: (b, j, 0)),
                  pl.BlockSpec((DN_CONV, 3 * DN_W), lambda b, j: (0, 0)),
                  pl.BlockSpec((1, LANES), lambda b, j: (0, 0)),
                  pl.BlockSpec((1, LANES), lambda b, j: (0, 0)),
                  pl.BlockSpec((1, DN_W), lambda b, j: (0, 0))],
        out_specs=pl.BlockSpec((1, ct, DN_W), lambda b, j: (b, j, 0)),
        out_shape=jax.ShapeDtypeStruct((B, S, DN_W), BF16),
        scratch_shapes=[pltpu.VMEM((ct + 8, 3 * DN_W), F32),
                        pltpu.VMEM((DN_HEADS, HEAD_DIM, HEAD_DIM), F32),
                        pltpu.VMEM((ct, DN_W), F32), pltpu.VMEM((ct, DN_W), F32), pltpu.VMEM((ct, DN_W), F32),
                        pltpu.VMEM((ct, LANES), F32), pltpu.VMEM((ct, LANES), F32),
                        pltpu.VMEM((ct, DN_W), F32)],
        compiler_params=_cparams(("arbitrary", "arbitrary")),
        name="gated_deltanet",
    )(dn_qkv, sm, z, conv_w, alog128, dtb128, ng512)


def _merge_kernel(ya_ref, yb_ref, mg_ref, x_ref, mod_ref, wa_ref, wb_ref, wo_ref, ng_ref, rw_ref,
                  xo_ref, hp_ref, lg_ref):
    d = D_MODEL
    m = jax.nn.sigmoid(mg_ref[...])
    y = m[:, :d] * _bdot(ya_ref[...], wa_ref[...]) + m[:, d:] * _bdot(yb_ref[...], wb_ref[...])
    xn = x_ref[...] + mod_ref[0, 2:3, :] * _bdot(y.astype(BF16), wo_ref[...])
    xo_ref[...] = xn
    ms = jnp.mean(xn * xn, axis=-1, keepdims=True)
    h = xn * lax.rsqrt(ms + NORM_EPS) * ng_ref[...]
    h = h * (1.0 + mod_ref[0, 4:5, :]) + mod_ref[0, 3:4, :]
    lg_ref[...] = jnp.dot(h, rw_ref[...], preferred_element_type=F32, precision=lax.Precision.HIGHEST)
    bits = pltpu.bitcast(h.astype(BF16).astype(F32), jnp.uint32)
    hp_ref[...] = (bits[:, :d // 2] & jnp.uint32(0xFFFF0000)) | (bits[:, d // 2:] >> 16)


def _merge(ya, yb, mg, x2, mod_l, wa, wb, wo, norm_g, router_w128, seq):
    T, D = x2.shape
    tm = TM_MERGE
    tpb = seq // tm
    row = lambda w: pl.BlockSpec((tm, w), lambda i: (i, 0))
    full = lambda a: pl.BlockSpec(a.shape, lambda i: (0,) * a.ndim)
    return pl.pallas_call(
        _merge_kernel,
        grid=(T // tm,),
        in_specs=[row(NSA_Q), row(DN_W), row(2 * D), row(D),
                  pl.BlockSpec((1, 6, D), lambda i: (i // tpb, 0, 0)),
                  full(wa), full(wb), full(wo), full(norm_g), full(router_w128)],
        out_specs=[row(D), row(D // 2), row(LANES)],
        out_shape=[jax.ShapeDtypeStruct((T, D), F32), jax.ShapeDtypeStruct((T, D // 2), jnp.uint32),
                   jax.ShapeDtypeStruct((T, LANES), F32)],
        compiler_params=_cparams(("arbitrary",)),
        name="merge_out",
    )(ya, yb, mg, x2, mod_l, wa, wb, wo, norm_g, router_w128)


def _moe_rows(tm):
    return -(-(TOPK * tm + N_EXPERTS * (ROW_ALIGN - 1) + BLK_MOE) // ROW_ALIGN) * ROW_ALIGN


def _moe_route(logits, router_b, n_tiles, tm):
    T = logits.shape[0]
    scores = jax.nn.sigmoid(logits)
    sel = scores + router_b.astype(F32)
    grp = lax.top_k(sel.reshape(T, N_GROUPS, EXPERTS_PER_GROUP), 2)[0].sum(-1)
    _, top_grp = lax.top_k(grp, 1)
    grp_mask = jnp.any(top_grp[..., None] == jnp.arange(N_GROUPS), axis=1)
    masked = jnp.where(jnp.repeat(grp_mask, EXPERTS_PER_GROUP, axis=1), sel, -jnp.inf)
    _, e_idx = lax.top_k(masked, TOPK)
    wts = jnp.take_along_axis(scores, e_idx, axis=1)
    wts = wts / jnp.sum(wts, axis=-1, keepdims=True)

    na = TOPK * tm
    ea = e_idx.reshape(n_tiles, na)
    onehot = (ea[..., None] == jnp.arange(N_EXPERTS)).astype(jnp.int32)
    csum = jnp.cumsum(onehot, axis=1)
    rank = jnp.take_along_axis(csum - onehot, ea[..., None], axis=2)[..., 0]
    counts = csum[:, -1, :]
    padded = (counts + ROW_ALIGN - 1) // ROW_ALIGN * ROW_ALIGN
    start = jnp.cumsum(padded, axis=1) - padded
    slot = jnp.take_along_axis(start, ea, axis=1) + rank
    src = jnp.zeros((n_tiles, _moe_rows(tm)), jnp.int32)
    src = src.at[jnp.arange(n_tiles)[:, None], slot].set(jnp.arange(na, dtype=jnp.int32)[None, :] // TOPK)
    pos = slot.reshape(n_tiles, tm, TOPK).transpose(0, 2, 1).reshape(n_tiles, TOPK * tm)
    wt = wts.reshape(n_tiles, tm, TOPK).transpose(0, 2, 1).reshape(n_tiles, TOPK * tm)
    seg = jnp.concatenate([start, (counts + BLK_MOE - 1) // BLK_MOE], axis=1).astype(jnp.int32)
    return src, pos.astype(jnp.int32), wt.astype(F32), seg


def _moe_kernel(src_ref, pos_ref, wt_ref, seg_ref, hp_ref, x_ref, mod_ref, wg_ref, wu_ref, wd_ref,
                o_ref, gbuf, ybuf, *, tm, n_rows):
    e = pl.program_id(1)
    half = D_MODEL // 2

    @pl.when(e == 0)
    def _():
        def gather(r, carry):
            gbuf[pl.ds(r, 1), :] = hp_ref[pl.ds(src_ref[0, 0, r], 1), :]
            return carry
        lax.fori_loop(0, n_rows, gather, 0, unroll=8)

    start = seg_ref[0, 0, e]

    def chunk(ci, carry):
        r0 = pl.multiple_of(start + ci * BLK_MOE, ROW_ALIGN)
        w = gbuf[pl.ds(r0, BLK_MOE), :]
        x_hi = pltpu.bitcast(w & jnp.uint32(0xFFFF0000), F32).astype(BF16)
        x_lo = pltpu.bitcast(w << 16, F32).astype(BF16)
        gt = _bdot(x_hi, wg_ref[0, 0, 0:half, :]) + _bdot(x_lo, wg_ref[0, 0, half:, :])
        up = _bdot(x_hi, wu_ref[0, 0, 0:half, :]) + _bdot(x_lo, wu_ref[0, 0, half:, :])
        act = (jax.nn.silu(gt) * up).astype(BF16)
        ybuf[pl.ds(r0, BLK_MOE), :] = _bdot(act, wd_ref[0, 0])
        return carry

    lax.fori_loop(0, seg_ref[0, 0, N_EXPERTS + e], chunk, 0)

    @pl.when(e == N_EXPERTS - 1)
    def _():
        def combine(t, carry):
            y0 = ybuf[pl.ds(pos_ref[0, 0, t], 1), :] * wt_ref[0, 0, t]
            y1 = ybuf[pl.ds(pos_ref[0, 0, tm + t], 1), :] * wt_ref[0, 0, tm + t]
            o_ref[pl.ds(t, 1), :] = y0 + y1
            return carry
        lax.fori_loop(0, tm, combine, 0, unroll=8)
        o_ref[...] = x_ref[...] + mod_ref[0, 5:6, :] * o_ref[...]


def _moe(hp, x2, mod_l, route, wg, wu, wd, layer, seq):
    T, D = x2.shape
    tm = TM_MOE
    nt = T // tm
    tpb = seq // tm
    n_rows = _moe_rows(tm)
    src, pos, wt, seg = route
    smem = lambda w: pl.BlockSpec((1, 1, w), lambda i, e: (i, 0, 0), memory_space=pltpu.SMEM)
    src, pos, wt, seg = (a[:, None, :] for a in (src, pos, wt, seg))
    return pl.pallas_call(
        functools.partial(_moe_kernel, tm=tm, n_rows=n_rows),
        grid=(nt, N_EXPERTS),
        in_specs=[smem(n_rows), smem(TOPK * tm), smem(TOPK * tm), smem(2 * N_EXPERTS),
                  pl.BlockSpec((tm, D // 2), lambda i, e: (i, 0)),
                  pl.BlockSpec((tm, D), lambda i, e: (i, 0)),
                  pl.BlockSpec((1, 6, D), lambda i, e: (i // tpb, 0, 0)),
                  pl.BlockSpec((1, 1, D, D_EXPERT), lambda i, e: (layer, e, 0, 0)),
                  pl.BlockSpec((1, 1, D, D_EXPERT), lambda i, e: (layer, e, 0, 0)),
                  pl.BlockSpec((1, 1, D_EXPERT, D), lambda i, e: (layer, e, 0, 0))],
        out_specs=pl.BlockSpec((tm, D), lambda i, e: (i, 0)),
        out_shape=jax.ShapeDtypeStruct((T, D), F32),
        scratch_shapes=[pltpu.VMEM((n_rows, D // 2), jnp.uint32), pltpu.VMEM((n_rows, D), F32)],
        compiler_params=_cparams(("arbitrary", "arbitrary")),
        name="moe_ffn",
    )(src, pos, wt, seg, hp, x2, mod_l, wg, wu, wd)


def _pad_lanes(v, offset):
    return jnp.zeros((1, LANES), F32).at[0, offset:offset + v.shape[0]].set(v.astype(F32))


def kernel(x, c, rel_bias, router_w, router_b, ada_w, ada_b, norm1_g, norm2_g, w_in, qk_norm_g,
           cmp_pos, cmp_w1, cmp_w2, dn_conv_w, dn_a_log, dn_dt_bias, dn_norm_g, w_branch_a,
           w_branch_b, w_out, moe_w_gate, moe_w_up, moe_w_down):
    B, S, D = x.shape
    T = B * S
    L = ada_w.shape[0]

    mod = _ada_mod(c, ada_w, ada_b)
    cols = _proj_columns()
    w_proj = jnp.where(jnp.asarray(cols >= 0)[None, None, :],
                       jnp.take(w_in, jnp.asarray(np.maximum(cols, 0)), axis=2), 0.0).astype(BF16)
    tabs = _nsa_tables(rel_bias, S)
    statics = _nsa_static(S)
    router_w128 = jnp.zeros((D, LANES), F32).at[:, :N_EXPERTS].set(router_w.astype(F32))
    wg, wu, wd = moe_w_gate.astype(BF16), moe_w_up.astype(BF16), moe_w_down.astype(BF16)
    n_moe_tiles = T // TM_MOE

    x2 = x.reshape(T, D)
    for l in range(L):
        qkg = qk_norm_g[l].astype(F32)
        q_gain = jnp.tile(qkg[0], NSA_HEADS)[None, :]
        k_gain = jnp.concatenate([jnp.tile(qkg[2], NSA_GROUPS), jnp.tile(qkg[3], NSA_GROUPS)])[None, :]
        q, cmp_raw, k_sw, v_sw, dn_qkv, z, mg, sm = _in_proj(
            x2, mod[l], norm1_g[l][None, :], w_proj[l], q_gain, k_gain, S)

        pos128 = jnp.tile(cmp_pos[l].astype(F32), (1, 1, NSA_GROUPS))
        kc, vc = _compress(cmp_raw.reshape(B, S, 256), pos128, cmp_w1[l].astype(BF16),
                           cmp_w2[l].astype(BF16), qkg[1][None, :])
        y_a = _nsa_attention(q.reshape(B, S, NSA_Q), sm.reshape(B, S, LANES), kc, vc,
                             k_sw.reshape(B, S, 256), v_sw.reshape(B, S, 512), tabs, statics)

        y_b = _deltanet(dn_qkv.reshape(B, S, 3 * DN_W), sm.reshape(B, S, LANES), z.reshape(B, S, DN_W),
                        dn_conv_w[l].astype(F32), _pad_lanes(dn_a_log[l], SM_A),
                        _pad_lanes(dn_dt_bias[l], SM_A), jnp.tile(dn_norm_g[l].astype(F32), DN_HEADS)[None, :])

        x_mid, hp, logits = _merge(y_a.reshape(T, NSA_Q), y_b.reshape(T, DN_W), mg, x2, mod[l],
                                   w_branch_a[l].astype(BF16), w_branch_b[l].astype(BF16),
                                   w_out[l].astype(BF16), norm2_g[l][None, :], router_w128, S)
        route = _moe_route(logits[:, :N_EXPERTS], router_b, n_moe_tiles, TM_MOE)
        x2 = _moe(hp, x_mid, mod[l], route, wg, wu, wd, l, S)
    return x2.reshape(B, S, D)
```

```python
import functools
import math

import numpy as np
import jax
import jax.numpy as jnp
from jax import lax
from jax.experimental import pallas as pl
from jax.experimental.pallas import tpu as pltpu

F32 = jnp.float32
BF16 = jnp.bfloat16

D_MODEL = 1024
DEPTH = 4
HEAD_DIM = 64
NSA_HEADS = 8
NSA_GROUPS = 2
NSA_HPG = NSA_HEADS // NSA_GROUPS
CMP_STRIDE = 16
CMP_BLOCK = 32
CMP_HIDDEN = 256
SEL_BLOCK = 64
SEL_TOPK = 16
SEL_LOCAL = 2
WINDOW = 512
DN_HEADS = 8
DN_CONV = 4
DN_CHUNK = 64
REL_BUCKETS = 32
REL_MAX_DIST = 1024
N_EXPERTS = 16
N_GROUPS = 4
EXPERTS_PER_GROUP = N_EXPERTS // N_GROUPS
TOPK = 2
D_EXPERT = 512
NORM_EPS = 1e-6
FORCE_SCORE = 1e9
NEG = -1e30

NSA_Q = NSA_HEADS * HEAD_DIM
NSA_KV = NSA_GROUPS * HEAD_DIM
DN_W = DN_HEADS * HEAD_DIM
IN_SIZES = (NSA_Q, 6 * NSA_KV, 3 * NSA_HEADS, 3 * DN_W, DN_HEADS, DN_HEADS, DN_W, 2 * D_MODEL)
IN_OFFS = tuple(int(v) for v in np.cumsum((0,) + IN_SIZES))

LANES = 128
VMEM_LIMIT = 56 * 1024 * 1024

TM_PROJ = 512
TM_MERGE = 512
TQ = 128
TK_SEL = 512
CT_DN = 256
TM_MOE = 1024
BLK_MOE = 128
ROW_ALIGN = 8

_SEG_Q = (0, 512)
_SEG_CMP = (512, 768)
_SEG_K = (768, 1024)
_SEG_V = (1024, 1536)
_SEG_DN = (1536, 3072)
_SEG_Z = (3072, 3584)
_SEG_MG = (3584, 5632)
_SEG_SM = (5632, 5760)
N_PROJ = 5760
SM_GATE, SM_BETA, SM_A = 0, 24, 32


def _proj_weight(w_in):
    o = IN_OFFS
    kv = o[1]
    zeros = lambda n: jnp.zeros(w_in.shape[:2] + (n,), w_in.dtype)
    cut = lambda a, b: w_in[:, :, a:b]
    parts = [cut(o[0], o[1]),
             cut(kv, kv + 256),
             cut(kv + 256, kv + 384), cut(kv + 512, kv + 640)]
    for base in (kv + 384, kv + 640):
        parts += [cut(base, base + 64), zeros(128), cut(base + 64, base + 128)]
    parts += [cut(o[3], o[4]), cut(o[6], o[7]), cut(o[7], o[8]),
              cut(o[2], o[3]), cut(o[4], o[5]), cut(o[5], o[6])]
    width = sum(p.shape[2] for p in parts)
    parts.append(zeros(N_PROJ - width))
    return jnp.concatenate(parts, axis=2).astype(BF16)


def _rel_bucket_table(n):
    exact = REL_BUCKETS // 2
    d = np.arange(n, dtype=np.int64)
    far = np.maximum(d, exact).astype(np.float64)
    large = exact + (np.log(far / exact) / math.log(REL_MAX_DIST / exact) * (REL_BUCKETS - exact)).astype(np.int64)
    return np.where(d < exact, d, np.minimum(large, REL_BUCKETS - 1)).astype(np.int32)


def _cparams(sem, vmem=VMEM_LIMIT):
    return pltpu.CompilerParams(dimension_semantics=sem, vmem_limit_bytes=vmem)


def _bdot(a, b):
    return jnp.dot(a, b, preferred_element_type=F32)


def _dot_nt(a, b):
    return lax.dot_general(a, b, (((1,), (1,)), ((), ())), preferred_element_type=F32)


def _dot_tn(a, b):
    return lax.dot_general(a, b, (((0,), (0,)), ((), ())), preferred_element_type=F32)


def _split3(x):
    h = x.astype(BF16)
    r = x - h.astype(F32)
    m = r.astype(BF16)
    l = (r - m.astype(F32)).astype(BF16)
    return h, m, l


def _seg64_sumsq(x):
    rows, width = x.shape
    low = lax.broadcasted_iota(jnp.int32, (rows, LANES), 1) < 64
    outs = []
    for c in range(width // LANES):
        sq = x[:, c * LANES:(c + 1) * LANES]
        sq = sq * sq
        s_lo = jnp.sum(jnp.where(low, sq, 0.0), axis=-1, keepdims=True)
        s_hi = jnp.sum(jnp.where(low, 0.0, sq), axis=-1, keepdims=True)
        outs.append(jnp.where(low, s_lo, s_hi))
    return outs[0] if len(outs) == 1 else jnp.concatenate(outs, axis=1)


def _ada_kernel(c_ref, w_ref, b_ref, o_ref):
    ca = jax.nn.silu(c_ref[...]).astype(BF16)
    o_ref[0] = _bdot(ca, w_ref[0].astype(BF16)) + b_ref[0]


def _ada_mod(c, ada_w, ada_b):
    L, D, N = ada_w.shape
    B = c.shape[0]
    tn = 1536
    out = pl.pallas_call(
        _ada_kernel,
        grid=(L, N // tn),
        in_specs=[pl.BlockSpec((B, D), lambda l, j: (0, 0)),
                  pl.BlockSpec((1, D, tn), lambda l, j: (l, 0, j)),
                  pl.BlockSpec((1, 1, tn), lambda l, j: (l, 0, j))],
        out_specs=pl.BlockSpec((1, B, tn), lambda l, j: (l, 0, j)),
        out_shape=jax.ShapeDtypeStruct((L, B, N), F32),
        compiler_params=_cparams(("arbitrary", "arbitrary")),
        name="ada_mod",
    )(c, ada_w, ada_b.reshape(L, 1, N))
    return out.reshape(L, B, 6, D)


def _inproj_kernel(x_ref, mod_ref, ng_ref, w_ref, qg_ref, kg_ref,
                   q_out, cmp_out, k_out, v_out, dn_out, z_out, mg_out, sm_out):
    x = x_ref[...]
    ms = jnp.mean(x * x, axis=-1, keepdims=True)
    h = x * lax.rsqrt(ms + NORM_EPS) * ng_ref[...]
    h = h * (1.0 + mod_ref[0, 1:2, :]) + mod_ref[0, 0:1, :]
    hb = h.astype(BF16)

    def seg(ab):
        return _bdot(hb, w_ref[:, ab[0]:ab[1]])

    q = seg(_SEG_Q)
    q = q * lax.rsqrt(_seg64_sumsq(q) * (1.0 / HEAD_DIM) + NORM_EPS) * qg_ref[...]
    q_out[...] = (q * (HEAD_DIM ** -0.5)).astype(BF16)
    cmp_out[...] = seg(_SEG_CMP)
    k = seg(_SEG_K)
    k = k * lax.rsqrt(_seg64_sumsq(k) * (1.0 / HEAD_DIM) + NORM_EPS) * kg_ref[...]
    k_out[...] = k.astype(BF16)
    v = seg(_SEG_V)
    lane = lax.broadcasted_iota(jnp.int32, v.shape, 1) % 256
    ones = jnp.where((lane >= 64) & (lane < 192), 1.0, 0.0)
    v_out[...] = (v + ones).astype(BF16)
    dn_out[...] = seg(_SEG_DN)
    z_out[...] = seg(_SEG_Z)
    mg_out[...] = seg(_SEG_MG)
    sm_out[...] = seg(_SEG_SM)


def _in_proj(x2, mod_l, norm_g, w_proj, q_gain, k_gain, seq):
    T, D = x2.shape
    tm = TM_PROJ
    tpb = seq // tm
    widths = [(512, BF16), (256, F32), (256, BF16), (512, BF16), (1536, F32), (512, F32), (2048, F32), (128, F32)]
    return pl.pallas_call(
        _inproj_kernel,
        grid=(T // tm,),
        in_specs=[pl.BlockSpec((tm, D), lambda i: (i, 0)),
                  pl.BlockSpec((1, 6, D), lambda i: (i // tpb, 0, 0)),
                  pl.BlockSpec((1, D), lambda i: (0, 0)),
                  pl.BlockSpec((D, N_PROJ), lambda i: (0, 0)),
                  pl.BlockSpec((1, 512), lambda i: (0, 0)),
                  pl.BlockSpec((1, 256), lambda i: (0, 0))],
        out_specs=[pl.BlockSpec((tm, w), lambda i: (i, 0)) for w, _ in widths],
        out_shape=[jax.ShapeDtypeStruct((T, w), dt) for w, dt in widths],
        compiler_params=_cparams(("arbitrary",)),
        name="in_proj",
    )(x2, mod_l, norm_g, w_proj, q_gain, k_gain)


def _compress_kernel(kraw_ref, vraw_ref, pos_ref, w1_ref, w2_ref, kg_ref, kc_out, vc_out, *, n_chunks):
    for j, raw_ref, out_ref in ((0, kraw_ref, kc_out), (1, vraw_ref, vc_out)):
        top = [jnp.zeros((n_chunks, CMP_HIDDEN), F32) for _ in range(NSA_GROUPS)]
        bot = [jnp.zeros((n_chunks, CMP_HIDDEN), F32) for _ in range(NSA_GROUPS)]
        for r in range(CMP_STRIDE):
            xr = raw_ref[0, pl.ds(r, n_chunks, stride=CMP_STRIDE), :]
            x_top = (xr + pos_ref[j, r:r + 1, :]).astype(BF16)
            x_bot = (xr + pos_ref[j, CMP_STRIDE + r:CMP_STRIDE + r + 1, :]).astype(BF16)
            for g in range(NSA_GROUPS):
                ls = slice(g * HEAD_DIM, (g + 1) * HEAD_DIM)
                top[g] = top[g] + _bdot(x_top[:, ls], w1_ref[j, r * HEAD_DIM:(r + 1) * HEAD_DIM, :])
                bot[g] = bot[g] + _bdot(x_bot[:, ls], w1_ref[j, (CMP_STRIDE + r) * HEAD_DIM:(CMP_STRIDE + r + 1) * HEAD_DIM, :])
        outs = []
        for g in range(NSA_GROUPS):
            hid = top[g] + pltpu.roll(bot[g], n_chunks - 1, 0)
            o = _bdot(jax.nn.gelu(hid).astype(BF16), w2_ref[j])
            if j == 0:
                ms = jnp.mean(o * o, axis=-1, keepdims=True)
                o = o * lax.rsqrt(ms + NORM_EPS) * kg_ref[...]
            outs.append(o)
        out_ref[0] = jnp.concatenate(outs, axis=1).astype(BF16)


def _compress(cmp_raw, pos128, w1, w2, k_gain):
    B, S, _ = cmp_raw.shape
    nc = S // CMP_STRIDE
    return pl.pallas_call(
        functools.partial(_compress_kernel, n_chunks=nc),
        grid=(B,),
        in_specs=[pl.BlockSpec((1, S, LANES), lambda b: (b, 0, 0)),
                  pl.BlockSpec((1, S, LANES), lambda b: (b, 0, 1)),
                  pl.BlockSpec((2, CMP_BLOCK, LANES), lambda b: (0, 0, 0)),
                  pl.BlockSpec((2, CMP_BLOCK * HEAD_DIM, CMP_HIDDEN), lambda b: (0, 0, 0)),
                  pl.BlockSpec((2, CMP_HIDDEN, HEAD_DIM), lambda b: (0, 0, 0)),
                  pl.BlockSpec((1, HEAD_DIM), lambda b: (0, 0))],
        out_specs=[pl.BlockSpec((1, nc, LANES), lambda b: (b, 0, 0))] * 2,
        out_shape=[jax.ShapeDtypeStruct((B, nc, LANES), BF16)] * 2,
        compiler_params=_cparams(("arbitrary",)),
        name="nsa_compress",
    )(cmp_raw, cmp_raw, pos128, w1, w2, k_gain)


def _nsa_tables(rel_bias, seq):
    nq = seq // TQ
    ncp = seq // CMP_STRIDE
    bucket = jnp.asarray(_rel_bucket_table(seq + WINDOW + TQ))
    rb = rel_bias.astype(F32)
    q = np.arange(TQ)[:, None]
    k = np.arange(LANES)[None, :]

    def lookup(dist, valid):
        b = bucket[np.clip(dist, 0, None)]
        vals = rb[b]
        vals = jnp.where(jnp.asarray(valid)[..., None], vals, NEG)
        vals = jnp.moveaxis(vals, -1, 0)
        return vals.reshape(NSA_GROUPS, NSA_HPG * dist.shape[0], dist.shape[1])

    dlim = int(np.argmax(_rel_bucket_table(seq + WINDOW + TQ) == REL_BUCKETS - 1))
    dt = min(nq - 1, -(-(dlim + TQ - 1) // TQ))
    sel = [jnp.full((NSA_GROUPS, NSA_HPG * TQ, LANES), NEG, F32)]
    for d in range(dt + 1):
        dist = TQ * d + q - k
        sel.append(lookup(dist, dist >= 0))
    sel_tab = jnp.stack(sel)
    win = []
    for c in range(WINDOW // LANES + 1):
        dist = q + WINDOW - LANES * c - k
        win.append(lookup(dist, (dist >= 0) & (dist < WINDOW)))
    win_tab = jnp.stack(win)
    off = CMP_STRIDE * (ncp - 1) + CMP_BLOCK - 1
    by_dist = jnp.concatenate([jnp.full((NSA_HEADS, off), NEG, F32), rb[bucket[:seq]].T], axis=1)
    cols = [by_dist[:, off - (CMP_STRIDE * n + CMP_BLOCK - 1):][:, :seq] for n in range(ncp - 1)]
    cols.append(jnp.full((NSA_HEADS, seq), NEG, F32))
    cmp_tab = jnp.stack(cols, axis=-1)
    cmp_tab = cmp_tab.reshape(NSA_GROUPS, NSA_HPG, nq, TQ, ncp).transpose(2, 0, 1, 3, 4)
    cmp_tab = cmp_tab.reshape(nq, NSA_GROUPS, NSA_HPG * TQ, ncp)
    return sel_tab, win_tab, cmp_tab, dt


def _nsa_static(seq):
    nb = seq // SEL_BLOCK
    ncp = seq // CMP_STRIDE
    ratio, nsub = SEL_BLOCK // CMP_STRIDE, CMP_BLOCK // CMP_STRIDE
    delta = np.arange(ncp)[:, None] - ratio * np.arange(nb)[None, :]
    m_idx = delta[..., None] + np.arange(nsub)
    overlap = np.sum((m_idx >= 0) & (m_idx < ratio), axis=-1).astype(np.float32)
    overlap[ncp - 1, :] = 0.0
    expand = (np.arange(nb)[:, None] == (np.arange(seq)[None, :] // SEL_BLOCK)).astype(np.float32)
    return jnp.asarray(overlap.T, BF16), jnp.asarray(expand, BF16)


def _nsa_kernel(q_ref, sm_ref, kc_ref, vc_ref, k_ref, v_ref, bc_ref, ws_ref, ww_ref, ovt_ref, ex_ref,
                o_ref, m_s, acc_s, *, n_blocks, n_sel, dt):
    i = pl.program_id(1)
    t0 = i * TQ
    rows = NSA_HPG * TQ
    half = lax.broadcasted_iota(jnp.int32, (TQ, LANES), 1) // HEAD_DIM
    q = q_ref[0].astype(F32)
    gates = jax.nn.sigmoid(sm_ref[0])
    jb = lax.broadcasted_iota(jnp.int32, (n_blocks, TQ), 0)
    cur = (t0 + lax.broadcasted_iota(jnp.int32, (n_blocks, TQ), 1)) // SEL_BLOCK
    valid = jb <= cur
    forced = valid & ((jb == 0) | (jb > cur - SEL_LOCAL))
    sub = TK_SEL // LANES
    y_heads = []

    def block_max(blocks):
        bm = blocks[0]
        for b in blocks[1:]:
            bm = jnp.maximum(bm, b)
        return jnp.max(bm, axis=-1, keepdims=True)

    def online_update(blocks, v_tile, m_ref, acc_ref):
        m_old = m_ref[...]
        m_new = jnp.maximum(m_old, block_max(blocks))
        p = jnp.concatenate([jnp.exp(b - m_new) for b in blocks], axis=1).astype(BF16)
        acc_ref[...] = jnp.exp(m_old - m_new) * acc_ref[...] + _bdot(p, v_tile)
        m_ref[...] = m_new

    for g in range(NSA_GROUPS):
        stack = []
        for h in range(NSA_HPG):
            hd = NSA_HPG * g + h
            blk = q[:, (hd // 2) * LANES:(hd // 2 + 1) * LANES]
            if hd % 2 != g:
                blk = pltpu.roll(blk, HEAD_DIM, 1)
            stack.append(jnp.where(half == g, blk, 0.0))
        qg = jnp.concatenate(stack, axis=0).astype(BF16)

        bc = bc_ref[0, g]
        sc = _dot_nt(qg, kc_ref[0]) + bc
        ec = jnp.exp(sc - jnp.max(sc, axis=-1, keepdims=True))
        ec = jnp.where(bc > 0.5 * NEG, ec, 0.0)
        lc = jnp.sum(ec, axis=-1, keepdims=True)
        pc = ec / jnp.where(lc > 0.0, lc, 1.0)
        o_c = _bdot(pc.astype(BF16), vc_ref[0])

        psum = pc[0:TQ] + pc[TQ:2 * TQ] + pc[2 * TQ:3 * TQ] + pc[3 * TQ:4 * TQ]
        p_hi = psum.astype(BF16)
        p_lo = (psum - p_hi.astype(F32)).astype(BF16)
        imp = _dot_nt(ovt_ref[...], p_hi) + _dot_nt(ovt_ref[...], p_lo)
        score = jnp.where(forced, FORCE_SCORE, jnp.where(valid, imp, -1.0))

        def pick(_, carry):
            sc_, sel_ = carry
            top = jnp.max(sc_, axis=0, keepdims=True)
            first = jnp.min(jnp.where(sc_ == top, jb, n_blocks), axis=0, keepdims=True)
            hit = jb == first
            return jnp.where(hit, -3e38, sc_), jnp.where(hit, 1.0, sel_)

        _, sel = lax.fori_loop(0, n_sel, pick, (score, jnp.zeros((n_blocks, TQ), F32)))
        sel_q = jnp.transpose(sel).astype(BF16)

        m_s[...] = jnp.full((rows, LANES), -3e38, F32)
        acc_s[...] = jnp.zeros((rows, LANES), F32)

        def sel_body(jt, carry):
            ks = pl.multiple_of(jt * TK_SEL, TK_SEL)
            s = _dot_nt(qg, k_ref[0, pl.ds(ks, TK_SEL), 0:LANES])
            selx = _bdot(sel_q, ex_ref[:, pl.ds(ks, TK_SEL)])
            madd = (selx - 1.0) * (-NEG)
            parts = []
            for c in range(sub):
                d = i - (jt * sub + c)
                b = ws_ref[jnp.clip(d, -1, dt) + 1, g]
                sc_ = s[:, c * LANES:(c + 1) * LANES] + b
                sc_ = sc_.reshape(NSA_HPG, TQ, LANES) + madd[:, c * LANES:(c + 1) * LANES][None]
                parts.append(sc_.reshape(rows, LANES))
            online_update(parts, v_ref[0, pl.ds(ks, TK_SEL), g * LANES:(g + 1) * LANES], m_s, acc_s)
            return carry

        lax.fori_loop(0, (t0 + TQ + TK_SEL - 1) // TK_SEL, sel_body, 0)

        nwt = WINDOW // LANES + 1
        starts = [t0 - WINDOW + c * LANES for c in range(nwt)]
        clamped = [pl.multiple_of(jnp.maximum(ks, 0), LANES) for ks in starts]
        kw = jnp.concatenate([k_ref[0, pl.ds(ks, LANES), LANES:2 * LANES] for ks in clamped], axis=0)
        vw = jnp.concatenate([v_ref[0, pl.ds(ks, LANES), (2 + g) * LANES:(3 + g) * LANES] for ks in clamped], axis=0)
        sw = _dot_nt(qg, kw)
        blocks = [sw[:, c * LANES:(c + 1) * LANES] + jnp.where(starts[c] >= 0, ww_ref[c, g], NEG)
                  for c in range(nwt)]
        mw = block_max(blocks)
        pw = jnp.concatenate([jnp.exp(b - mw) for b in blocks], axis=1).astype(BF16)

        def finish(acc):
            out = acc / pltpu.roll(acc, HEAD_DIM, 1)
            return jnp.where(jnp.concatenate([half] * NSA_HPG, axis=0) == g, out, 0.0)

        o_s = finish(acc_s[...])
        o_w = finish(_bdot(pw, vw))
        for h in range(NSA_HPG):
            hd = NSA_HPG * g + h
            rs = slice(h * TQ, (h + 1) * TQ)
            y = (gates[:, 3 * hd:3 * hd + 1] * o_c[rs] + gates[:, 3 * hd + 1:3 * hd + 2] * o_s[rs]
                 + gates[:, 3 * hd + 2:3 * hd + 3] * o_w[rs])
            y = jnp.where(half == g, y, 0.0)
            if hd % 2 != g:
                y = pltpu.roll(y, HEAD_DIM, 1)
            y_heads.append(y)

    o_ref[0] = jnp.concatenate([y_heads[2 * c] + y_heads[2 * c + 1] for c in range(NSA_HEADS // 2)],
                               axis=1).astype(BF16)


def _nsa_attention(q, sm, kc, vc, k_sw, v_sw, tabs, statics):
    B, S, _ = q.shape
    sel_tab, win_tab, cmp_tab, dt = tabs
    ovt, expand = statics
    nq = S // TQ
    ncp = S // CMP_STRIDE
    nb = S // SEL_BLOCK
    rows = NSA_HPG * TQ
    const = lambda nd: (lambda b, i: (0,) * nd)
    return pl.pallas_call(
        functools.partial(_nsa_kernel, n_blocks=nb, n_sel=min(SEL_TOPK, nb), dt=dt),
        grid=(B, nq),
        in_specs=[pl.BlockSpec((1, TQ, NSA_Q), lambda b, i: (b, i, 0)),
                  pl.BlockSpec((1, TQ, LANES), lambda b, i: (b, i, 0)),
                  pl.BlockSpec((1, ncp, LANES), lambda b, i: (b, 0, 0)),
                  pl.BlockSpec((1, ncp, LANES), lambda b, i: (b, 0, 0)),
                  pl.BlockSpec((1, S, 256), lambda b, i: (b, 0, 0)),
                  pl.BlockSpec((1, S, 512), lambda b, i: (b, 0, 0)),
                  pl.BlockSpec((1, NSA_GROUPS, rows, ncp), lambda b, i: (i, 0, 0, 0)),
                  pl.BlockSpec(sel_tab.shape, const(4)),
                  pl.BlockSpec(win_tab.shape, const(4)),
                  pl.BlockSpec(ovt.shape, const(2)),
                  pl.BlockSpec(expand.shape, const(2))],
        out_specs=pl.BlockSpec((1, TQ, NSA_Q), lambda b, i: (b, i, 0)),
        out_shape=jax.ShapeDtypeStruct((B, S, NSA_Q), BF16),
        scratch_shapes=[pltpu.VMEM((rows, LANES), F32), pltpu.VMEM((rows, LANES), F32)],
        compiler_params=_cparams(("arbitrary", "arbitrary")),
        name="nsa_attention",
    )(q, sm, kc, vc, k_sw, v_sw, cmp_tab, sel_tab, win_tab, ovt, expand)


DN_QUAD = 4
DN_QW = DN_QUAD * HEAD_DIM


def _block_diag(a, head_masks):
    zero = jnp.zeros_like(a)
    return jnp.concatenate([jnp.where(m, a, zero) for m in head_masks], axis=0)


def _quad_tri_inverse(low, eye, head_masks):
    x = eye - low
    lb = low.astype(BF16)
    p = _bdot(lb, _block_diag(lb, head_masks))
    steps = int(math.log2(DN_CHUNK)) - 1
    for s in range(steps):
        pb = p.astype(BF16)
        pd = _block_diag(pb, head_masks)
        x = x + _bdot(x.astype(BF16), pd)
        if s + 1 < steps:
            p = _bdot(pb, pd)
    return x


def _dn_kernel(qkv_ref, sm_ref, z_ref, cw_ref, alog_ref, dtb_ref, ng_ref, eb_ref, ea_ref, o_ref,
               xbuf, state, q_s, k_s, v_s, b_s, gc_s, out_s):
    j = pl.program_id(1)
    ct = CT_DN
    c = DN_CHUNK

    @pl.when(j == 0)
    def _():
        xbuf[0:8, :] = jnp.zeros((8, 3 * DN_W), F32)
        state[...] = jnp.zeros(state.shape, F32)

    xbuf[8:8 + ct, :] = qkv_ref[0]
    acc = cw_ref[0:1, :] * xbuf[5:5 + ct, :]
    for tap in range(1, DN_CONV):
        acc = acc + cw_ref[tap:tap + 1, :] * xbuf[5 + tap:5 + tap + ct, :]
    xbuf[0:8, :] = xbuf[ct:ct + 8, :]
    y = jax.nn.silu(acc)
    qh = y[:, 0:DN_W]
    kh = y[:, DN_W:2 * DN_W]
    q_s[...] = qh * lax.rsqrt(_seg64_sumsq(qh) + NORM_EPS) * (HEAD_DIM ** -0.5)
    k_s[...] = kh * lax.rsqrt(_seg64_sumsq(kh) + NORM_EPS)
    v_s[...] = y[:, 2 * DN_W:3 * DN_W]
    sm = sm_ref[0]
    b_s[...] = sum(_bdot(p, eb_ref[...]) for p in _split3(jax.nn.sigmoid(sm)))
    gdec = -jnp.exp(alog_ref[...]) * jax.nn.softplus(sm + dtb_ref[...])
    g_wide = [_bdot(p, ea_ref[...]).astype(BF16) for p in _split3(gdec)]
    row = lax.broadcasted_iota(jnp.int32, (c, DN_QW), 0)
    col = lax.broadcasted_iota(jnp.int32, (c, DN_QW), 1) % c
    causal = row >= col
    strict = row > col
    diag = row == col
    eye = diag.astype(F32)
    masks = [lax.broadcasted_iota(jnp.int32, (c, DN_QW), 1) // HEAD_DIM == h for h in range(DN_QUAD)]
    tril = (lax.broadcasted_iota(jnp.int32, (c, c), 0) >= lax.broadcasted_iota(jnp.int32, (c, c), 1)).astype(BF16)
    for ch in range(ct // c):
        rs = slice(ch * c, (ch + 1) * c)
        gc_s[rs, :] = sum(_bdot(tril, gw[rs]) for gw in g_wide)

    n_quads = DN_HEADS // DN_QUAD
    st = [state[qd] for qd in range(n_quads)]
    for ch in range(ct // c):
        rs = slice(ch * c, (ch + 1) * c)
        for qd in range(n_quads):
            ls = slice(qd * DN_QW, (qd + 1) * DN_QW)
            q4, k4, v4, b4, gc4 = q_s[rs, ls], k_s[rs, ls], v_s[rs, ls], b_s[rs, ls], gc_s[rs, ls]
            g_key = jnp.sum(jnp.where(diag, gc4, 0.0), axis=0, keepdims=True)
            decay = jnp.where(causal, jnp.exp(jnp.where(causal, gc4 - g_key, 0.0)), 0.0)
            eg = jnp.exp(gc4)
            kb = k4 * b4
            k_bd = _block_diag(k4.astype(BF16), masks)
            low = jnp.where(strict, _dot_nt(kb.astype(BF16), k_bd) * decay, 0.0)
            t_inv = _quad_tri_inverse(low, eye, masks).astype(BF16)
            u = _bdot(t_inv, _block_diag((v4 * b4).astype(BF16), masks))
            w = _bdot(t_inv, _block_diag((kb * eg).astype(BF16), masks))
            attn = jnp.where(causal, _dot_nt(q4.astype(BF16), k_bd) * decay, 0.0)
            g_last = gc4[c - 1:c, :]
            s_bd = _block_diag(st[qd].astype(BF16), masks)
            v_new = (u - _bdot(w.astype(BF16), s_bd)).astype(BF16)
            out_s[rs, ls] = (_bdot((q4 * eg).astype(BF16), s_bd)
                             + _bdot(attn.astype(BF16), _block_diag(v_new, masks)))
            k_dec = (k4 * jnp.exp(g_last - gc4)).astype(BF16)
            cross = _dot_tn(k_dec, v_new)
            upd = sum(jnp.where(masks[h], cross[h * HEAD_DIM:(h + 1) * HEAD_DIM, :], 0.0) for h in range(DN_QUAD))
            st[qd] = st[qd] * jnp.exp(g_last) + upd
    for qd in range(n_quads):
        state[qd] = st[qd]

    o = out_s[...]
    o = o * lax.rsqrt(_seg64_sumsq(o) * (1.0 / HEAD_DIM) + NORM_EPS) * ng_ref[...]
    o_ref[0] = (o * jax.nn.silu(z_ref[0])).astype(BF16)


def _deltanet(dn_qkv, sm, z, conv_w, alog128, dtb128, ng512):
    B, S, _ = dn_qkv.shape
    ct = CT_DN
    head_of_lane = np.arange(DN_W) // HEAD_DIM
    spread = lambda off: jnp.asarray(np.arange(LANES)[:, None] == off + head_of_lane[None, :], BF16)
    return pl.pallas_call(
        _dn_kernel,
        grid=(B, S // ct),
        in_specs=[pl.BlockSpec((1, ct, 3 * DN_W), lambda b, j: (b, j, 0)),
                  pl.BlockSpec((1, ct, LANES), lambda b, j: (b, j, 0)),
                  pl.BlockSpec((1, ct, DN_W), lambda b, j: (b, j, 0)),
                  pl.BlockSpec((DN_CONV, 3 * DN_W), lambda b, j: (0, 0)),
                  pl.BlockSpec((1, LANES), lambda b, j: (0, 0)),
                  pl.BlockSpec((1, LANES), lambda b, j: (0, 0)),
                  pl.BlockSpec((1, DN_W), lambda b, j: (0, 0)),
                  pl.BlockSpec((LANES, DN_W), lambda b, j: (0, 0)),
                  pl.BlockSpec((LANES, DN_W), lambda b, j: (0, 0))],
        out_specs=pl.BlockSpec((1, ct, DN_W), lambda b, j: (b, j, 0)),
        out_shape=jax.ShapeDtypeStruct((B, S, DN_W), BF16),
        scratch_shapes=[pltpu.VMEM((ct + 8, 3 * DN_W), F32),
                        pltpu.VMEM((DN_HEADS // DN_QUAD, HEAD_DIM, DN_QW), F32),
                        pltpu.VMEM((ct, DN_W), F32), pltpu.VMEM((ct, DN_W), F32), pltpu.VMEM((ct, DN_W), F32),
                        pltpu.VMEM((ct, DN_W), F32), pltpu.VMEM((ct, DN_W), F32),
                        pltpu.VMEM((ct, DN_W), F32)],
        compiler_params=_cparams(("arbitrary", "arbitrary")),
        name="gated_deltanet",
    )(dn_qkv, sm, z, conv_w, alog128, dtb128, ng512, spread(SM_BETA), spread(SM_A))


def _merge_kernel(ya_ref, yb_ref, mg_ref, x_ref, mod_ref, wa_ref, wb_ref, wo_ref, ng_ref, rw_ref,
                  xo_ref, hp_ref, lg_ref):
    d = D_MODEL
    m = jax.nn.sigmoid(mg_ref[...])
    y = m[:, :d] * _bdot(ya_ref[...], wa_ref[...]) + m[:, d:] * _bdot(yb_ref[...], wb_ref[...])
    xn = x_ref[...] + mod_ref[0, 2:3, :] * _bdot(y.astype(BF16), wo_ref[...])
    xo_ref[...] = xn
    ms = jnp.mean(xn * xn, axis=-1, keepdims=True)
    h = xn * lax.rsqrt(ms + NORM_EPS) * ng_ref[...]
    h = h * (1.0 + mod_ref[0, 4:5, :]) + mod_ref[0, 3:4, :]
    lg_ref[...] = lax.dot_general(rw_ref[...], h, (((1,), (1,)), ((), ())), preferred_element_type=F32,
                                  precision=lax.Precision.HIGHEST)
    bits = pltpu.bitcast(h.astype(BF16).astype(F32), jnp.uint32)
    hp_ref[...] = (bits[:, :d // 2] & jnp.uint32(0xFFFF0000)) | (bits[:, d // 2:] >> 16)


def _merge(ya, yb, mg, x2, mod_l, wa, wb, wo, norm_g, router_wt, seq):
    T, D = x2.shape
    tm = TM_MERGE
    tpb = seq // tm
    row = lambda w: pl.BlockSpec((tm, w), lambda i: (i, 0))
    full = lambda a: pl.BlockSpec(a.shape, lambda i: (0,) * a.ndim)
    return pl.pallas_call(
        _merge_kernel,
        grid=(T // tm,),
        in_specs=[row(NSA_Q), row(DN_W), row(2 * D), row(D),
                  pl.BlockSpec((1, 6, D), lambda i: (i // tpb, 0, 0)),
                  full(wa), full(wb), full(wo), full(norm_g), full(router_wt)],
        out_specs=[row(D), row(D // 2), pl.BlockSpec((N_EXPERTS, tm), lambda i: (0, i))],
        out_shape=[jax.ShapeDtypeStruct((T, D), F32), jax.ShapeDtypeStruct((T, D // 2), jnp.uint32),
                   jax.ShapeDtypeStruct((N_EXPERTS, T), F32)],
        compiler_params=_cparams(("arbitrary",)),
        name="merge_out",
    )(ya, yb, mg, x2, mod_l, wa, wb, wo, norm_g, router_wt)


def _moe_rows(tm):
    return -(-(TOPK * tm + N_EXPERTS * (ROW_ALIGN - 1) + BLK_MOE) // ROW_ALIGN) * ROW_ALIGN


def _first_max(vals):
    best = vals[0]
    for v in vals[1:]:
        best = jnp.maximum(best, v)
    idx = jnp.full(best.shape, len(vals) - 1, jnp.int32)
    for j in range(len(vals) - 2, -1, -1):
        idx = jnp.where(vals[j] == best, j, idx)
    return best, idx


def _pick(vals, idx):
    out = vals[-1]
    for j in range(len(vals) - 2, -1, -1):
        out = jnp.where(idx == j, vals[j], out)
    return out


def _route_kernel(lg_ref, rb_ref, tri_ref, low_ref, slot_ref, wt_ref, seg_ref, *, tm):
    epg = EXPERTS_PER_GROUP
    scores = jax.nn.sigmoid(lg_ref[...])
    sel = scores + rb_ref[...]
    s_rows = [sel[e:e + 1, :] for e in range(N_EXPERTS)]
    p_rows = [scores[e:e + 1, :] for e in range(N_EXPERTS)]
    grp = []
    for g in range(N_GROUPS):
        a, b, c, d = s_rows[epg * g:epg * g + epg]
        hi1, lo1, hi2, lo2 = jnp.maximum(a, b), jnp.minimum(a, b), jnp.maximum(c, d), jnp.minimum(c, d)
        second = jnp.maximum(jnp.minimum(hi1, hi2), jnp.where(hi1 >= hi2, lo1, lo2))
        grp.append(jnp.maximum(hi1, hi2) + second)
    _, gidx = _first_max(grp)
    cs = [_pick([s_rows[epg * g + j] for g in range(N_GROUPS)], gidx) for j in range(epg)]
    cp = [_pick([p_rows[epg * g + j] for g in range(N_GROUPS)], gidx) for j in range(epg)]
    _, j1 = _first_max(cs)
    _, j2 = _first_max([jnp.where(j1 == j, -jnp.inf, cs[j]) for j in range(epg)])
    w1 = _pick(cp, j1)
    w2 = _pick(cp, j2)
    den = w1 + w2
    wt_ref[0] = jnp.concatenate([w1 / den, w2 / den], axis=0)
    e1 = epg * gidx + j1
    e2 = epg * gidx + j2

    erow = lax.broadcasted_iota(jnp.int32, (N_EXPERTS, tm), 0)
    oh0 = (erow == e1).astype(F32)
    oh1 = (erow == e2).astype(F32)
    cum = _bdot(jnp.concatenate([oh0, oh1], axis=0).astype(BF16), tri_ref[...])
    cum0, cum1 = cum[:N_EXPERTS], cum[N_EXPERTS:]
    cnt0 = cum0[:, tm - 1:tm]
    counts = cnt0 + cum1[:, tm - 1:tm]
    padded = jnp.floor((counts + (ROW_ALIGN - 1)) * (1.0 / ROW_ALIGN)) * ROW_ALIGN
    start = jnp.dot(low_ref[...], jnp.broadcast_to(padded, (N_EXPERTS, LANES)),
                    preferred_element_type=F32, precision=lax.Precision.HIGHEST)
    st = start[:, 0:1]
    slot0 = jnp.sum(oh0 * (st + cum0 - 1.0), axis=0, keepdims=True)
    slot1 = jnp.sum(oh1 * (st + cnt0 + cum1 - 1.0), axis=0, keepdims=True)
    slot_ref[0] = jnp.concatenate([slot0, slot1], axis=0).astype(jnp.int32)
    nch = jnp.floor((counts + (BLK_MOE - 1)) * (1.0 / BLK_MOE))
    seg_ref[0] = jnp.concatenate([start, jnp.broadcast_to(nch, (N_EXPERTS, LANES))], axis=0).astype(jnp.int32)


def _moe_route(logits_t, router_b, tm):
    E, T = logits_t.shape
    nt = T // tm
    tri = (jnp.arange(tm)[:, None] <= jnp.arange(tm)[None, :]).astype(BF16)
    low = (jnp.arange(E)[None, :] < jnp.arange(E)[:, None]).astype(F32)
    slot, wt, seg = pl.pallas_call(
        functools.partial(_route_kernel, tm=tm),
        grid=(nt,),
        in_specs=[pl.BlockSpec((E, tm), lambda i: (0, i)),
                  pl.BlockSpec((E, 1), lambda i: (0, 0)),
                  pl.BlockSpec((tm, tm), lambda i: (0, 0)),
                  pl.BlockSpec((E, E), lambda i: (0, 0))],
        out_specs=[pl.BlockSpec((1, TOPK, tm), lambda i: (i, 0, 0)),
                   pl.BlockSpec((1, TOPK, tm), lambda i: (i, 0, 0)),
                   pl.BlockSpec((1, 2 * E, LANES), lambda i: (i, 0, 0))],
        out_shape=[jax.ShapeDtypeStruct((nt, TOPK, tm), jnp.int32),
                   jax.ShapeDtypeStruct((nt, TOPK, tm), F32),
                   jax.ShapeDtypeStruct((nt, 2 * E, LANES), jnp.int32)],
        compiler_params=_cparams(("arbitrary",)),
        name="moe_route",
    )(logits_t, router_b.astype(F32)[:, None], tri, low)
    return slot, wt, seg[:, :, 0][:, None, :]


def _moe_kernel(slot_ref, wt_ref, seg_ref, hp_ref, x_ref, mod_ref, wg_ref, wu_ref, wd_ref,
                o_ref, gbuf, ybuf, *, tm):
    e = pl.program_id(1)
    half = D_MODEL // 2

    @pl.when(e == 0)
    def _():
        gbuf[...] = jnp.zeros(gbuf.shape, gbuf.dtype)

        def scatter(t, carry):
            row = hp_ref[pl.ds(t, 1), :]
            gbuf[pl.ds(slot_ref[0, 0, t], 1), :] = row
            gbuf[pl.ds(slot_ref[0, 1, t], 1), :] = row
            return carry
        lax.fori_loop(0, tm, scatter, 0, unroll=8)

    start = seg_ref[0, 0, e]

    def chunk(ci, carry):
        r0 = pl.multiple_of(start + ci * BLK_MOE, ROW_ALIGN)
        w = gbuf[pl.ds(r0, BLK_MOE), :]
        x_hi = pltpu.bitcast(w & jnp.uint32(0xFFFF0000), F32).astype(BF16)
        x_lo = pltpu.bitcast(w << 16, F32).astype(BF16)
        gt = _bdot(x_hi, wg_ref[0, 0, 0:half, :]) + _bdot(x_lo, wg_ref[0, 0, half:, :])
        up = _bdot(x_hi, wu_ref[0, 0, 0:half, :]) + _bdot(x_lo, wu_ref[0, 0, half:, :])
        act = (jax.nn.silu(gt) * up).astype(BF16)
        ybuf[pl.ds(r0, BLK_MOE), :] = _bdot(act, wd_ref[0, 0])
        return carry

    lax.fori_loop(0, seg_ref[0, 0, N_EXPERTS + e], chunk, 0)

    @pl.when(e == N_EXPERTS - 1)
    def _():
        def combine(t, carry):
            y0 = ybuf[pl.ds(slot_ref[0, 0, t], 1), :] * wt_ref[0, 0, t]
            y1 = ybuf[pl.ds(slot_ref[0, 1, t], 1), :] * wt_ref[0, 1, t]
            o_ref[pl.ds(t, 1), :] = y0 + y1
            return carry
        lax.fori_loop(0, tm, combine, 0, unroll=8)
        o_ref[...] = x_ref[...] + mod_ref[0, 5:6, :] * o_ref[...]


def _moe(hp, x2, mod_l, route, wg, wu, wd, layer, seq):
    T, D = x2.shape
    tm = TM_MOE
    nt = T // tm
    tpb = seq // tm
    n_rows = _moe_rows(tm)
    slot, wt, seg = route
    smem = lambda r, w: pl.BlockSpec((1, r, w), lambda i, e: (i, 0, 0), memory_space=pltpu.SMEM)
    return pl.pallas_call(
        functools.partial(_moe_kernel, tm=tm),
        grid=(nt, N_EXPERTS),
        in_specs=[smem(TOPK, tm), smem(TOPK, tm), smem(1, 2 * N_EXPERTS),
                  pl.BlockSpec((tm, D // 2), lambda i, e: (i, 0)),
                  pl.BlockSpec((tm, D), lambda i, e: (i, 0)),
                  pl.BlockSpec((1, 6, D), lambda i, e: (i // tpb, 0, 0)),
                  pl.BlockSpec((1, 1, D, D_EXPERT), lambda i, e: (layer, e, 0, 0)),
                  pl.BlockSpec((1, 1, D, D_EXPERT), lambda i, e: (layer, e, 0, 0)),
                  pl.BlockSpec((1, 1, D_EXPERT, D), lambda i, e: (layer, e, 0, 0))],
        out_specs=pl.BlockSpec((tm, D), lambda i, e: (i, 0)),
        out_shape=jax.ShapeDtypeStruct((T, D), F32),
        scratch_shapes=[pltpu.VMEM((n_rows, D // 2), jnp.uint32), pltpu.VMEM((n_rows, D), F32)],
        compiler_params=_cparams(("arbitrary", "arbitrary")),
        name="moe_ffn",
    )(slot, wt, seg, hp, x2, mod_l, wg, wu, wd)


def _pad_lanes(v, offset):
    return jnp.zeros((1, LANES), F32).at[0, offset:offset + v.shape[0]].set(v.astype(F32))


def kernel(x, c, rel_bias, router_w, router_b, ada_w, ada_b, norm1_g, norm2_g, w_in, qk_norm_g,
           cmp_pos, cmp_w1, cmp_w2, dn_conv_w, dn_a_log, dn_dt_bias, dn_norm_g, w_branch_a,
           w_branch_b, w_out, moe_w_gate, moe_w_up, moe_w_down):
    B, S, D = x.shape
    T = B * S
    L = ada_w.shape[0]

    mod = _ada_mod(c, ada_w, ada_b)
    w_proj = _proj_weight(w_in)
    tabs = _nsa_tables(rel_bias, S)
    statics = _nsa_static(S)
    router_wt = router_w.astype(F32).T
    wg, wu, wd = moe_w_gate.astype(BF16), moe_w_up.astype(BF16), moe_w_down.astype(BF16)

    x2 = x.reshape(T, D)
    for l in range(L):
        qkg = qk_norm_g[l].astype(F32)
        q_gain = jnp.tile(qkg[0], NSA_HEADS)[None, :]
        k_gain = jnp.concatenate([jnp.tile(qkg[2], NSA_GROUPS), jnp.tile(qkg[3], NSA_GROUPS)])[None, :]
        q, cmp_raw, k_sw, v_sw, dn_qkv, z, mg, sm = _in_proj(
            x2, mod[l], norm1_g[l][None, :], w_proj[l], q_gain, k_gain, S)

        pos128 = jnp.tile(cmp_pos[l].astype(F32), (1, 1, NSA_GROUPS))
        kc, vc = _compress(cmp_raw.reshape(B, S, 256), pos128, cmp_w1[l].astype(BF16),
                           cmp_w2[l].astype(BF16), qkg[1][None, :])
        y_a = _nsa_attention(q.reshape(B, S, NSA_Q), sm.reshape(B, S, LANES), kc, vc,
                             k_sw.reshape(B, S, 256), v_sw.reshape(B, S, 512), tabs, statics)

        y_b = _deltanet(dn_qkv.reshape(B, S, 3 * DN_W), sm.reshape(B, S, LANES), z.reshape(B, S, DN_W),
                        dn_conv_w[l].astype(F32), _pad_lanes(dn_a_log[l], SM_A),
                        _pad_lanes(dn_dt_bias[l], SM_A), jnp.tile(dn_norm_g[l].astype(F32), DN_HEADS)[None, :])

        x_mid, hp, logits = _merge(y_a.reshape(T, NSA_Q), y_b.reshape(T, DN_W), mg, x2, mod[l],
                                   w_branch_a[l].astype(BF16), w_branch_b[l].astype(BF16),
                                   w_out[l].astype(BF16), norm2_g[l][None, :], router_wt, S)
        route = _moe_route(logits, router_b, TM_MOE)
        x2 = _moe(hp, x_mid, mod[l], route, wg, wu, wd, l, S)
    return x2.reshape(B, S, D)
```

```python
import functools
import math

import numpy as np
import jax
import jax.numpy as jnp
from jax import lax
from jax.experimental import pallas as pl
from jax.experimental.pallas import tpu as pltpu

F32 = jnp.float32
BF16 = jnp.bfloat16

D_MODEL = 1024
DEPTH = 4
HEAD_DIM = 64
NSA_HEADS = 8
NSA_GROUPS = 2
NSA_HPG = NSA_HEADS // NSA_GROUPS
CMP_STRIDE = 16
CMP_BLOCK = 32
CMP_HIDDEN = 256
SEL_BLOCK = 64
SEL_TOPK = 16
SEL_LOCAL = 2
WINDOW = 512
DN_HEADS = 8
DN_CONV = 4
DN_CHUNK = 64
REL_BUCKETS = 32
REL_MAX_DIST = 1024
N_EXPERTS = 16
N_GROUPS = 4
EXPERTS_PER_GROUP = N_EXPERTS // N_GROUPS
TOPK = 2
D_EXPERT = 512
NORM_EPS = 1e-6
FORCE_SCORE = 1e9
NEG = -1e30

NSA_Q = NSA_HEADS * HEAD_DIM
NSA_KV = NSA_GROUPS * HEAD_DIM
DN_W = DN_HEADS * HEAD_DIM
IN_SIZES = (NSA_Q, 6 * NSA_KV, 3 * NSA_HEADS, 3 * DN_W, DN_HEADS, DN_HEADS, DN_W, 2 * D_MODEL)
IN_OFFS = tuple(int(v) for v in np.cumsum((0,) + IN_SIZES))

LANES = 128
VMEM_LIMIT = 56 * 1024 * 1024

TM_PROJ = 512
TM_MERGE = 512
TQ = 128
TK_SEL = 512
CT_DN = 256
TM_MOE = 1024
BLK_MOE = 128
ROW_ALIGN = 8

_SEG_Q = (0, 512)
_SEG_CMP = (512, 768)
_SEG_K = (768, 1024)
_SEG_V = (1024, 1536)
_SEG_DN = (1536, 3072)
_SEG_Z = (3072, 3584)
_SEG_MG = (3584, 5632)
_SEG_SM = (5632, 5760)
N_PROJ = 5760
SM_GATE, SM_BETA, SM_A = 0, 24, 32


def _proj_weight(w_in):
    o = IN_OFFS
    kv = o[1]
    zeros = lambda n: jnp.zeros(w_in.shape[:2] + (n,), w_in.dtype)
    cut = lambda a, b: w_in[:, :, a:b]
    parts = [cut(o[0], o[1]),
             cut(kv, kv + 256),
             cut(kv + 256, kv + 384), cut(kv + 512, kv + 640)]
    for base in (kv + 384, kv + 640):
        parts += [cut(base, base + 64), zeros(128), cut(base + 64, base + 128)]
    parts += [cut(o[3], o[4]), cut(o[6], o[7]), cut(o[7], o[8]),
              cut(o[2], o[3]), cut(o[4], o[5]), cut(o[5], o[6])]
    width = sum(p.shape[2] for p in parts)
    parts.append(zeros(N_PROJ - width))
    return jnp.concatenate(parts, axis=2).astype(BF16)


def _rel_bucket_table(n):
    exact = REL_BUCKETS // 2
    d = np.arange(n, dtype=np.int64)
    far = np.maximum(d, exact).astype(np.float64)
    large = exact + (np.log(far / exact) / math.log(REL_MAX_DIST / exact) * (REL_BUCKETS - exact)).astype(np.int64)
    return np.where(d < exact, d, np.minimum(large, REL_BUCKETS - 1)).astype(np.int32)


def _cparams(sem, vmem=VMEM_LIMIT):
    return pltpu.CompilerParams(dimension_semantics=sem, vmem_limit_bytes=vmem)


def _bdot(a, b):
    return jnp.dot(a, b, preferred_element_type=F32)


def _dot_nt(a, b):
    return lax.dot_general(a, b, (((1,), (1,)), ((), ())), preferred_element_type=F32)


def _dot_tn(a, b):
    return lax.dot_general(a, b, (((0,), (0,)), ((), ())), preferred_element_type=F32)


def _split3(x):
    h = x.astype(BF16)
    r = x - h.astype(F32)
    m = r.astype(BF16)
    l = (r - m.astype(F32)).astype(BF16)
    return h, m, l


def _seg64_sumsq(x):
    rows, width = x.shape
    low = lax.broadcasted_iota(jnp.int32, (rows, LANES), 1) < 64
    outs = []
    for c in range(width // LANES):
        sq = x[:, c * LANES:(c + 1) * LANES]
        sq = sq * sq
        s_lo = jnp.sum(jnp.where(low, sq, 0.0), axis=-1, keepdims=True)
        s_hi = jnp.sum(jnp.where(low, 0.0, sq), axis=-1, keepdims=True)
        outs.append(jnp.where(low, s_lo, s_hi))
    return outs[0] if len(outs) == 1 else jnp.concatenate(outs, axis=1)


def _ada_kernel(c_ref, w_ref, b_ref, o_ref):
    ca = jax.nn.silu(c_ref[...]).astype(BF16)
    o_ref[0] = _bdot(ca, w_ref[0].astype(BF16)) + b_ref[0]


def _ada_mod(c, ada_w, ada_b):
    L, D, N = ada_w.shape
    B = c.shape[0]
    tn = 1536
    out = pl.pallas_call(
        _ada_kernel,
        grid=(L, N // tn),
        in_specs=[pl.BlockSpec((B, D), lambda l, j: (0, 0)),
                  pl.BlockSpec((1, D, tn), lambda l, j: (l, 0, j)),
                  pl.BlockSpec((1, 1, tn), lambda l, j: (l, 0, j))],
        out_specs=pl.BlockSpec((1, B, tn), lambda l, j: (l, 0, j)),
        out_shape=jax.ShapeDtypeStruct((L, B, N), F32),
        compiler_params=_cparams(("arbitrary", "arbitrary")),
        name="ada_mod",
    )(c, ada_w, ada_b.reshape(L, 1, N))
    return out.reshape(L, B, 6, D)


def _inproj_kernel(x_ref, mod_ref, ng_ref, w_ref, qg_ref, kg_ref,
                   q_out, cmp_out, k_out, v_out, dn_out, z_out, mg_out, sm_out):
    x = x_ref[...]
    ms = jnp.mean(x * x, axis=-1, keepdims=True)
    h = x * lax.rsqrt(ms + NORM_EPS) * ng_ref[...]
    h = h * (1.0 + mod_ref[0, 1:2, :]) + mod_ref[0, 0:1, :]
    hb = h.astype(BF16)

    def seg(ab):
        return _bdot(hb, w_ref[:, ab[0]:ab[1]])

    q = seg(_SEG_Q)
    q = q * lax.rsqrt(_seg64_sumsq(q) * (1.0 / HEAD_DIM) + NORM_EPS) * qg_ref[...]
    q_out[...] = (q * (HEAD_DIM ** -0.5)).astype(BF16)
    cmp_out[...] = seg(_SEG_CMP)
    k = seg(_SEG_K)
    k = k * lax.rsqrt(_seg64_sumsq(k) * (1.0 / HEAD_DIM) + NORM_EPS) * kg_ref[...]
    k_out[...] = k.astype(BF16)
    v = seg(_SEG_V)
    lane = lax.broadcasted_iota(jnp.int32, v.shape, 1) % 256
    ones = jnp.where((lane >= 64) & (lane < 192), 1.0, 0.0)
    v_out[...] = (v + ones).astype(BF16)
    dn_out[...] = seg(_SEG_DN)
    z_out[...] = seg(_SEG_Z)
    mg_out[...] = seg(_SEG_MG)
    sm_out[...] = seg(_SEG_SM)


def _in_proj(x2, mod_l, norm_g, w_proj, q_gain, k_gain, seq):
    T, D = x2.shape
    tm = TM_PROJ
    tpb = seq // tm
    widths = [(512, BF16), (256, F32), (256, BF16), (512, BF16), (1536, F32), (512, F32), (2048, F32), (128, F32)]
    return pl.pallas_call(
        _inproj_kernel,
        grid=(T // tm,),
        in_specs=[pl.BlockSpec((tm, D), lambda i: (i, 0)),
                  pl.BlockSpec((1, 6, D), lambda i: (i // tpb, 0, 0)),
                  pl.BlockSpec((1, D), lambda i: (0, 0)),
                  pl.BlockSpec((D, N_PROJ), lambda i: (0, 0)),
                  pl.BlockSpec((1, 512), lambda i: (0, 0)),
                  pl.BlockSpec((1, 256), lambda i: (0, 0))],
        out_specs=[pl.BlockSpec((tm, w), lambda i: (i, 0)) for w, _ in widths],
        out_shape=[jax.ShapeDtypeStruct((T, w), dt) for w, dt in widths],
        compiler_params=_cparams(("arbitrary",)),
        name="in_proj",
    )(x2, mod_l, norm_g, w_proj, q_gain, k_gain)


def _compress_kernel(kraw_ref, vraw_ref, pos_ref, w1_ref, w2_ref, kg_ref, kc_out, vc_out, *, n_chunks):
    for j, raw_ref, out_ref in ((0, kraw_ref, kc_out), (1, vraw_ref, vc_out)):
        top = [jnp.zeros((n_chunks, CMP_HIDDEN), F32) for _ in range(NSA_GROUPS)]
        bot = [jnp.zeros((n_chunks, CMP_HIDDEN), F32) for _ in range(NSA_GROUPS)]
        for r in range(CMP_STRIDE):
            xr = raw_ref[0, pl.ds(r, n_chunks, stride=CMP_STRIDE), :]
            x_top = (xr + pos_ref[j, r:r + 1, :]).astype(BF16)
            x_bot = (xr + pos_ref[j, CMP_STRIDE + r:CMP_STRIDE + r + 1, :]).astype(BF16)
            for g in range(NSA_GROUPS):
                ls = slice(g * HEAD_DIM, (g + 1) * HEAD_DIM)
                top[g] = top[g] + _bdot(x_top[:, ls], w1_ref[j, r * HEAD_DIM:(r + 1) * HEAD_DIM, :])
                bot[g] = bot[g] + _bdot(x_bot[:, ls], w1_ref[j, (CMP_STRIDE + r) * HEAD_DIM:(CMP_STRIDE + r + 1) * HEAD_DIM, :])
        outs = []
        for g in range(NSA_GROUPS):
            hid = top[g] + pltpu.roll(bot[g], n_chunks - 1, 0)
            o = _bdot(jax.nn.gelu(hid).astype(BF16), w2_ref[j])
            if j == 0:
                ms = jnp.mean(o * o, axis=-1, keepdims=True)
                o = o * lax.rsqrt(ms + NORM_EPS) * kg_ref[...]
            outs.append(o)
        out_ref[0] = jnp.concatenate(outs, axis=1).astype(BF16)


def _compress(cmp_raw, pos128, w1, w2, k_gain):
    B, S, _ = cmp_raw.shape
    nc = S // CMP_STRIDE
    return pl.pallas_call(
        functools.partial(_compress_kernel, n_chunks=nc),
        grid=(B,),
        in_specs=[pl.BlockSpec((1, S, LANES), lambda b: (b, 0, 0)),
                  pl.BlockSpec((1, S, LANES), lambda b: (b, 0, 1)),
                  pl.BlockSpec((2, CMP_BLOCK, LANES), lambda b: (0, 0, 0)),
                  pl.BlockSpec((2, CMP_BLOCK * HEAD_DIM, CMP_HIDDEN), lambda b: (0, 0, 0)),
                  pl.BlockSpec((2, CMP_HIDDEN, HEAD_DIM), lambda b: (0, 0, 0)),
                  pl.BlockSpec((1, HEAD_DIM), lambda b: (0, 0))],
        out_specs=[pl.BlockSpec((1, nc, LANES), lambda b: (b, 0, 0))] * 2,
        out_shape=[jax.ShapeDtypeStruct((B, nc, LANES), BF16)] * 2,
        compiler_params=_cparams(("arbitrary",)),
        name="nsa_compress",
    )(cmp_raw, cmp_raw, pos128, w1, w2, k_gain)


def _nsa_tables(rel_bias, seq):
    nq = seq // TQ
    ncp = seq // CMP_STRIDE
    bucket = jnp.asarray(_rel_bucket_table(seq + WINDOW + TQ))
    rb = rel_bias.astype(F32)
    q = np.arange(TQ)[:, None]
    k = np.arange(LANES)[None, :]

    def lookup(dist, valid):
        b = bucket[np.clip(dist, 0, None)]
        vals = rb[b]
        vals = jnp.where(jnp.asarray(valid)[..., None], vals, NEG)
        vals = jnp.moveaxis(vals, -1, 0)
        return vals.reshape(NSA_GROUPS, NSA_HPG * dist.shape[0], dist.shape[1])

    dlim = int(np.argmax(_rel_bucket_table(seq + WINDOW + TQ) == REL_BUCKETS - 1))
    dt = min(nq - 1, -(-(dlim + TQ - 1) // TQ))
    sel = [jnp.full((NSA_GROUPS, NSA_HPG * TQ, LANES), NEG, F32)]
    for d in range(dt + 1):
        dist = TQ * d + q - k
        sel.append(lookup(dist, dist >= 0))
    sel_tab = jnp.stack(sel)
    win = []
    for c in range(WINDOW // LANES + 1):
        dist = q + WINDOW - LANES * c - k
        win.append(lookup(dist, (dist >= 0) & (dist < WINDOW)))
    win_tab = jnp.stack(win)
    off = CMP_STRIDE * (ncp - 1) + CMP_BLOCK - 1
    by_dist = jnp.concatenate([jnp.full((NSA_HEADS, off), NEG, F32), rb[bucket[:seq]].T], axis=1)
    cols = [by_dist[:, off - (CMP_STRIDE * n + CMP_BLOCK - 1):][:, :seq] for n in range(ncp - 1)]
    cols.append(jnp.full((NSA_HEADS, seq), NEG, F32))
    cmp_tab = jnp.stack(cols, axis=-1)
    cmp_tab = cmp_tab.reshape(NSA_GROUPS, NSA_HPG, nq, TQ, ncp).transpose(2, 0, 1, 3, 4)
    cmp_tab = cmp_tab.reshape(nq, NSA_GROUPS, NSA_HPG * TQ, ncp)
    return sel_tab, win_tab, cmp_tab, dt


def _nsa_static(seq):
    nb = seq // SEL_BLOCK
    ncp = seq // CMP_STRIDE
    ratio, nsub = SEL_BLOCK // CMP_STRIDE, CMP_BLOCK // CMP_STRIDE
    delta = np.arange(ncp)[:, None] - ratio * np.arange(nb)[None, :]
    m_idx = delta[..., None] + np.arange(nsub)
    overlap = np.sum((m_idx >= 0) & (m_idx < ratio), axis=-1).astype(np.float32)
    overlap[ncp - 1, :] = 0.0
    expand = (np.arange(nb)[:, None] == (np.arange(seq)[None, :] // SEL_BLOCK)).astype(np.float32)
    return jnp.asarray(overlap.T, BF16), jnp.asarray(expand, BF16)


def _nsa_kernel(q_ref, sm_ref, kc_ref, vc_ref, k_ref, v_ref, bc_ref, ws_ref, ww_ref, ovt_ref, ex_ref,
                o_ref, m_s, acc_s, *, n_blocks, n_sel, dt):
    i = pl.program_id(1)
    t0 = i * TQ
    rows = NSA_HPG * TQ
    half = lax.broadcasted_iota(jnp.int32, (TQ, LANES), 1) // HEAD_DIM
    q = q_ref[0].astype(F32)
    gates = jax.nn.sigmoid(sm_ref[0])
    ng = NSA_GROUPS
    arows = ng * rows
    lane2 = lax.broadcasted_iota(jnp.int32, (n_blocks, ng * TQ), 1)
    jb = lax.broadcasted_iota(jnp.int32, (n_blocks, ng * TQ), 0)
    cur = (t0 + lane2 % TQ) // SEL_BLOCK
    valid = jb <= cur
    forced = valid & ((jb == 0) | (jb > cur - SEL_LOCAL))
    sub = TK_SEL // LANES
    y_heads = []

    def block_max(blocks):
        bm = blocks[0]
        for b in blocks[1:]:
            bm = jnp.maximum(bm, b)
        return jnp.max(bm, axis=-1, keepdims=True)

    def softmax_pv(blocks, v_tiles, m):
        p = jnp.concatenate([jnp.exp(b - m) for b in blocks], axis=1).astype(BF16)
        return jnp.concatenate([_bdot(p[g * rows:(g + 1) * rows], v_tiles[g]) for g in range(ng)], axis=0)

    def online_update(blocks, v_tiles, m_ref, acc_ref):
        m_old = m_ref[...]
        m_new = jnp.maximum(m_old, block_max(blocks))
        acc_ref[...] = jnp.exp(m_old - m_new) * acc_ref[...] + softmax_pv(blocks, v_tiles, m_new)
        m_ref[...] = m_new

    stack = []
    for g in range(ng):
        for h in range(NSA_HPG):
            hd = NSA_HPG * g + h
            blk = q[:, (hd // 2) * LANES:(hd // 2 + 1) * LANES]
            if hd % 2 != g:
                blk = pltpu.roll(blk, HEAD_DIM, 1)
            stack.append(jnp.where(half == g, blk, 0.0))
    qa = jnp.concatenate(stack, axis=0).astype(BF16)
    group_half = jnp.concatenate([half == g for g in range(ng) for _ in range(NSA_HPG)], axis=0)

    bc = bc_ref[0].reshape(arows, bc_ref.shape[-1])
    sc = _dot_nt(qa, kc_ref[0]) + bc
    ec = jnp.exp(sc - jnp.max(sc, axis=-1, keepdims=True))
    ec = jnp.where(bc > 0.5 * NEG, ec, 0.0)
    lc = jnp.sum(ec, axis=-1, keepdims=True)
    pc = ec / jnp.where(lc > 0.0, lc, 1.0)
    o_c = _bdot(pc.astype(BF16), vc_ref[0])

    psum = jnp.concatenate(
        [sum(pc[g * rows + h * TQ:g * rows + (h + 1) * TQ] for h in range(NSA_HPG)) for g in range(ng)], axis=0)
    p_hi = psum.astype(BF16)
    p_lo = (psum - p_hi.astype(F32)).astype(BF16)
    imp = _dot_nt(ovt_ref[...], p_hi) + _dot_nt(ovt_ref[...], p_lo)
    score = jnp.where(forced, FORCE_SCORE, jnp.where(valid, imp, -1.0))

    def pick(_, carry):
        sc_, sel_ = carry
        top = jnp.max(sc_, axis=0, keepdims=True)
        first = jnp.min(jnp.where(sc_ == top, jb, n_blocks), axis=0, keepdims=True)
        hit = jb == first
        return jnp.where(hit, -3e38, sc_), jnp.where(hit, 1.0, sel_)

    _, sel = lax.fori_loop(0, n_sel, pick, (score, jnp.zeros((n_blocks, ng * TQ), F32)))
    sel_q = jnp.transpose(sel).astype(BF16)

    m_s[...] = jnp.full((arows, LANES), -3e38, F32)
    acc_s[...] = jnp.zeros((arows, LANES), F32)

    def sel_body(jt, carry):
        ks = pl.multiple_of(jt * TK_SEL, TK_SEL)
        s = _dot_nt(qa, k_ref[0, pl.ds(ks, TK_SEL), 0:LANES])
        selx = _bdot(sel_q, ex_ref[:, pl.ds(ks, TK_SEL)])
        madd = (selx - 1.0) * (-NEG)
        parts = []
        for c in range(sub):
            d = i - (jt * sub + c)
            b = ws_ref[jnp.clip(d, -1, dt) + 1].reshape(arows, LANES)
            sc_ = (s[:, c * LANES:(c + 1) * LANES] + b).reshape(ng, NSA_HPG, TQ, LANES)
            sc_ = sc_ + madd[:, c * LANES:(c + 1) * LANES].reshape(ng, 1, TQ, LANES)
            parts.append(sc_.reshape(arows, LANES))
        online_update(parts, [v_ref[0, pl.ds(ks, TK_SEL), g * LANES:(g + 1) * LANES] for g in range(ng)],
                      m_s, acc_s)
        return carry

    lax.fori_loop(0, (t0 + TQ + TK_SEL - 1) // TK_SEL, sel_body, 0)

    nwt = WINDOW // LANES + 1
    starts = [t0 - WINDOW + c * LANES for c in range(nwt)]
    clamped = [pl.multiple_of(jnp.maximum(ks, 0), LANES) for ks in starts]
    kw = jnp.concatenate([k_ref[0, pl.ds(ks, LANES), LANES:2 * LANES] for ks in clamped], axis=0)
    vws = [jnp.concatenate([v_ref[0, pl.ds(ks, LANES), (2 + g) * LANES:(3 + g) * LANES] for ks in clamped], axis=0)
           for g in range(ng)]
    sw = _dot_nt(qa, kw)
    blocks = [sw[:, c * LANES:(c + 1) * LANES]
              + jnp.where(starts[c] >= 0, ww_ref[c].reshape(arows, LANES), NEG) for c in range(nwt)]

    def finish(acc):
        out = acc / pltpu.roll(acc, HEAD_DIM, 1)
        return jnp.where(group_half, out, 0.0)

    o_s = finish(acc_s[...])
    o_w = finish(softmax_pv(blocks, vws, block_max(blocks)))
    for g in range(ng):
        for h in range(NSA_HPG):
            hd = NSA_HPG * g + h
            rs = slice(g * rows + h * TQ, g * rows + (h + 1) * TQ)
            y = (gates[:, 3 * hd:3 * hd + 1] * o_c[rs] + gates[:, 3 * hd + 1:3 * hd + 2] * o_s[rs]
                 + gates[:, 3 * hd + 2:3 * hd + 3] * o_w[rs])
            y = jnp.where(half == g, y, 0.0)
            if hd % 2 != g:
                y = pltpu.roll(y, HEAD_DIM, 1)
            y_heads.append(y)

    o_ref[0] = jnp.concatenate([y_heads[2 * c] + y_heads[2 * c + 1] for c in range(NSA_HEADS // 2)],
                               axis=1).astype(BF16)


def _nsa_attention(q, sm, kc, vc, k_sw, v_sw, tabs, statics):
    B, S, _ = q.shape
    sel_tab, win_tab, cmp_tab, dt = tabs
    ovt, expand = statics
    nq = S // TQ
    ncp = S // CMP_STRIDE
    nb = S // SEL_BLOCK
    rows = NSA_HPG * TQ
    const = lambda nd: (lambda b, i: (0,) * nd)
    return pl.pallas_call(
        functools.partial(_nsa_kernel, n_blocks=nb, n_sel=min(SEL_TOPK, nb), dt=dt),
        grid=(B, nq),
        in_specs=[pl.BlockSpec((1, TQ, NSA_Q), lambda b, i: (b, i, 0)),
                  pl.BlockSpec((1, TQ, LANES), lambda b, i: (b, i, 0)),
                  pl.BlockSpec((1, ncp, LANES), lambda b, i: (b, 0, 0)),
                  pl.BlockSpec((1, ncp, LANES), lambda b, i: (b, 0, 0)),
                  pl.BlockSpec((1, S, 256), lambda b, i: (b, 0, 0)),
                  pl.BlockSpec((1, S, 512), lambda b, i: (b, 0, 0)),
                  pl.BlockSpec((1, NSA_GROUPS, rows, ncp), lambda b, i: (i, 0, 0, 0)),
                  pl.BlockSpec(sel_tab.shape, const(4)),
                  pl.BlockSpec(win_tab.shape, const(4)),
                  pl.BlockSpec(ovt.shape, const(2)),
                  pl.BlockSpec(expand.shape, const(2))],
        out_specs=pl.BlockSpec((1, TQ, NSA_Q), lambda b, i: (b, i, 0)),
        out_shape=jax.ShapeDtypeStruct((B, S, NSA_Q), BF16),
        scratch_shapes=[pltpu.VMEM((NSA_GROUPS * rows, LANES), F32), pltpu.VMEM((NSA_GROUPS * rows, LANES), F32)],
        compiler_params=_cparams(("arbitrary", "arbitrary")),
        name="nsa_attention",
    )(q, sm, kc, vc, k_sw, v_sw, cmp_tab, sel_tab, win_tab, ovt, expand)


DN_QUAD = 4
DN_QW = DN_QUAD * HEAD_DIM


def _block_diag(a, head_masks):
    zero = jnp.zeros_like(a)
    return jnp.concatenate([jnp.where(m, a, zero) for m in head_masks], axis=0)


def _quad_tri_inverse(low, eye, head_masks):
    x = eye - low
    lb = low.astype(BF16)
    p = _bdot(lb, _block_diag(lb, head_masks))
    steps = int(math.log2(DN_CHUNK)) - 1
    for s in range(steps):
        pb = p.astype(BF16)
        pd = _block_diag(pb, head_masks)
        x = x + _bdot(x.astype(BF16), pd)
        if s + 1 < steps:
            p = _bdot(pb, pd)
    return x


def _dn_kernel(qkv_ref, sm_ref, z_ref, cw_ref, alog_ref, dtb_ref, ng_ref, eb_ref, ea_ref, o_ref,
               xbuf, state, q_s, k_s, v_s, b_s, gc_s, out_s):
    j = pl.program_id(1)
    ct = CT_DN
    c = DN_CHUNK

    @pl.when(j == 0)
    def _():
        xbuf[0:8, :] = jnp.zeros((8, 3 * DN_W), F32)
        state[...] = jnp.zeros(state.shape, F32)

    xbuf[8:8 + ct, :] = qkv_ref[0]
    acc = cw_ref[0:1, :] * xbuf[5:5 + ct, :]
    for tap in range(1, DN_CONV):
        acc = acc + cw_ref[tap:tap + 1, :] * xbuf[5 + tap:5 + tap + ct, :]
    xbuf[0:8, :] = xbuf[ct:ct + 8, :]
    y = jax.nn.silu(acc)
    qh = y[:, 0:DN_W]
    kh = y[:, DN_W:2 * DN_W]
    q_s[...] = qh * lax.rsqrt(_seg64_sumsq(qh) + NORM_EPS) * (HEAD_DIM ** -0.5)
    k_s[...] = kh * lax.rsqrt(_seg64_sumsq(kh) + NORM_EPS)
    v_s[...] = y[:, 2 * DN_W:3 * DN_W]
    sm = sm_ref[0]
    b_s[...] = sum(_bdot(p, eb_ref[...]) for p in _split3(jax.nn.sigmoid(sm)))
    gdec = -jnp.exp(alog_ref[...]) * jax.nn.softplus(sm + dtb_ref[...])
    g_wide = [_bdot(p, ea_ref[...]).astype(BF16) for p in _split3(gdec)]
    row = lax.broadcasted_iota(jnp.int32, (c, DN_QW), 0)
    col = lax.broadcasted_iota(jnp.int32, (c, DN_QW), 1) % c
    causal = row >= col
    strict = row > col
    diag = row == col
    eye = diag.astype(F32)
    masks = [lax.broadcasted_iota(jnp.int32, (c, DN_QW), 1) // HEAD_DIM == h for h in range(DN_QUAD)]
    tril = (lax.broadcasted_iota(jnp.int32, (c, c), 0) >= lax.broadcasted_iota(jnp.int32, (c, c), 1)).astype(BF16)
    for ch in range(ct // c):
        rs = slice(ch * c, (ch + 1) * c)
        gc_s[rs, :] = sum(_bdot(tril, gw[rs]) for gw in g_wide)

    n_quads = DN_HEADS // DN_QUAD
    st = [state[qd] for qd in range(n_quads)]
    for ch in range(ct // c):
        rs = slice(ch * c, (ch + 1) * c)
        for qd in range(n_quads):
            ls = slice(qd * DN_QW, (qd + 1) * DN_QW)
            q4, k4, v4, b4, gc4 = q_s[rs, ls], k_s[rs, ls], v_s[rs, ls], b_s[rs, ls], gc_s[rs, ls]
            g_key = jnp.sum(jnp.where(diag, gc4, 0.0), axis=0, keepdims=True)
            decay = jnp.where(causal, jnp.exp(jnp.where(causal, gc4 - g_key, 0.0)), 0.0)
            eg = jnp.exp(gc4)
            kb = k4 * b4
            k_bd = _block_diag(k4.astype(BF16), masks)
            low = jnp.where(strict, _dot_nt(kb.astype(BF16), k_bd) * decay, 0.0)
            t_inv = _quad_tri_inverse(low, eye, masks).astype(BF16)
            u = _bdot(t_inv, _block_diag((v4 * b4).astype(BF16), masks))
            w = _bdot(t_inv, _block_diag((kb * eg).astype(BF16), masks))
            attn = jnp.where(causal, _dot_nt(q4.astype(BF16), k_bd) * decay, 0.0)
            g_last = gc4[c - 1:c, :]
            s_bd = _block_diag(st[qd].astype(BF16), masks)
            v_new = (u - _bdot(w.astype(BF16), s_bd)).astype(BF16)
            out_s[rs, ls] = (_bdot((q4 * eg).astype(BF16), s_bd)
                             + _bdot(attn.astype(BF16), _block_diag(v_new, masks)))
            k_dec = (k4 * jnp.exp(g_last - gc4)).astype(BF16)
            cross = _dot_tn(k_dec, v_new)
            upd = sum(jnp.where(masks[h], cross[h * HEAD_DIM:(h + 1) * HEAD_DIM, :], 0.0) for h in range(DN_QUAD))
            st[qd] = st[qd] * jnp.exp(g_last) + upd
    for qd in range(n_quads):
        state[qd] = st[qd]

    o = out_s[...]
    o = o * lax.rsqrt(_seg64_sumsq(o) * (1.0 / HEAD_DIM) + NORM_EPS) * ng_ref[...]
    o_ref[0] = (o * jax.nn.silu(z_ref[0])).astype(BF16)


def _deltanet(dn_qkv, sm, z, conv_w, alog128, dtb128, ng512):
    B, S, _ = dn_qkv.shape
    ct = CT_DN
    head_of_lane = np.arange(DN_W) // HEAD_DIM
    spread = lambda off: jnp.asarray(np.arange(LANES)[:, None] == off + head_of_lane[None, :], BF16)
    return pl.pallas_call(
        _dn_kernel,
        grid=(B, S // ct),
        in_specs=[pl.BlockSpec((1, ct, 3 * DN_W), lambda b, j: (b, j, 0)),
                  pl.BlockSpec((1, ct, LANES), lambda b, j: (b, j, 0)),
                  pl.BlockSpec((1, ct, DN_W), lambda b, j: (b, j, 0)),
                  pl.BlockSpec((DN_CONV, 3 * DN_W), lambda b, j: (0, 0)),
                  pl.BlockSpec((1, LANES), lambda b, j: (0, 0)),
                  pl.BlockSpec((1, LANES), lambda b, j: (0, 0)),
                  pl.BlockSpec((1, DN_W), lambda b, j: (0, 0)),
                  pl.BlockSpec((LANES, DN_W), lambda b, j: (0, 0)),
                  pl.BlockSpec((LANES, DN_W), lambda b, j: (0, 0))],
        out_specs=pl.BlockSpec((1, ct, DN_W), lambda b, j: (b, j, 0)),
        out_shape=jax.ShapeDtypeStruct((B, S, DN_W), BF16),
        scratch_shapes=[pltpu.VMEM((ct + 8, 3 * DN_W), F32),
                        pltpu.VMEM((DN_HEADS // DN_QUAD, HEAD_DIM, DN_QW), F32),
                        pltpu.VMEM((ct, DN_W), F32), pltpu.VMEM((ct, DN_W), F32), pltpu.VMEM((ct, DN_W), F32),
                        pltpu.VMEM((ct, DN_W), F32), pltpu.VMEM((ct, DN_W), F32),
                        pltpu.VMEM((ct, DN_W), F32)],
        compiler_params=_cparams(("arbitrary", "arbitrary")),
        name="gated_deltanet",
    )(dn_qkv, sm, z, conv_w, alog128, dtb128, ng512, spread(SM_BETA), spread(SM_A))


def _merge_kernel(ya_ref, yb_ref, mg_ref, x_ref, mod_ref, wa_ref, wb_ref, wo_ref, ng_ref, rw_ref,
                  xo_ref, hp_ref, lg_ref):
    d = D_MODEL
    m = jax.nn.sigmoid(mg_ref[...])
    y = m[:, :d] * _bdot(ya_ref[...], wa_ref[...]) + m[:, d:] * _bdot(yb_ref[...], wb_ref[...])
    xn = x_ref[...] + mod_ref[0, 2:3, :] * _bdot(y.astype(BF16), wo_ref[...])
    xo_ref[...] = xn
    ms = jnp.mean(xn * xn, axis=-1, keepdims=True)
    h = xn * lax.rsqrt(ms + NORM_EPS) * ng_ref[...]
    h = h * (1.0 + mod_ref[0, 4:5, :]) + mod_ref[0, 3:4, :]
    lg_ref[...] = lax.dot_general(rw_ref[...], h, (((1,), (1,)), ((), ())), preferred_element_type=F32,
                                  precision=lax.Precision.HIGHEST)
    bits = pltpu.bitcast(h.astype(BF16).astype(F32), jnp.uint32)
    hp_ref[...] = (bits[:, :d // 2] & jnp.uint32(0xFFFF0000)) | (bits[:, d // 2:] >> 16)


def _merge(ya, yb, mg, x2, mod_l, wa, wb, wo, norm_g, router_wt, seq):
    T, D = x2.shape
    tm = TM_MERGE
    tpb = seq // tm
    row = lambda w: pl.BlockSpec((tm, w), lambda i: (i, 0))
    full = lambda a: pl.BlockSpec(a.shape, lambda i: (0,) * a.ndim)
    return pl.pallas_call(
        _merge_kernel,
        grid=(T // tm,),
        in_specs=[row(NSA_Q), row(DN_W), row(2 * D), row(D),
                  pl.BlockSpec((1, 6, D), lambda i: (i // tpb, 0, 0)),
                  full(wa), full(wb), full(wo), full(norm_g), full(router_wt)],
        out_specs=[row(D), row(D // 2), pl.BlockSpec((N_EXPERTS, tm), lambda i: (0, i))],
        out_shape=[jax.ShapeDtypeStruct((T, D), F32), jax.ShapeDtypeStruct((T, D // 2), jnp.uint32),
                   jax.ShapeDtypeStruct((N_EXPERTS, T), F32)],
        compiler_params=_cparams(("arbitrary",)),
        name="merge_out",
    )(ya, yb, mg, x2, mod_l, wa, wb, wo, norm_g, router_wt)


def _moe_rows(tm):
    return -(-(TOPK * tm + N_EXPERTS * (ROW_ALIGN - 1) + BLK_MOE) // ROW_ALIGN) * ROW_ALIGN


def _first_max(vals):
    best = vals[0]
    for v in vals[1:]:
        best = jnp.maximum(best, v)
    idx = jnp.full(best.shape, len(vals) - 1, jnp.int32)
    for j in range(len(vals) - 2, -1, -1):
        idx = jnp.where(vals[j] == best, j, idx)
    return best, idx


def _pick(vals, idx):
    out = vals[-1]
    for j in range(len(vals) - 2, -1, -1):
        out = jnp.where(idx == j, vals[j], out)
    return out


def _route_kernel(lg_ref, rb_ref, tri_ref, low_ref, slot_ref, wt_ref, seg_ref, *, tm):
    epg = EXPERTS_PER_GROUP
    scores = jax.nn.sigmoid(lg_ref[...])
    sel = scores + rb_ref[...]
    s_rows = [sel[e:e + 1, :] for e in range(N_EXPERTS)]
    p_rows = [scores[e:e + 1, :] for e in range(N_EXPERTS)]
    grp = []
    for g in range(N_GROUPS):
        a, b, c, d = s_rows[epg * g:epg * g + epg]
        hi1, lo1, hi2, lo2 = jnp.maximum(a, b), jnp.minimum(a, b), jnp.maximum(c, d), jnp.minimum(c, d)
        second = jnp.maximum(jnp.minimum(hi1, hi2), jnp.where(hi1 >= hi2, lo1, lo2))
        grp.append(jnp.maximum(hi1, hi2) + second)
    _, gidx = _first_max(grp)
    cs = [_pick([s_rows[epg * g + j] for g in range(N_GROUPS)], gidx) for j in range(epg)]
    cp = [_pick([p_rows[epg * g + j] for g in range(N_GROUPS)], gidx) for j in range(epg)]
    _, j1 = _first_max(cs)
    _, j2 = _first_max([jnp.where(j1 == j, -jnp.inf, cs[j]) for j in range(epg)])
    w1 = _pick(cp, j1)
    w2 = _pick(cp, j2)
    den = w1 + w2
    wt_ref[0] = jnp.concatenate([w1 / den, w2 / den], axis=0)
    e1 = epg * gidx + j1
    e2 = epg * gidx + j2

    erow = lax.broadcasted_iota(jnp.int32, (N_EXPERTS, tm), 0)
    oh0 = (erow == e1).astype(F32)
    oh1 = (erow == e2).astype(F32)
    cum = _bdot(jnp.concatenate([oh0, oh1], axis=0).astype(BF16), tri_ref[...])
    cum0, cum1 = cum[:N_EXPERTS], cum[N_EXPERTS:]
    cnt0 = cum0[:, tm - 1:tm]
    counts = cnt0 + cum1[:, tm - 1:tm]
    padded = jnp.floor((counts + (ROW_ALIGN - 1)) * (1.0 / ROW_ALIGN)) * ROW_ALIGN
    start = jnp.dot(low_ref[...], jnp.broadcast_to(padded, (N_EXPERTS, LANES)),
                    preferred_element_type=F32, precision=lax.Precision.HIGHEST)
    st = start[:, 0:1]
    slot0 = jnp.sum(oh0 * (st + cum0 - 1.0), axis=0, keepdims=True)
    slot1 = jnp.sum(oh1 * (st + cnt0 + cum1 - 1.0), axis=0, keepdims=True)
    slot_ref[0] = jnp.concatenate([slot0, slot1], axis=0).astype(jnp.int32)
    nch = jnp.floor((counts + (BLK_MOE - 1)) * (1.0 / BLK_MOE))
    seg_ref[0] = jnp.concatenate([start, jnp.broadcast_to(nch, (N_EXPERTS, LANES))], axis=0).astype(jnp.int32)


def _moe_route(logits_t, router_b, tm):
    E, T = logits_t.shape
    nt = T // tm
    tri = (jnp.arange(tm)[:, None] <= jnp.arange(tm)[None, :]).astype(BF16)
    low = (jnp.arange(E)[None, :] < jnp.arange(E)[:, None]).astype(F32)
    slot, wt, seg = pl.pallas_call(
        functools.partial(_route_kernel, tm=tm),
        grid=(nt,),
        in_specs=[pl.BlockSpec((E, tm), lambda i: (0, i)),
                  pl.BlockSpec((E, 1), lambda i: (0, 0)),
                  pl.BlockSpec((tm, tm), lambda i: (0, 0)),
                  pl.BlockSpec((E, E), lambda i: (0, 0))],
        out_specs=[pl.BlockSpec((1, TOPK, tm), lambda i: (i, 0, 0)),
                   pl.BlockSpec((1, TOPK, tm), lambda i: (i, 0, 0)),
                   pl.BlockSpec((1, 2 * E, LANES), lambda i: (i, 0, 0))],
        out_shape=[jax.ShapeDtypeStruct((nt, TOPK, tm), jnp.int32),
                   jax.ShapeDtypeStruct((nt, TOPK, tm), F32),
                   jax.ShapeDtypeStruct((nt, 2 * E, LANES), jnp.int32)],
        compiler_params=_cparams(("arbitrary",)),
        name="moe_route",
    )(logits_t, router_b.astype(F32)[:, None], tri, low)
    return slot, wt, seg[:, :, 0][:, None, :]


def _moe_kernel(slot_ref, wt_ref, seg_ref, hp_ref, x_ref, mod_ref, wg_ref, wu_ref, wd_ref,
                o_ref, gbuf, ybuf, *, tm):
    e = pl.program_id(1)
    half = D_MODEL // 2

    @pl.when(e == 0)
    def _():
        gbuf[...] = jnp.zeros(gbuf.shape, gbuf.dtype)

        def scatter(t, carry):
            row = hp_ref[pl.ds(t, 1), :]
            gbuf[pl.ds(slot_ref[0, 0, t], 1), :] = row
            gbuf[pl.ds(slot_ref[0, 1, t], 1), :] = row
            return carry
        lax.fori_loop(0, tm, scatter, 0, unroll=8)

    start = seg_ref[0, 0, e]

    def chunk(ci, carry):
        r0 = pl.multiple_of(start + ci * BLK_MOE, ROW_ALIGN)
        w = gbuf[pl.ds(r0, BLK_MOE), :]
        x_hi = pltpu.bitcast(w & jnp.uint32(0xFFFF0000), F32).astype(BF16)
        x_lo = pltpu.bitcast(w << 16, F32).astype(BF16)
        gt = _bdot(x_hi, wg_ref[0, 0, 0:half, :]) + _bdot(x_lo, wg_ref[0, 0, half:, :])
        up = _bdot(x_hi, wu_ref[0, 0, 0:half, :]) + _bdot(x_lo, wu_ref[0, 0, half:, :])
        act = (jax.nn.silu(gt) * up).astype(BF16)
        ybuf[pl.ds(r0, BLK_MOE), :] = _bdot(act, wd_ref[0, 0])
        return carry

    lax.fori_loop(0, seg_ref[0, 0, N_EXPERTS + e], chunk, 0)

    @pl.when(e == N_EXPERTS - 1)
    def _():
        def combine(t, carry):
            y0 = ybuf[pl.ds(slot_ref[0, 0, t], 1), :] * wt_ref[0, 0, t]
            y1 = ybuf[pl.ds(slot_ref[0, 1, t], 1), :] * wt_ref[0, 1, t]
            o_ref[pl.ds(t, 1), :] = y0 + y1
            return carry
        lax.fori_loop(0, tm, combine, 0, unroll=8)
        o_ref[...] = x_ref[...] + mod_ref[0, 5:6, :] * o_ref[...]


def _moe(hp, x2, mod_l, route, wg, wu, wd, layer, seq):
    T, D = x2.shape
    tm = TM_MOE
    nt = T // tm
    tpb = seq // tm
    n_rows = _moe_rows(tm)
    slot, wt, seg = route
    smem = lambda r, w: pl.BlockSpec((1, r, w), lambda i, e: (i, 0, 0), memory_space=pltpu.SMEM)
    return pl.pallas_call(
        functools.partial(_moe_kernel, tm=tm),
        grid=(nt, N_EXPERTS),
        in_specs=[smem(TOPK, tm), smem(TOPK, tm), smem(1, 2 * N_EXPERTS),
                  pl.BlockSpec((tm, D // 2), lambda i, e: (i, 0)),
                  pl.BlockSpec((tm, D), lambda i, e: (i, 0)),
                  pl.BlockSpec((1, 6, D), lambda i, e: (i // tpb, 0, 0)),
                  pl.BlockSpec((1, 1, D, D_EXPERT), lambda i, e: (layer, e, 0, 0)),
                  pl.BlockSpec((1, 1, D, D_EXPERT), lambda i, e: (layer, e, 0, 0)),
                  pl.BlockSpec((1, 1, D_EXPERT, D), lambda i, e: (layer, e, 0, 0))],
        out_specs=pl.BlockSpec((tm, D), lambda i, e: (i, 0)),
        out_shape=jax.ShapeDtypeStruct((T, D), F32),
        scratch_shapes=[pltpu.VMEM((n_rows, D // 2), jnp.uint32), pltpu.VMEM((n_rows, D), F32)],
        compiler_params=_cparams(("arbitrary", "arbitrary")),
        name="moe_ffn",
    )(slot, wt, seg, hp, x2, mod_l, wg, wu, wd)


def _pad_lanes(v, offset):
    return jnp.zeros((1, LANES), F32).at[0, offset:offset + v.shape[0]].set(v.astype(F32))


def kernel(x, c, rel_bias, router_w, router_b, ada_w, ada_b, norm1_g, norm2_g, w_in, qk_norm_g,
           cmp_pos, cmp_w1, cmp_w2, dn_conv_w, dn_a_log, dn_dt_bias, dn_norm_g, w_branch_a,
           w_branch_b, w_out, moe_w_gate, moe_w_up, moe_w_down):
    B, S, D = x.shape
    T = B * S
    L = ada_w.shape[0]

    mod = _ada_mod(c, ada_w, ada_b)
    w_proj = _proj_weight(w_in)
    tabs = _nsa_tables(rel_bias, S)
    statics = _nsa_static(S)
    router_wt = router_w.astype(F32).T
    wg, wu, wd = moe_w_gate.astype(BF16), moe_w_up.astype(BF16), moe_w_down.astype(BF16)

    x2 = x.reshape(T, D)
    for l in range(L):
        qkg = qk_norm_g[l].astype(F32)
        q_gain = jnp.tile(qkg[0], NSA_HEADS)[None, :]
        k_gain = jnp.concatenate([jnp.tile(qkg[2], NSA_GROUPS), jnp.tile(qkg[3], NSA_GROUPS)])[None, :]
        q, cmp_raw, k_sw, v_sw, dn_qkv, z, mg, sm = _in_proj(
            x2, mod[l], norm1_g[l][None, :], w_proj[l], q_gain, k_gain, S)

        pos128 = jnp.tile(cmp_pos[l].astype(F32), (1, 1, NSA_GROUPS))
        kc, vc = _compress(cmp_raw.reshape(B, S, 256), pos128, cmp_w1[l].astype(BF16),
                           cmp_w2[l].astype(BF16), qkg[1][None, :])
        y_a = _nsa_attention(q.reshape(B, S, NSA_Q), sm.reshape(B, S, LANES), kc, vc,
                             k_sw.reshape(B, S, 256), v_sw.reshape(B, S, 512), tabs, statics)

        y_b = _deltanet(dn_qkv.reshape(B, S, 3 * DN_W), sm.reshape(B, S, LANES), z.reshape(B, S, DN_W),
                        dn_conv_w[l].astype(F32), _pad_lanes(dn_a_log[l], SM_A),
                        _pad_lanes(dn_dt_bias[l], SM_A), jnp.tile(dn_norm_g[l].astype(F32), DN_HEADS)[None, :])

        x_mid, hp, logits = _merge(y_a.reshape(T, NSA_Q), y_b.reshape(T, DN_W), mg, x2, mod[l],
                                   w_branch_a[l].astype(BF16), w_branch_b[l].astype(BF16),
                                   w_out[l].astype(BF16), norm2_g[l][None, :], router_wt, S)
        route = _moe_route(logits, router_b, TM_MOE)
        x2 = _moe(hp, x_mid, mod[l], route, wg, wu, wd, l, S)
    return x2.reshape(B, S, D)
```

```python
import functools
import math

import numpy as np
import jax
import jax.numpy as jnp
from jax import lax
from jax.experimental import pallas as pl
from jax.experimental.pallas import tpu as pltpu

F32 = jnp.float32
BF16 = jnp.bfloat16

D_MODEL = 1024
DEPTH = 4
HEAD_DIM = 64
NSA_HEADS = 8
NSA_GROUPS = 2
NSA_HPG = NSA_HEADS // NSA_GROUPS
CMP_STRIDE = 16
CMP_BLOCK = 32
CMP_HIDDEN = 256
SEL_BLOCK = 64
SEL_TOPK = 16
SEL_LOCAL = 2
WINDOW = 512
DN_HEADS = 8
DN_CONV = 4
DN_CHUNK = 64
REL_BUCKETS = 32
REL_MAX_DIST = 1024
N_EXPERTS = 16
N_GROUPS = 4
EXPERTS_PER_GROUP = N_EXPERTS // N_GROUPS
TOPK = 2
D_EXPERT = 512
NORM_EPS = 1e-6
FORCE_SCORE = 1e9
NEG = -1e30

NSA_Q = NSA_HEADS * HEAD_DIM
NSA_KV = NSA_GROUPS * HEAD_DIM
DN_W = DN_HEADS * HEAD_DIM
IN_SIZES = (NSA_Q, 6 * NSA_KV, 3 * NSA_HEADS, 3 * DN_W, DN_HEADS, DN_HEADS, DN_W, 2 * D_MODEL)
IN_OFFS = tuple(int(v) for v in np.cumsum((0,) + IN_SIZES))

LANES = 128
VMEM_LIMIT = 56 * 1024 * 1024

TM_PROJ = 512
TM_MERGE = 512
TQ = 128
TK_SEL = 512
CT_DN = 256
TM_MOE = 1024
BLK_MOE = 128
ROW_ALIGN = 8

_SEG_Q = (0, 512)
_SEG_CMP = (512, 768)
_SEG_K = (768, 1024)
_SEG_V = (1024, 1536)
_SEG_DN = (1536, 3072)
_SEG_Z = (3072, 3584)
_SEG_MG = (3584, 5632)
_SEG_SM = (5632, 5760)
N_PROJ = 5760
SM_GATE, SM_BETA, SM_A = 0, 24, 32


def _proj_weight(w_in):
    o = IN_OFFS
    kv = o[1]
    zeros = lambda n: jnp.zeros(w_in.shape[:2] + (n,), w_in.dtype)
    cut = lambda a, b: w_in[:, :, a:b]
    parts = [cut(o[0], o[1]),
             cut(kv, kv + 256),
             cut(kv + 256, kv + 384), cut(kv + 512, kv + 640)]
    for base in (kv + 384, kv + 640):
        parts += [cut(base, base + 64), zeros(128), cut(base + 64, base + 128)]
    parts += [cut(o[3], o[4]), cut(o[6], o[7]), cut(o[7], o[8]),
              cut(o[2], o[3]), cut(o[4], o[5]), cut(o[5], o[6])]
    width = sum(p.shape[2] for p in parts)
    parts.append(zeros(N_PROJ - width))
    return jnp.concatenate(parts, axis=2).astype(BF16)


def _rel_bucket_table(n):
    exact = REL_BUCKETS // 2
    d = np.arange(n, dtype=np.int64)
    far = np.maximum(d, exact).astype(np.float64)
    large = exact + (np.log(far / exact) / math.log(REL_MAX_DIST / exact) * (REL_BUCKETS - exact)).astype(np.int64)
    return np.where(d < exact, d, np.minimum(large, REL_BUCKETS - 1)).astype(np.int32)


def _cparams(sem, vmem=VMEM_LIMIT):
    return pltpu.CompilerParams(dimension_semantics=sem, vmem_limit_bytes=vmem)


def _bdot(a, b):
    return jnp.dot(a, b, preferred_element_type=F32)


def _dot_nt(a, b):
    return lax.dot_general(a, b, (((1,), (1,)), ((), ())), preferred_element_type=F32)


def _dot_tn(a, b):
    return lax.dot_general(a, b, (((0,), (0,)), ((), ())), preferred_element_type=F32)


def _split3(x):
    h = x.astype(BF16)
    r = x - h.astype(F32)
    m = r.astype(BF16)
    l = (r - m.astype(F32)).astype(BF16)
    return h, m, l


def _seg64_sumsq(x):
    rows, width = x.shape
    low = lax.broadcasted_iota(jnp.int32, (rows, LANES), 1) < 64
    outs = []
    for c in range(width // LANES):
        sq = x[:, c * LANES:(c + 1) * LANES]
        sq = sq * sq
        s_lo = jnp.sum(jnp.where(low, sq, 0.0), axis=-1, keepdims=True)
        s_hi = jnp.sum(jnp.where(low, 0.0, sq), axis=-1, keepdims=True)
        outs.append(jnp.where(low, s_lo, s_hi))
    return outs[0] if len(outs) == 1 else jnp.concatenate(outs, axis=1)


def _ada_kernel(c_ref, w_ref, b_ref, o_ref):
    ca = jax.nn.silu(c_ref[...]).astype(BF16)
    o_ref[0] = _bdot(ca, w_ref[0].astype(BF16)) + b_ref[0]


def _ada_mod(c, ada_w, ada_b):
    L, D, N = ada_w.shape
    B = c.shape[0]
    tn = 1536
    out = pl.pallas_call(
        _ada_kernel,
        grid=(L, N // tn),
        in_specs=[pl.BlockSpec((B, D), lambda l, j: (0, 0)),
                  pl.BlockSpec((1, D, tn), lambda l, j: (l, 0, j)),
                  pl.BlockSpec((1, 1, tn), lambda l, j: (l, 0, j))],
        out_specs=pl.BlockSpec((1, B, tn), lambda l, j: (l, 0, j)),
        out_shape=jax.ShapeDtypeStruct((L, B, N), F32),
        compiler_params=_cparams(("arbitrary", "arbitrary")),
        name="ada_mod",
    )(c, ada_w, ada_b.reshape(L, 1, N))
    return out.reshape(L, B, 6, D)


def _inproj_kernel(x_ref, mod_ref, ng_ref, w_ref, qg_ref, kg_ref,
                   q_out, cmp_out, k_out, v_out, dn_out, z_out, mg_out, sm_out):
    x = x_ref[...]
    ms = jnp.mean(x * x, axis=-1, keepdims=True)
    h = x * lax.rsqrt(ms + NORM_EPS) * ng_ref[...]
    h = h * (1.0 + mod_ref[0, 1:2, :]) + mod_ref[0, 0:1, :]
    hb = h.astype(BF16)

    def seg(ab):
        return _bdot(hb, w_ref[:, ab[0]:ab[1]])

    q = seg(_SEG_Q)
    q = q * lax.rsqrt(_seg64_sumsq(q) * (1.0 / HEAD_DIM) + NORM_EPS) * qg_ref[...]
    q_out[...] = (q * (HEAD_DIM ** -0.5)).astype(BF16)
    cmp_out[...] = seg(_SEG_CMP)
    k = seg(_SEG_K)
    k = k * lax.rsqrt(_seg64_sumsq(k) * (1.0 / HEAD_DIM) + NORM_EPS) * kg_ref[...]
    k_out[...] = k.astype(BF16)
    v = seg(_SEG_V)
    lane = lax.broadcasted_iota(jnp.int32, v.shape, 1) % 256
    ones = jnp.where((lane >= 64) & (lane < 192), 1.0, 0.0)
    v_out[...] = (v + ones).astype(BF16)
    dn_out[...] = seg(_SEG_DN)
    z_out[...] = seg(_SEG_Z)
    mg_out[...] = seg(_SEG_MG)
    sm_out[...] = seg(_SEG_SM)


def _in_proj(x2, mod_l, norm_g, w_proj, q_gain, k_gain, seq):
    T, D = x2.shape
    tm = TM_PROJ
    tpb = seq // tm
    widths = [(512, BF16), (256, F32), (256, BF16), (512, BF16), (1536, F32), (512, F32), (2048, F32), (128, F32)]
    return pl.pallas_call(
        _inproj_kernel,
        grid=(T // tm,),
        in_specs=[pl.BlockSpec((tm, D), lambda i: (i, 0)),
                  pl.BlockSpec((1, 6, D), lambda i: (i // tpb, 0, 0)),
                  pl.BlockSpec((1, D), lambda i: (0, 0)),
                  pl.BlockSpec((D, N_PROJ), lambda i: (0, 0)),
                  pl.BlockSpec((1, 512), lambda i: (0, 0)),
                  pl.BlockSpec((1, 256), lambda i: (0, 0))],
        out_specs=[pl.BlockSpec((tm, w), lambda i: (i, 0)) for w, _ in widths],
        out_shape=[jax.ShapeDtypeStruct((T, w), dt) for w, dt in widths],
        compiler_params=_cparams(("arbitrary",)),
        name="in_proj",
    )(x2, mod_l, norm_g, w_proj, q_gain, k_gain)


def _compress_kernel(kraw_ref, vraw_ref, pos_ref, w1_ref, w2_ref, kg_ref, kc_out, vc_out, *, n_chunks):
    for j, raw_ref, out_ref in ((0, kraw_ref, kc_out), (1, vraw_ref, vc_out)):
        top = [jnp.zeros((n_chunks, CMP_HIDDEN), F32) for _ in range(NSA_GROUPS)]
        bot = [jnp.zeros((n_chunks, CMP_HIDDEN), F32) for _ in range(NSA_GROUPS)]
        for r in range(CMP_STRIDE):
            xr = raw_ref[0, pl.ds(r, n_chunks, stride=CMP_STRIDE), :]
            x_top = (xr + pos_ref[j, r:r + 1, :]).astype(BF16)
            x_bot = (xr + pos_ref[j, CMP_STRIDE + r:CMP_STRIDE + r + 1, :]).astype(BF16)
            for g in range(NSA_GROUPS):
                ls = slice(g * HEAD_DIM, (g + 1) * HEAD_DIM)
                top[g] = top[g] + _bdot(x_top[:, ls], w1_ref[j, r * HEAD_DIM:(r + 1) * HEAD_DIM, :])
                bot[g] = bot[g] + _bdot(x_bot[:, ls], w1_ref[j, (CMP_STRIDE + r) * HEAD_DIM:(CMP_STRIDE + r + 1) * HEAD_DIM, :])
        outs = []
        for g in range(NSA_GROUPS):
            hid = top[g] + pltpu.roll(bot[g], n_chunks - 1, 0)
            o = _bdot(jax.nn.gelu(hid).astype(BF16), w2_ref[j])
            if j == 0:
                ms = jnp.mean(o * o, axis=-1, keepdims=True)
                o = o * lax.rsqrt(ms + NORM_EPS) * kg_ref[...]
            outs.append(o)
        out_ref[0] = jnp.concatenate(outs, axis=1).astype(BF16)


def _compress(cmp_raw, pos128, w1, w2, k_gain):
    B, S, _ = cmp_raw.shape
    nc = S // CMP_STRIDE
    return pl.pallas_call(
        functools.partial(_compress_kernel, n_chunks=nc),
        grid=(B,),
        in_specs=[pl.BlockSpec((1, S, LANES), lambda b: (b, 0, 0)),
                  pl.BlockSpec((1, S, LANES), lambda b: (b, 0, 1)),
                  pl.BlockSpec((2, CMP_BLOCK, LANES), lambda b: (0, 0, 0)),
                  pl.BlockSpec((2, CMP_BLOCK * HEAD_DIM, CMP_HIDDEN), lambda b: (0, 0, 0)),
                  pl.BlockSpec((2, CMP_HIDDEN, HEAD_DIM), lambda b: (0, 0, 0)),
                  pl.BlockSpec((1, HEAD_DIM), lambda b: (0, 0))],
        out_specs=[pl.BlockSpec((1, nc, LANES), lambda b: (b, 0, 0))] * 2,
        out_shape=[jax.ShapeDtypeStruct((B, nc, LANES), BF16)] * 2,
        compiler_params=_cparams(("arbitrary",)),
        name="nsa_compress",
    )(cmp_raw, cmp_raw, pos128, w1, w2, k_gain)


def _nsa_tables(rel_bias, seq):
    nq = seq // TQ
    ncp = seq // CMP_STRIDE
    n_dist = seq + TQ
    buckets = _rel_bucket_table(n_dist)
    ext = jnp.concatenate([jnp.full((NSA_HEADS, TQ), NEG, F32),
                           rel_bias.astype(F32)[jnp.asarray(buckets)].T], axis=1)
    ext_at = lambda a, b: ext[:, a + TQ:b + TQ]

    def skew(w, n_rows, stride, n_cols):
        length = w.shape[-1]
        a = jnp.tile(w, (1,) * (w.ndim - 1) + (n_rows,))[..., :n_rows * (length - stride)]
        return a.reshape(w.shape[:-1] + (n_rows, length - stride))[..., :n_cols]

    def toeplitz_tiles(src, n_tiles):
        blocks = src[:, :TQ * (n_tiles + 1)].reshape(NSA_HEADS, n_tiles + 1, TQ)
        pair = jnp.concatenate([blocks[:, :-1], blocks[:, 1:]], axis=-1)
        w = jnp.roll(pair[..., ::-1], -(TQ - 1), axis=-1)
        t = skew(w, TQ, 1, LANES)
        return t.transpose(1, 0, 2, 3).reshape(n_tiles, NSA_GROUPS, NSA_HPG * TQ, LANES)

    dlim = int(np.argmax(buckets == REL_BUCKETS - 1))
    dt = min(nq - 1, -(-(dlim + TQ - 1) // TQ))
    sel_tab = jnp.concatenate([jnp.full((1, NSA_GROUPS, NSA_HPG * TQ, LANES), NEG, F32),
                               toeplitz_tiles(ext, dt + 1)])
    nwt = WINDOW // LANES + 1
    in_window = jnp.asarray(np.arange(-TQ, n_dist) < WINDOW)
    win_tab = toeplitz_tiles(jnp.where(in_window[None, :], ext, NEG), nwt)[::-1]
    tail = CMP_STRIDE * ncp
    w = jnp.concatenate([ext_at(-(CMP_BLOCK - 1), seq - (CMP_BLOCK - 1)), jnp.full((NSA_HEADS, tail), NEG, F32)], axis=1)
    cmp_tab = skew(w, ncp, CMP_STRIDE, seq)
    cmp_tab = jnp.where((jnp.arange(ncp) < ncp - 1)[None, :, None], cmp_tab, NEG).transpose(0, 2, 1)
    cmp_tab = cmp_tab.reshape(NSA_GROUPS, NSA_HPG, nq, TQ, ncp).transpose(2, 0, 1, 3, 4)
    cmp_tab = cmp_tab.reshape(nq, NSA_GROUPS, NSA_HPG * TQ, ncp)
    return sel_tab, win_tab, cmp_tab, dt


def _nsa_static(seq):
    nb = seq // SEL_BLOCK
    ncp = seq // CMP_STRIDE
    ratio, nsub = SEL_BLOCK // CMP_STRIDE, CMP_BLOCK // CMP_STRIDE
    delta = np.arange(ncp)[:, None] - ratio * np.arange(nb)[None, :]
    m_idx = delta[..., None] + np.arange(nsub)
    overlap = np.sum((m_idx >= 0) & (m_idx < ratio), axis=-1).astype(np.float32)
    overlap[ncp - 1, :] = 0.0
    expand = (np.arange(nb)[:, None] == (np.arange(seq)[None, :] // SEL_BLOCK)).astype(np.float32)
    return jnp.asarray(overlap.T, BF16), jnp.asarray(expand, BF16)


def _nsa_kernel(q_ref, sm_ref, kc_ref, vc_ref, k_ref, v_ref, bc_ref, ws_ref, ww_ref, ovt_ref, ex_ref,
                o_ref, m_s, acc_s, *, n_blocks, n_sel, dt):
    i = pl.program_id(1)
    t0 = i * TQ
    rows = NSA_HPG * TQ
    half = lax.broadcasted_iota(jnp.int32, (TQ, LANES), 1) // HEAD_DIM
    q = q_ref[0].astype(F32)
    gates = jax.nn.sigmoid(sm_ref[0])
    ng = NSA_GROUPS
    arows = ng * rows
    lane2 = lax.broadcasted_iota(jnp.int32, (n_blocks, ng * TQ), 1)
    jb = lax.broadcasted_iota(jnp.int32, (n_blocks, ng * TQ), 0)
    cur = (t0 + lane2 % TQ) // SEL_BLOCK
    valid = jb <= cur
    forced = valid & ((jb == 0) | (jb > cur - SEL_LOCAL))
    sub = TK_SEL // LANES
    y_heads = []

    def block_max(blocks):
        bm = blocks[0]
        for b in blocks[1:]:
            bm = jnp.maximum(bm, b)
        return jnp.max(bm, axis=-1, keepdims=True)

    def softmax_pv(blocks, v_tiles, m):
        p = jnp.concatenate([jnp.exp(b - m) for b in blocks], axis=1).astype(BF16)
        return jnp.concatenate([_bdot(p[g * rows:(g + 1) * rows], v_tiles[g]) for g in range(ng)], axis=0)

    def online_update(blocks, v_tiles, m_ref, acc_ref):
        m_old = m_ref[...]
        m_new = jnp.maximum(m_old, block_max(blocks))
        acc_ref[...] = jnp.exp(m_old - m_new) * acc_ref[...] + softmax_pv(blocks, v_tiles, m_new)
        m_ref[...] = m_new

    stack = []
    for g in range(ng):
        for h in range(NSA_HPG):
            hd = NSA_HPG * g + h
            blk = q[:, (hd // 2) * LANES:(hd // 2 + 1) * LANES]
            if hd % 2 != g:
                blk = pltpu.roll(blk, HEAD_DIM, 1)
            stack.append(jnp.where(half == g, blk, 0.0))
    qa = jnp.concatenate(stack, axis=0).astype(BF16)
    group_half = jnp.concatenate([half == g for g in range(ng) for _ in range(NSA_HPG)], axis=0)

    bc = bc_ref[0].reshape(arows, bc_ref.shape[-1])
    sc = _dot_nt(qa, kc_ref[0]) + bc
    ec = jnp.exp(sc - jnp.max(sc, axis=-1, keepdims=True))
    ec = jnp.where(bc > 0.5 * NEG, ec, 0.0)
    lc = jnp.sum(ec, axis=-1, keepdims=True)
    pc = ec / jnp.where(lc > 0.0, lc, 1.0)
    o_c = _bdot(pc.astype(BF16), vc_ref[0])

    psum = jnp.concatenate(
        [sum(pc[g * rows + h * TQ:g * rows + (h + 1) * TQ] for h in range(NSA_HPG)) for g in range(ng)], axis=0)
    p_hi = psum.astype(BF16)
    p_lo = (psum - p_hi.astype(F32)).astype(BF16)
    imp = _dot_nt(ovt_ref[...], p_hi) + _dot_nt(ovt_ref[...], p_lo)
    score = jnp.where(forced, FORCE_SCORE, jnp.where(valid, imp, -1.0))

    def pick(_, carry):
        sc_, sel_ = carry
        top = jnp.max(sc_, axis=0, keepdims=True)
        first = jnp.min(jnp.where(sc_ == top, jb, n_blocks), axis=0, keepdims=True)
        hit = jb == first
        return jnp.where(hit, -3e38, sc_), jnp.where(hit, 1.0, sel_)

    _, sel = lax.fori_loop(0, n_sel, pick, (score, jnp.zeros((n_blocks, ng * TQ), F32)))
    sel_q = jnp.transpose(sel).astype(BF16)

    m_s[...] = jnp.full((arows, LANES), -3e38, F32)
    acc_s[...] = jnp.zeros((arows, LANES), F32)

    def sel_body(jt, carry):
        ks = pl.multiple_of(jt * TK_SEL, TK_SEL)
        s = _dot_nt(qa, k_ref[0, pl.ds(ks, TK_SEL), 0:LANES])
        selx = _bdot(sel_q, ex_ref[:, pl.ds(ks, TK_SEL)])
        madd = (selx - 1.0) * (-NEG)
        parts = []
        for c in range(sub):
            d = i - (jt * sub + c)
            b = ws_ref[jnp.clip(d, -1, dt) + 1].reshape(arows, LANES)
            sc_ = (s[:, c * LANES:(c + 1) * LANES] + b).reshape(ng, NSA_HPG, TQ, LANES)
            sc_ = sc_ + madd[:, c * LANES:(c + 1) * LANES].reshape(ng, 1, TQ, LANES)
            parts.append(sc_.reshape(arows, LANES))
        online_update(parts, [v_ref[0, pl.ds(ks, TK_SEL), g * LANES:(g + 1) * LANES] for g in range(ng)],
                      m_s, acc_s)
        return carry

    lax.fori_loop(0, (t0 + TQ + TK_SEL - 1) // TK_SEL, sel_body, 0)

    nwt = WINDOW // LANES + 1
    starts = [t0 - WINDOW + c * LANES for c in range(nwt)]
    clamped = [pl.multiple_of(jnp.maximum(ks, 0), LANES) for ks in starts]
    kw = jnp.concatenate([k_ref[0, pl.ds(ks, LANES), LANES:2 * LANES] for ks in clamped], axis=0)
    vws = [jnp.concatenate([v_ref[0, pl.ds(ks, LANES), (2 + g) * LANES:(3 + g) * LANES] for ks in clamped], axis=0)
           for g in range(ng)]
    sw = _dot_nt(qa, kw)
    blocks = [sw[:, c * LANES:(c + 1) * LANES]
              + jnp.where(starts[c] >= 0, ww_ref[c].reshape(arows, LANES), NEG) for c in range(nwt)]

    def finish(acc):
        out = acc / pltpu.roll(acc, HEAD_DIM, 1)
        return jnp.where(group_half, out, 0.0)

    o_s = finish(acc_s[...])
    o_w = finish(softmax_pv(blocks, vws, block_max(blocks)))
    for g in range(ng):
        for h in range(NSA_HPG):
            hd = NSA_HPG * g + h
            rs = slice(g * rows + h * TQ, g * rows + (h + 1) * TQ)
            y = (gates[:, 3 * hd:3 * hd + 1] * o_c[rs] + gates[:, 3 * hd + 1:3 * hd + 2] * o_s[rs]
                 + gates[:, 3 * hd + 2:3 * hd + 3] * o_w[rs])
            y = jnp.where(half == g, y, 0.0)
            if hd % 2 != g:
                y = pltpu.roll(y, HEAD_DIM, 1)
            y_heads.append(y)

    o_ref[0] = jnp.concatenate([y_heads[2 * c] + y_heads[2 * c + 1] for c in range(NSA_HEADS // 2)],
                               axis=1).astype(BF16)


def _nsa_attention(q, sm, kc, vc, k_sw, v_sw, tabs, statics):
    B, S, _ = q.shape
    sel_tab, win_tab, cmp_tab, dt = tabs
    ovt, expand = statics
    nq = S // TQ
    ncp = S // CMP_STRIDE
    nb = S // SEL_BLOCK
    rows = NSA_HPG * TQ
    const = lambda nd: (lambda b, i: (0,) * nd)
    return pl.pallas_call(
        functools.partial(_nsa_kernel, n_blocks=nb, n_sel=min(SEL_TOPK, nb), dt=dt),
        grid=(B, nq),
        in_specs=[pl.BlockSpec((1, TQ, NSA_Q), lambda b, i: (b, i, 0)),
                  pl.BlockSpec((1, TQ, LANES), lambda b, i: (b, i, 0)),
                  pl.BlockSpec((1, ncp, LANES), lambda b, i: (b, 0, 0)),
                  pl.BlockSpec((1, ncp, LANES), lambda b, i: (b, 0, 0)),
                  pl.BlockSpec((1, S, 256), lambda b, i: (b, 0, 0)),
                  pl.BlockSpec((1, S, 512), lambda b, i: (b, 0, 0)),
                  pl.BlockSpec((1, NSA_GROUPS, rows, ncp), lambda b, i: (i, 0, 0, 0)),
                  pl.BlockSpec(sel_tab.shape, const(4)),
                  pl.BlockSpec(win_tab.shape, const(4)),
                  pl.BlockSpec(ovt.shape, const(2)),
                  pl.BlockSpec(expand.shape, const(2))],
        out_specs=pl.BlockSpec((1, TQ, NSA_Q), lambda b, i: (b, i, 0)),
        out_shape=jax.ShapeDtypeStruct((B, S, NSA_Q), BF16),
        scratch_shapes=[pltpu.VMEM((NSA_GROUPS * rows, LANES), F32), pltpu.VMEM((NSA_GROUPS * rows, LANES), F32)],
        compiler_params=_cparams(("arbitrary", "arbitrary")),
        name="nsa_attention",
    )(q, sm, kc, vc, k_sw, v_sw, cmp_tab, sel_tab, win_tab, ovt, expand)


DN_QUAD = 4
DN_QW = DN_QUAD * HEAD_DIM


def _block_diag(a, head_masks):
    zero = jnp.zeros_like(a)
    return jnp.concatenate([jnp.where(m, a, zero) for m in head_masks], axis=0)


def _dn_kernel(qkv_ref, sm_ref, z_ref, cw_ref, alog_ref, dtb_ref, ng_ref, eb_ref, ea_ref, o_ref,
               xbuf, state, q_s, k_s, v_s, b_s, gc_s, out_s):
    j = pl.program_id(1)
    ct = CT_DN
    c = DN_CHUNK

    @pl.when(j == 0)
    def _():
        xbuf[0:8, :] = jnp.zeros((8, 3 * DN_W), F32)
        state[...] = jnp.zeros(state.shape, F32)

    xbuf[8:8 + ct, :] = qkv_ref[0]
    acc = cw_ref[0:1, :] * xbuf[5:5 + ct, :]
    for tap in range(1, DN_CONV):
        acc = acc + cw_ref[tap:tap + 1, :] * xbuf[5 + tap:5 + tap + ct, :]
    xbuf[0:8, :] = xbuf[ct:ct + 8, :]
    y = jax.nn.silu(acc)
    qh = y[:, 0:DN_W]
    kh = y[:, DN_W:2 * DN_W]
    q_s[...] = qh * lax.rsqrt(_seg64_sumsq(qh) + NORM_EPS) * (HEAD_DIM ** -0.5)
    k_s[...] = kh * lax.rsqrt(_seg64_sumsq(kh) + NORM_EPS)
    v_s[...] = y[:, 2 * DN_W:3 * DN_W]
    sm = sm_ref[0]
    b_s[...] = sum(_bdot(p, eb_ref[...]) for p in _split3(jax.nn.sigmoid(sm)))
    gdec = -jnp.exp(alog_ref[...]) * jax.nn.softplus(sm + dtb_ref[...])
    g_wide = [_bdot(p, ea_ref[...]).astype(BF16) for p in _split3(gdec)]
    row = lax.broadcasted_iota(jnp.int32, (c, DN_QW), 0)
    col = lax.broadcasted_iota(jnp.int32, (c, DN_QW), 1) % c
    causal = row >= col
    strict = row > col
    diag = row == col
    eye = diag.astype(F32)
    masks = [lax.broadcasted_iota(jnp.int32, (c, DN_QW), 1) // HEAD_DIM == h for h in range(DN_QUAD)]
    tril = (lax.broadcasted_iota(jnp.int32, (c, c), 0) >= lax.broadcasted_iota(jnp.int32, (c, c), 1)).astype(BF16)
    for ch in range(ct // c):
        rs = slice(ch * c, (ch + 1) * c)
        gc_s[rs, :] = sum(_bdot(tril, gw[rs]) for gw in g_wide)

    n_quads = DN_HEADS // DN_QUAD
    n_chunks = ct // c
    items = [(ch, qd) for ch in range(n_chunks) for qd in range(n_quads)]
    sl = lambda it: (slice(it[0] * c, (it[0] + 1) * c), slice(it[1] * DN_QW, (it[1] + 1) * DN_QW))
    every = lambda f, *lists: [f(*args) for args in zip(*lists)]
    bd = lambda a: _block_diag(a.astype(BF16), masks)
    q4 = [q_s[sl(it)] for it in items]
    k4 = [k_s[sl(it)] for it in items]
    b4 = [b_s[sl(it)] for it in items]
    gc4 = [gc_s[sl(it)] for it in items]
    vb = [v_s[sl(it)] * b for it, b in zip(items, b4)]
    g_key = every(lambda g: jnp.sum(jnp.where(diag, g, 0.0), axis=0, keepdims=True), gc4)
    decay = every(lambda g, gk: jnp.where(causal, jnp.exp(jnp.where(causal, g - gk, 0.0)), 0.0), gc4, g_key)
    eg = every(jnp.exp, gc4)
    kb = every(lambda k, b: k * b, k4, b4)
    k_bd = every(bd, k4)
    low = every(lambda a, kd, dc: jnp.where(strict, _dot_nt(a.astype(BF16), kd) * dc, 0.0), kb, k_bd, decay)
    attn = every(lambda a, kd, dc: jnp.where(causal, _dot_nt(a.astype(BF16), kd) * dc, 0.0), q4, k_bd, decay)
    x = every(lambda lo: eye - lo, low)
    p = every(lambda lo: _bdot(lo.astype(BF16), bd(lo)), low)
    steps = int(math.log2(c)) - 1
    for s in range(steps):
        pd = every(bd, p)
        x = every(lambda xi, pdi: xi + _bdot(xi.astype(BF16), pdi), x, pd)
        if s + 1 < steps:
            p = every(lambda pi, pdi: _bdot(pi.astype(BF16), pdi), p, pd)
    t_inv = every(lambda xi: xi.astype(BF16), x)
    u = every(lambda t, a: _bdot(t, bd(a)), t_inv, vb)
    w = every(lambda t, a, e: _bdot(t, bd(a * e)), t_inv, kb, eg)
    qe = every(lambda a, e: (a * e).astype(BF16), q4, eg)
    g_last = every(lambda g: g[c - 1:c, :], gc4)
    k_dec = every(lambda k, gl, g: (k * jnp.exp(gl - g)).astype(BF16), k4, g_last, gc4)

    st = [state[qd] for qd in range(n_quads)]
    for ch in range(n_chunks):
        idx = [ch * n_quads + qd for qd in range(n_quads)]
        s_bd = every(bd, st)
        v_new = [(u[i] - _bdot(w[i].astype(BF16), sb)).astype(BF16) for i, sb in zip(idx, s_bd)]
        outs = [_bdot(qe[i], sb) + _bdot(attn[i].astype(BF16), _block_diag(vn, masks))
                for i, sb, vn in zip(idx, s_bd, v_new)]
        cross = [_dot_tn(k_dec[i], vn) for i, vn in zip(idx, v_new)]
        for qd, i in enumerate(idx):
            out_s[sl(items[i])] = outs[qd]
            upd = sum(jnp.where(masks[h], cross[qd][h * HEAD_DIM:(h + 1) * HEAD_DIM, :], 0.0) for h in range(DN_QUAD))
            st[qd] = st[qd] * jnp.exp(g_last[i]) + upd
    for qd in range(n_quads):
        state[qd] = st[qd]

    o = out_s[...]
    o = o * lax.rsqrt(_seg64_sumsq(o) * (1.0 / HEAD_DIM) + NORM_EPS) * ng_ref[...]
    o_ref[0] = (o * jax.nn.silu(z_ref[0])).astype(BF16)


def _deltanet(dn_qkv, sm, z, conv_w, alog128, dtb128, ng512):
    B, S, _ = dn_qkv.shape
    ct = CT_DN
    head_of_lane = np.arange(DN_W) // HEAD_DIM
    spread = lambda off: jnp.asarray(np.arange(LANES)[:, None] == off + head_of_lane[None, :], BF16)
    return pl.pallas_call(
        _dn_kernel,
        grid=(B, S // ct),
        in_specs=[pl.BlockSpec((1, ct, 3 * DN_W), lambda b, j: (b, j, 0)),
                  pl.BlockSpec((1, ct, LANES), lambda b, j: (b, j, 0)),
                  pl.BlockSpec((1, ct, DN_W), lambda b, j: (b, j, 0)),
                  pl.BlockSpec((DN_CONV, 3 * DN_W), lambda b, j: (0, 0)),
                  pl.BlockSpec((1, LANES), lambda b, j: (0, 0)),
                  pl.BlockSpec((1, LANES), lambda b, j: (0, 0)),
                  pl.BlockSpec((1, DN_W), lambda b, j: (0, 0)),
                  pl.BlockSpec((LANES, DN_W), lambda b, j: (0, 0)),
                  pl.BlockSpec((LANES, DN_W), lambda b, j: (0, 0))],
        out_specs=pl.BlockSpec((1, ct, DN_W), lambda b, j: (b, j, 0)),
        out_shape=jax.ShapeDtypeStruct((B, S, DN_W), BF16),
        scratch_shapes=[pltpu.VMEM((ct + 8, 3 * DN_W), F32),
                        pltpu.VMEM((DN_HEADS // DN_QUAD, HEAD_DIM, DN_QW), F32),
                        pltpu.VMEM((ct, DN_W), F32), pltpu.VMEM((ct, DN_W), F32), pltpu.VMEM((ct, DN_W), F32),
                        pltpu.VMEM((ct, DN_W), F32), pltpu.VMEM((ct, DN_W), F32),
                        pltpu.VMEM((ct, DN_W), F32)],
        compiler_params=_cparams(("arbitrary", "arbitrary")),
        name="gated_deltanet",
    )(dn_qkv, sm, z, conv_w, alog128, dtb128, ng512, spread(SM_BETA), spread(SM_A))


def _merge_kernel(ya_ref, yb_ref, mg_ref, x_ref, mod_ref, wa_ref, wb_ref, wo_ref, ng_ref, rw_ref,
                  xo_ref, hp_ref, lg_ref):
    d = D_MODEL
    m = jax.nn.sigmoid(mg_ref[...])
    y = m[:, :d] * _bdot(ya_ref[...], wa_ref[...]) + m[:, d:] * _bdot(yb_ref[...], wb_ref[...])
    xn = x_ref[...] + mod_ref[0, 2:3, :] * _bdot(y.astype(BF16), wo_ref[...])
    xo_ref[...] = xn
    ms = jnp.mean(xn * xn, axis=-1, keepdims=True)
    h = xn * lax.rsqrt(ms + NORM_EPS) * ng_ref[...]
    h = h * (1.0 + mod_ref[0, 4:5, :]) + mod_ref[0, 3:4, :]
    lg_ref[...] = lax.dot_general(rw_ref[...], h, (((1,), (1,)), ((), ())), preferred_element_type=F32,
                                  precision=lax.Precision.HIGHEST)
    bits = pltpu.bitcast(h.astype(BF16).astype(F32), jnp.uint32)
    hp_ref[...] = (bits[:, :d // 2] & jnp.uint32(0xFFFF0000)) | (bits[:, d // 2:] >> 16)


def _merge(ya, yb, mg, x2, mod_l, wa, wb, wo, norm_g, router_wt, seq):
    T, D = x2.shape
    tm = TM_MERGE
    tpb = seq // tm
    row = lambda w: pl.BlockSpec((tm, w), lambda i: (i, 0))
    full = lambda a: pl.BlockSpec(a.shape, lambda i: (0,) * a.ndim)
    return pl.pallas_call(
        _merge_kernel,
        grid=(T // tm,),
        in_specs=[row(NSA_Q), row(DN_W), row(2 * D), row(D),
                  pl.BlockSpec((1, 6, D), lambda i: (i // tpb, 0, 0)),
                  full(wa), full(wb), full(wo), full(norm_g), full(router_wt)],
        out_specs=[row(D), row(D // 2), pl.BlockSpec((N_EXPERTS, tm), lambda i: (0, i))],
        out_shape=[jax.ShapeDtypeStruct((T, D), F32), jax.ShapeDtypeStruct((T, D // 2), jnp.uint32),
                   jax.ShapeDtypeStruct((N_EXPERTS, T), F32)],
        compiler_params=_cparams(("arbitrary",)),
        name="merge_out",
    )(ya, yb, mg, x2, mod_l, wa, wb, wo, norm_g, router_wt)


def _moe_rows(tm):
    return -(-(TOPK * tm + N_EXPERTS * (ROW_ALIGN - 1) + BLK_MOE) // ROW_ALIGN) * ROW_ALIGN


def _first_max(vals):
    best = vals[0]
    for v in vals[1:]:
        best = jnp.maximum(best, v)
    idx = jnp.full(best.shape, len(vals) - 1, jnp.int32)
    for j in range(len(vals) - 2, -1, -1):
        idx = jnp.where(vals[j] == best, j, idx)
    return best, idx


def _pick(vals, idx):
    out = vals[-1]
    for j in range(len(vals) - 2, -1, -1):
        out = jnp.where(idx == j, vals[j], out)
    return out


def _route_kernel(lg_ref, rb_ref, tri_ref, low_ref, slot_ref, wt_ref, seg_ref, *, tm):
    epg = EXPERTS_PER_GROUP
    scores = jax.nn.sigmoid(lg_ref[...])
    sel = scores + rb_ref[...]
    s_rows = [sel[e:e + 1, :] for e in range(N_EXPERTS)]
    p_rows = [scores[e:e + 1, :] for e in range(N_EXPERTS)]
    grp = []
    for g in range(N_GROUPS):
        a, b, c, d = s_rows[epg * g:epg * g + epg]
        hi1, lo1, hi2, lo2 = jnp.maximum(a, b), jnp.minimum(a, b), jnp.maximum(c, d), jnp.minimum(c, d)
        second = jnp.maximum(jnp.minimum(hi1, hi2), jnp.where(hi1 >= hi2, lo1, lo2))
        grp.append(jnp.maximum(hi1, hi2) + second)
    _, gidx = _first_max(grp)
    cs = [_pick([s_rows[epg * g + j] for g in range(N_GROUPS)], gidx) for j in range(epg)]
    cp = [_pick([p_rows[epg * g + j] for g in range(N_GROUPS)], gidx) for j in range(epg)]
    _, j1 = _first_max(cs)
    _, j2 = _first_max([jnp.where(j1 == j, -jnp.inf, cs[j]) for j in range(epg)])
    w1 = _pick(cp, j1)
    w2 = _pick(cp, j2)
    den = w1 + w2
    wt_ref[0] = jnp.concatenate([w1 / den, w2 / den], axis=0)
    e1 = epg * gidx + j1
    e2 = epg * gidx + j2

    erow = lax.broadcasted_iota(jnp.int32, (N_EXPERTS, tm), 0)
    oh0 = (erow == e1).astype(F32)
    oh1 = (erow == e2).astype(F32)
    cum = _bdot(jnp.concatenate([oh0, oh1], axis=0).astype(BF16), tri_ref[...])
    cum0, cum1 = cum[:N_EXPERTS], cum[N_EXPERTS:]
    cnt0 = cum0[:, tm - 1:tm]
    counts = cnt0 + cum1[:, tm - 1:tm]
    padded = jnp.floor((counts + (ROW_ALIGN - 1)) * (1.0 / ROW_ALIGN)) * ROW_ALIGN
    start = jnp.dot(low_ref[...], jnp.broadcast_to(padded, (N_EXPERTS, LANES)),
                    preferred_element_type=F32, precision=lax.Precision.HIGHEST)
    st = start[:, 0:1]
    slot0 = jnp.sum(oh0 * (st + cum0 - 1.0), axis=0, keepdims=True)
    slot1 = jnp.sum(oh1 * (st + cnt0 + cum1 - 1.0), axis=0, keepdims=True)
    slot_ref[0] = jnp.concatenate([slot0, slot1], axis=0).astype(jnp.int32)
    nch = jnp.floor((counts + (BLK_MOE - 1)) * (1.0 / BLK_MOE))
    seg_ref[0] = jnp.concatenate([start, jnp.broadcast_to(nch, (N_EXPERTS, LANES))], axis=0).astype(jnp.int32)


def _moe_route(logits_t, router_b, tm):
    E, T = logits_t.shape
    nt = T // tm
    tri = (jnp.arange(tm)[:, None] <= jnp.arange(tm)[None, :]).astype(BF16)
    low = (jnp.arange(E)[None, :] < jnp.arange(E)[:, None]).astype(F32)
    slot, wt, seg = pl.pallas_call(
        functools.partial(_route_kernel, tm=tm),
        grid=(nt,),
        in_specs=[pl.BlockSpec((E, tm), lambda i: (0, i)),
                  pl.BlockSpec((E, 1), lambda i: (0, 0)),
                  pl.BlockSpec((tm, tm), lambda i: (0, 0)),
                  pl.BlockSpec((E, E), lambda i: (0, 0))],
        out_specs=[pl.BlockSpec((1, TOPK, tm), lambda i: (i, 0, 0)),
                   pl.BlockSpec((1, TOPK, tm), lambda i: (i, 0, 0)),
                   pl.BlockSpec((1, 2 * E, LANES), lambda i: (i, 0, 0))],
        out_shape=[jax.ShapeDtypeStruct((nt, TOPK, tm), jnp.int32),
                   jax.ShapeDtypeStruct((nt, TOPK, tm), F32),
                   jax.ShapeDtypeStruct((nt, 2 * E, LANES), jnp.int32)],
        compiler_params=_cparams(("arbitrary",)),
        name="moe_route",
    )(logits_t, router_b.astype(F32)[:, None], tri, low)
    return slot, wt, seg[:, :, 0][:, None, :]


def _moe_kernel(slot_ref, wt_ref, seg_ref, hp_ref, x_ref, mod_ref, wg_ref, wu_ref, wd_ref,
                o_ref, gbuf, ybuf, *, tm):
    e = pl.program_id(1)
    half = D_MODEL // 2

    @pl.when(e == 0)
    def _():
        gbuf[...] = jnp.zeros(gbuf.shape, gbuf.dtype)

        def scatter(t, carry):
            row = hp_ref[pl.ds(t, 1), :]
            gbuf[pl.ds(slot_ref[0, 0, t], 1), :] = row
            gbuf[pl.ds(slot_ref[0, 1, t], 1), :] = row
            return carry
        lax.fori_loop(0, tm, scatter, 0, unroll=8)

    start = seg_ref[0, 0, e]

    def chunk(ci, carry):
        r0 = pl.multiple_of(start + ci * BLK_MOE, ROW_ALIGN)
        w = gbuf[pl.ds(r0, BLK_MOE), :]
        x_hi = pltpu.bitcast(w & jnp.uint32(0xFFFF0000), F32).astype(BF16)
        x_lo = pltpu.bitcast(w << 16, F32).astype(BF16)
        gt = _bdot(x_hi, wg_ref[0, 0, 0:half, :]) + _bdot(x_lo, wg_ref[0, 0, half:, :])
        up = _bdot(x_hi, wu_ref[0, 0, 0:half, :]) + _bdot(x_lo, wu_ref[0, 0, half:, :])
        act = (jax.nn.silu(gt) * up).astype(BF16)
        ybuf[pl.ds(r0, BLK_MOE), :] = _bdot(act, wd_ref[0, 0])
        return carry

    lax.fori_loop(0, seg_ref[0, 0, N_EXPERTS + e], chunk, 0)

    @pl.when(e == N_EXPERTS - 1)
    def _():
        def combine(t, carry):
            y0 = ybuf[pl.ds(slot_ref[0, 0, t], 1), :] * wt_ref[0, 0, t]
            y1 = ybuf[pl.ds(slot_ref[0, 1, t], 1), :] * wt_ref[0, 1, t]
            o_ref[pl.ds(t, 1), :] = y0 + y1
            return carry
        lax.fori_loop(0, tm, combine, 0, unroll=8)
        o_ref[...] = x_ref[...] + mod_ref[0, 5:6, :] * o_ref[...]


def _moe(hp, x2, mod_l, route, wg, wu, wd, layer, seq):
    T, D = x2.shape
    tm = TM_MOE
    nt = T // tm
    tpb = seq // tm
    n_rows = _moe_rows(tm)
    slot, wt, seg = route
    smem = lambda r, w: pl.BlockSpec((1, r, w), lambda i, e: (i, 0, 0), memory_space=pltpu.SMEM)
    return pl.pallas_call(
        functools.partial(_moe_kernel, tm=tm),
        grid=(nt, N_EXPERTS),
        in_specs=[smem(TOPK, tm), smem(TOPK, tm), smem(1, 2 * N_EXPERTS),
                  pl.BlockSpec((tm, D // 2), lambda i, e: (i, 0)),
                  pl.BlockSpec((tm, D), lambda i, e: (i, 0)),
                  pl.BlockSpec((1, 6, D), lambda i, e: (i // tpb, 0, 0)),
                  pl.BlockSpec((1, 1, D, D_EXPERT), lambda i, e: (layer, e, 0, 0)),
                  pl.BlockSpec((1, 1, D, D_EXPERT), lambda i, e: (layer, e, 0, 0)),
                  pl.BlockSpec((1, 1, D_EXPERT, D), lambda i, e: (layer, e, 0, 0))],
        out_specs=pl.BlockSpec((tm, D), lambda i, e: (i, 0)),
        out_shape=jax.ShapeDtypeStruct((T, D), F32),
        scratch_shapes=[pltpu.VMEM((n_rows, D // 2), jnp.uint32), pltpu.VMEM((n_rows, D), F32)],
        compiler_params=_cparams(("arbitrary", "arbitrary")),
        name="moe_ffn",
    )(slot, wt, seg, hp, x2, mod_l, wg, wu, wd)


def _pad_lanes(v, offset):
    return jnp.zeros((1, LANES), F32).at[0, offset:offset + v.shape[0]].set(v.astype(F32))


def kernel(x, c, rel_bias, router_w, router_b, ada_w, ada_b, norm1_g, norm2_g, w_in, qk_norm_g,
           cmp_pos, cmp_w1, cmp_w2, dn_conv_w, dn_a_log, dn_dt_bias, dn_norm_g, w_branch_a,
           w_branch_b, w_out, moe_w_gate, moe_w_up, moe_w_down):
    B, S, D = x.shape
    T = B * S
    L = ada_w.shape[0]

    mod = _ada_mod(c, ada_w, ada_b)
    w_proj = _proj_weight(w_in)
    tabs = _nsa_tables(rel_bias, S)
    statics = _nsa_static(S)
    router_wt = router_w.astype(F32).T
    wg, wu, wd = moe_w_gate.astype(BF16), moe_w_up.astype(BF16), moe_w_down.astype(BF16)

    x2 = x.reshape(T, D)
    for l in range(L):
        qkg = qk_norm_g[l].astype(F32)
        q_gain = jnp.tile(qkg[0], NSA_HEADS)[None, :]
        k_gain = jnp.concatenate([jnp.tile(qkg[2], NSA_GROUPS), jnp.tile(qkg[3], NSA_GROUPS)])[None, :]
        q, cmp_raw, k_sw, v_sw, dn_qkv, z, mg, sm = _in_proj(
            x2, mod[l], norm1_g[l][None, :], w_proj[l], q_gain, k_gain, S)

        pos128 = jnp.tile(cmp_pos[l].astype(F32), (1, 1, NSA_GROUPS))
        kc, vc = _compress(cmp_raw.reshape(B, S, 256), pos128, cmp_w1[l].astype(BF16),
                           cmp_w2[l].astype(BF16), qkg[1][None, :])
        y_a = _nsa_attention(q.reshape(B, S, NSA_Q), sm.reshape(B, S, LANES), kc, vc,
                             k_sw.reshape(B, S, 256), v_sw.reshape(B, S, 512), tabs, statics)

        y_b = _deltanet(dn_qkv.reshape(B, S, 3 * DN_W), sm.reshape(B, S, LANES), z.reshape(B, S, DN_W),
                        dn_conv_w[l].astype(F32), _pad_lanes(dn_a_log[l], SM_A),
                        _pad_lanes(dn_dt_bias[l], SM_A), jnp.tile(dn_norm_g[l].astype(F32), DN_HEADS)[None, :])

        x_mid, hp, logits = _merge(y_a.reshape(T, NSA_Q), y_b.reshape(T, DN_W), mg, x2, mod[l],
                                   w_branch_a[l].astype(BF16), w_branch_b[l].astype(BF16),
                                   w_out[l].astype(BF16), norm2_g[l][None, :], router_wt, S)
        route = _moe_route(logits, router_b, TM_MOE)
        x2 = _moe(hp, x_mid, mod[l], route, wg, wu, wd, l, S)
    return x2.reshape(B, S, D)
```

```python
import functools
import math

import numpy as np
import jax
import jax.numpy as jnp
from jax import lax
from jax.experimental import pallas as pl
from jax.experimental.pallas import tpu as pltpu

F32 = jnp.float32
BF16 = jnp.bfloat16

D_MODEL = 1024
DEPTH = 4
HEAD_DIM = 64
NSA_HEADS = 8
NSA_GROUPS = 2
NSA_HPG = NSA_HEADS // NSA_GROUPS
CMP_STRIDE = 16
CMP_BLOCK = 32
CMP_HIDDEN = 256
SEL_BLOCK = 64
SEL_TOPK = 16
SEL_LOCAL = 2
WINDOW = 512
DN_HEADS = 8
DN_CONV = 4
DN_CHUNK = 64
REL_BUCKETS = 32
REL_MAX_DIST = 1024
N_EXPERTS = 16
N_GROUPS = 4
EXPERTS_PER_GROUP = N_EXPERTS // N_GROUPS
TOPK = 2
D_EXPERT = 512
NORM_EPS = 1e-6
FORCE_SCORE = 1e9
NEG = -1e30

NSA_Q = NSA_HEADS * HEAD_DIM
NSA_KV = NSA_GROUPS * HEAD_DIM
DN_W = DN_HEADS * HEAD_DIM
IN_SIZES = (NSA_Q, 6 * NSA_KV, 3 * NSA_HEADS, 3 * DN_W, DN_HEADS, DN_HEADS, DN_W, 2 * D_MODEL)
IN_OFFS = tuple(int(v) for v in np.cumsum((0,) + IN_SIZES))

LANES = 128
VMEM_LIMIT = 56 * 1024 * 1024

TM_PROJ = 512
TM_MERGE = 512
TQ = 128
TK_SEL = 512
CT_DN = 256
TM_MOE = 1024
BLK_MOE = 128
ROW_ALIGN = 8

_SEG_Q = (0, 512)
_SEG_CMP = (512, 768)
_SEG_K = (768, 1024)
_SEG_V = (1024, 1536)
_SEG_DN = (1536, 3072)
_SEG_Z = (3072, 3584)
_SEG_MG = (3584, 5632)
_SEG_SM = (5632, 5760)
N_PROJ = 5760
SM_GATE, SM_BETA, SM_A = 0, 24, 32


def _proj_weight(w_in):
    o = IN_OFFS
    kv = o[1]
    zeros = lambda n: jnp.zeros(w_in.shape[:2] + (n,), w_in.dtype)
    cut = lambda a, b: w_in[:, :, a:b]
    parts = [cut(o[0], o[1]),
             cut(kv, kv + 256),
             cut(kv + 256, kv + 384), cut(kv + 512, kv + 640)]
    for base in (kv + 384, kv + 640):
        parts += [cut(base, base + 64), zeros(128), cut(base + 64, base + 128)]
    parts += [cut(o[3], o[4]), cut(o[6], o[7]), cut(o[7], o[8]),
              cut(o[2], o[3]), cut(o[4], o[5]), cut(o[5], o[6])]
    width = sum(p.shape[2] for p in parts)
    parts.append(zeros(N_PROJ - width))
    return jnp.concatenate(parts, axis=2).astype(BF16)


def _rel_bucket_table(n):
    exact = REL_BUCKETS // 2
    d = np.arange(n, dtype=np.int64)
    far = np.maximum(d, exact).astype(np.float64)
    large = exact + (np.log(far / exact) / math.log(REL_MAX_DIST / exact) * (REL_BUCKETS - exact)).astype(np.int64)
    return np.where(d < exact, d, np.minimum(large, REL_BUCKETS - 1)).astype(np.int32)


def _cparams(sem, vmem=VMEM_LIMIT):
    return pltpu.CompilerParams(dimension_semantics=sem, vmem_limit_bytes=vmem)


def _bdot(a, b):
    return jnp.dot(a, b, preferred_element_type=F32)


def _dot_nt(a, b):
    return lax.dot_general(a, b, (((1,), (1,)), ((), ())), preferred_element_type=F32)


def _dot_tn(a, b):
    return lax.dot_general(a, b, (((0,), (0,)), ((), ())), preferred_element_type=F32)


def _split3(x):
    h = x.astype(BF16)
    r = x - h.astype(F32)
    m = r.astype(BF16)
    l = (r - m.astype(F32)).astype(BF16)
    return h, m, l


def _seg64_sumsq(x):
    rows, width = x.shape
    low = lax.broadcasted_iota(jnp.int32, (rows, LANES), 1) < 64
    outs = []
    for c in range(width // LANES):
        sq = x[:, c * LANES:(c + 1) * LANES]
        sq = sq * sq
        s_lo = jnp.sum(jnp.where(low, sq, 0.0), axis=-1, keepdims=True)
        s_hi = jnp.sum(jnp.where(low, 0.0, sq), axis=-1, keepdims=True)
        outs.append(jnp.where(low, s_lo, s_hi))
    return outs[0] if len(outs) == 1 else jnp.concatenate(outs, axis=1)


def _ada_kernel(c_ref, w_ref, b_ref, o_ref):
    ca = jax.nn.silu(c_ref[...]).astype(BF16)
    o_ref[0] = _bdot(ca, w_ref[0].astype(BF16)) + b_ref[0]


def _ada_mod(c, ada_w, ada_b):
    L, D, N = ada_w.shape
    B = c.shape[0]
    tn = 1536
    out = pl.pallas_call(
        _ada_kernel,
        grid=(L, N // tn),
        in_specs=[pl.BlockSpec((B, D), lambda l, j: (0, 0)),
                  pl.BlockSpec((1, D, tn), lambda l, j: (l, 0, j)),
                  pl.BlockSpec((1, 1, tn), lambda l, j: (l, 0, j))],
        out_specs=pl.BlockSpec((1, B, tn), lambda l, j: (l, 0, j)),
        out_shape=jax.ShapeDtypeStruct((L, B, N), F32),
        compiler_params=_cparams(("arbitrary", "arbitrary")),
        name="ada_mod",
    )(c, ada_w, ada_b.reshape(L, 1, N))
    return out.reshape(L, B, 6, D)


def _inproj_kernel(x_ref, mod_ref, ng_ref, w_ref, qg_ref, kg_ref,
                   q_out, cmp_out, k_out, v_out, dn_out, z_out, mg_out, sm_out):
    x = x_ref[...]
    ms = jnp.mean(x * x, axis=-1, keepdims=True)
    h = x * lax.rsqrt(ms + NORM_EPS) * ng_ref[...]
    h = h * (1.0 + mod_ref[0, 1:2, :]) + mod_ref[0, 0:1, :]
    hb = h.astype(BF16)

    def seg(ab):
        return _bdot(hb, w_ref[:, ab[0]:ab[1]])

    q = seg(_SEG_Q)
    q = q * lax.rsqrt(_seg64_sumsq(q) * (1.0 / HEAD_DIM) + NORM_EPS) * qg_ref[...]
    q_out[...] = (q * (HEAD_DIM ** -0.5)).astype(BF16)
    cmp_out[...] = seg(_SEG_CMP)
    k = seg(_SEG_K)
    k = k * lax.rsqrt(_seg64_sumsq(k) * (1.0 / HEAD_DIM) + NORM_EPS) * kg_ref[...]
    k_out[...] = k.astype(BF16)
    v = seg(_SEG_V)
    lane = lax.broadcasted_iota(jnp.int32, v.shape, 1) % 256
    ones = jnp.where((lane >= 64) & (lane < 192), 1.0, 0.0)
    v_out[...] = (v + ones).astype(BF16)
    dn_out[...] = seg(_SEG_DN)
    z_out[...] = seg(_SEG_Z)
    mg_out[...] = seg(_SEG_MG)
    sm_out[...] = seg(_SEG_SM)


def _in_proj(x2, mod_l, norm_g, w_proj, q_gain, k_gain, seq):
    T, D = x2.shape
    tm = TM_PROJ
    tpb = seq // tm
    widths = [(512, BF16), (256, F32), (256, BF16), (512, BF16), (1536, F32), (512, F32), (2048, F32), (128, F32)]
    return pl.pallas_call(
        _inproj_kernel,
        grid=(T // tm,),
        in_specs=[pl.BlockSpec((tm, D), lambda i: (i, 0)),
                  pl.BlockSpec((1, 6, D), lambda i: (i // tpb, 0, 0)),
                  pl.BlockSpec((1, D), lambda i: (0, 0)),
                  pl.BlockSpec((D, N_PROJ), lambda i: (0, 0)),
                  pl.BlockSpec((1, 512), lambda i: (0, 0)),
                  pl.BlockSpec((1, 256), lambda i: (0, 0))],
        out_specs=[pl.BlockSpec((tm, w), lambda i: (i, 0)) for w, _ in widths],
        out_shape=[jax.ShapeDtypeStruct((T, w), dt) for w, dt in widths],
        compiler_params=_cparams(("arbitrary",)),
        name="in_proj",
    )(x2, mod_l, norm_g, w_proj, q_gain, k_gain)


def _compress_kernel(kraw_ref, vraw_ref, pos_ref, w1_ref, w2_ref, kg_ref, kc_out, vc_out, *, n_chunks):
    for j, raw_ref, out_ref in ((0, kraw_ref, kc_out), (1, vraw_ref, vc_out)):
        top = [jnp.zeros((n_chunks, CMP_HIDDEN), F32) for _ in range(NSA_GROUPS)]
        bot = [jnp.zeros((n_chunks, CMP_HIDDEN), F32) for _ in range(NSA_GROUPS)]
        for r in range(CMP_STRIDE):
            xr = raw_ref[0, pl.ds(r, n_chunks, stride=CMP_STRIDE), :]
            x_top = (xr + pos_ref[j, r:r + 1, :]).astype(BF16)
            x_bot = (xr + pos_ref[j, CMP_STRIDE + r:CMP_STRIDE + r + 1, :]).astype(BF16)
            for g in range(NSA_GROUPS):
                ls = slice(g * HEAD_DIM, (g + 1) * HEAD_DIM)
                top[g] = top[g] + _bdot(x_top[:, ls], w1_ref[j, r * HEAD_DIM:(r + 1) * HEAD_DIM, :])
                bot[g] = bot[g] + _bdot(x_bot[:, ls], w1_ref[j, (CMP_STRIDE + r) * HEAD_DIM:(CMP_STRIDE + r + 1) * HEAD_DIM, :])
        outs = []
        for g in range(NSA_GROUPS):
            hid = top[g] + pltpu.roll(bot[g], n_chunks - 1, 0)
            o = _bdot(jax.nn.gelu(hid).astype(BF16), w2_ref[j])
            if j == 0:
                ms = jnp.mean(o * o, axis=-1, keepdims=True)
                o = o * lax.rsqrt(ms + NORM_EPS) * kg_ref[...]
            outs.append(o)
        out_ref[0] = jnp.concatenate(outs, axis=1).astype(BF16)


def _compress(cmp_raw, pos128, w1, w2, k_gain):
    B, S, _ = cmp_raw.shape
    nc = S // CMP_STRIDE
    return pl.pallas_call(
        functools.partial(_compress_kernel, n_chunks=nc),
        grid=(B,),
        in_specs=[pl.BlockSpec((1, S, LANES), lambda b: (b, 0, 0)),
                  pl.BlockSpec((1, S, LANES), lambda b: (b, 0, 1)),
                  pl.BlockSpec((2, CMP_BLOCK, LANES), lambda b: (0, 0, 0)),
                  pl.BlockSpec((2, CMP_BLOCK * HEAD_DIM, CMP_HIDDEN), lambda b: (0, 0, 0)),
                  pl.BlockSpec((2, CMP_HIDDEN, HEAD_DIM), lambda b: (0, 0, 0)),
                  pl.BlockSpec((1, HEAD_DIM), lambda b: (0, 0))],
        out_specs=[pl.BlockSpec((1, nc, LANES), lambda b: (b, 0, 0))] * 2,
        out_shape=[jax.ShapeDtypeStruct((B, nc, LANES), BF16)] * 2,
        compiler_params=_cparams(("arbitrary",)),
        name="nsa_compress",
    )(cmp_raw, cmp_raw, pos128, w1, w2, k_gain)


def _nsa_tables(rel_bias, seq):
    nq = seq // TQ
    ncp = seq // CMP_STRIDE
    buckets = _rel_bucket_table(seq + WINDOW + TQ)
    rb = rel_bias.astype(F32)

    def lookup(dist, valid):
        ids = jnp.asarray(np.where(valid, buckets[np.clip(dist, 0, None)], -1).astype(np.int8))
        out = jnp.full((NSA_HEADS,) + dist.shape, NEG, F32)
        for b in range(REL_BUCKETS):
            out = jnp.where(ids[None] == b, rb[b].reshape((NSA_HEADS,) + (1,) * dist.ndim), out)
        return out

    def tiles(dist, valid):
        t = lookup(dist, valid)
        return t.transpose(1, 0, 2, 3).reshape(dist.shape[0], NSA_GROUPS, NSA_HPG * TQ, LANES)

    q = np.arange(TQ)[None, :, None]
    k = np.arange(LANES)[None, None, :]
    dlim = int(np.argmax(buckets == REL_BUCKETS - 1))
    dt = min(nq - 1, -(-(dlim + TQ - 1) // TQ))
    dist = TQ * np.arange(-1, dt + 1)[:, None, None] + q - k
    sel_tab = tiles(dist, (dist >= 0) & (np.arange(-1, dt + 1)[:, None, None] >= 0))
    dist = WINDOW - LANES * np.arange(WINDOW // LANES + 1)[:, None, None] + q - k
    win_tab = tiles(dist, (dist >= 0) & (dist < WINDOW))
    n = np.arange(ncp)[None, :]
    dist = np.arange(seq)[:, None] - (n * CMP_STRIDE + CMP_BLOCK - 1)
    cmp_tab = lookup(dist, (dist >= 0) & (n < ncp - 1))
    cmp_tab = cmp_tab.reshape(NSA_GROUPS, NSA_HPG, nq, TQ, ncp).transpose(2, 0, 1, 3, 4)
    cmp_tab = cmp_tab.reshape(nq, NSA_GROUPS, NSA_HPG * TQ, ncp)
    return sel_tab, win_tab, cmp_tab, dt


def _nsa_static(seq):
    nb = seq // SEL_BLOCK
    ncp = seq // CMP_STRIDE
    ratio, nsub = SEL_BLOCK // CMP_STRIDE, CMP_BLOCK // CMP_STRIDE
    delta = np.arange(ncp)[:, None] - ratio * np.arange(nb)[None, :]
    m_idx = delta[..., None] + np.arange(nsub)
    overlap = np.sum((m_idx >= 0) & (m_idx < ratio), axis=-1).astype(np.float32)
    overlap[ncp - 1, :] = 0.0
    expand = (np.arange(nb)[:, None] == (np.arange(seq)[None, :] // SEL_BLOCK)).astype(np.float32)
    return jnp.asarray(overlap.T, BF16), jnp.asarray(expand, BF16)


def _nsa_kernel(q_ref, sm_ref, kc_ref, vc_ref, k_ref, v_ref, bc_ref, ws_ref, ww_ref, ovt_ref, ex_ref,
                o_ref, m_s, acc_s, *, n_blocks, n_sel, dt):
    i = pl.program_id(1)
    t0 = i * TQ
    rows = NSA_HPG * TQ
    half = lax.broadcasted_iota(jnp.int32, (TQ, LANES), 1) // HEAD_DIM
    q = q_ref[0].astype(F32)
    gates = jax.nn.sigmoid(sm_ref[0])
    ng = NSA_GROUPS
    arows = ng * rows
    lane2 = lax.broadcasted_iota(jnp.int32, (n_blocks, ng * TQ), 1)
    jb = lax.broadcasted_iota(jnp.int32, (n_blocks, ng * TQ), 0)
    cur = (t0 + lane2 % TQ) // SEL_BLOCK
    valid = jb <= cur
    forced = valid & ((jb == 0) | (jb > cur - SEL_LOCAL))
    sub = TK_SEL // LANES
    y_heads = []

    def block_max(blocks):
        bm = blocks[0]
        for b in blocks[1:]:
            bm = jnp.maximum(bm, b)
        return jnp.max(bm, axis=-1, keepdims=True)

    def softmax_pv(blocks, v_tiles, m):
        p = jnp.concatenate([jnp.exp(b - m) for b in blocks], axis=1).astype(BF16)
        return jnp.concatenate([_bdot(p[g * rows:(g + 1) * rows], v_tiles[g]) for g in range(ng)], axis=0)

    def online_update(blocks, v_tiles, m_ref, acc_ref):
        m_old = m_ref[...]
        m_new = jnp.maximum(m_old, block_max(blocks))
        acc_ref[...] = jnp.exp(m_old - m_new) * acc_ref[...] + softmax_pv(blocks, v_tiles, m_new)
        m_ref[...] = m_new

    stack = []
    for g in range(ng):
        for h in range(NSA_HPG):
            hd = NSA_HPG * g + h
            blk = q[:, (hd // 2) * LANES:(hd // 2 + 1) * LANES]
            if hd % 2 != g:
                blk = pltpu.roll(blk, HEAD_DIM, 1)
            stack.append(jnp.where(half == g, blk, 0.0))
    qa = jnp.concatenate(stack, axis=0).astype(BF16)
    group_half = jnp.concatenate([half == g for g in range(ng) for _ in range(NSA_HPG)], axis=0)

    bc = bc_ref[0].reshape(arows, bc_ref.shape[-1])
    sc = _dot_nt(qa, kc_ref[0]) + bc
    ec = jnp.exp(sc - jnp.max(sc, axis=-1, keepdims=True))
    ec = jnp.where(bc > 0.5 * NEG, ec, 0.0)
    lc = jnp.sum(ec, axis=-1, keepdims=True)
    pc = ec / jnp.where(lc > 0.0, lc, 1.0)
    o_c = _bdot(pc.astype(BF16), vc_ref[0])

    psum = jnp.concatenate(
        [sum(pc[g * rows + h * TQ:g * rows + (h + 1) * TQ] for h in range(NSA_HPG)) for g in range(ng)], axis=0)
    p_hi = psum.astype(BF16)
    p_lo = (psum - p_hi.astype(F32)).astype(BF16)
    imp = _dot_nt(ovt_ref[...], p_hi) + _dot_nt(ovt_ref[...], p_lo)
    score = jnp.where(forced, FORCE_SCORE, jnp.where(valid, imp, -1.0))

    def pick(_, carry):
        sc_, sel_ = carry
        top = jnp.max(sc_, axis=0, keepdims=True)
        first = jnp.min(jnp.where(sc_ == top, jb, n_blocks), axis=0, keepdims=True)
        hit = jb == first
        return jnp.where(hit, -3e38, sc_), jnp.where(hit, 1.0, sel_)

    _, sel = lax.fori_loop(0, n_sel, pick, (score, jnp.zeros((n_blocks, ng * TQ), F32)))
    sel_q = jnp.transpose(sel).astype(BF16)

    m_s[...] = jnp.full((arows, LANES), -3e38, F32)
    acc_s[...] = jnp.zeros((arows, LANES), F32)

    def sel_scores(jt):
        ks = pl.multiple_of(jt * TK_SEL, TK_SEL)
        s = _dot_nt(qa, k_ref[0, pl.ds(ks, TK_SEL), 0:LANES])
        selx = _bdot(sel_q, ex_ref[:, pl.ds(ks, TK_SEL)])
        madd = (selx - 1.0) * (-NEG)
        parts = []
        for c in range(sub):
            d = i - (jt * sub + c)
            b = ws_ref[jnp.clip(d, -1, dt) + 1].reshape(arows, LANES)
            sc_ = (s[:, c * LANES:(c + 1) * LANES] + b).reshape(ng, NSA_HPG, TQ, LANES)
            sc_ = sc_ + madd[:, c * LANES:(c + 1) * LANES].reshape(ng, 1, TQ, LANES)
            parts.append(sc_.reshape(arows, LANES))
        return parts

    def sel_update(jt, parts):
        ks = pl.multiple_of(jt * TK_SEL, TK_SEL)
        online_update(parts, [v_ref[0, pl.ds(ks, TK_SEL), g * LANES:(g + 1) * LANES] for g in range(ng)],
                      m_s, acc_s)

    def sel_pair(jp, carry):
        first, second = sel_scores(2 * jp), sel_scores(2 * jp + 1)
        sel_update(2 * jp, first)
        sel_update(2 * jp + 1, second)
        return carry

    n_tiles = (t0 + TQ + TK_SEL - 1) // TK_SEL
    lax.fori_loop(0, n_tiles // 2, sel_pair, 0)

    @pl.when(n_tiles % 2 == 1)
    def _():
        sel_update(n_tiles - 1, sel_scores(n_tiles - 1))

    nwt = WINDOW // LANES + 1
    starts = [t0 - WINDOW + c * LANES for c in range(nwt)]
    clamped = [pl.multiple_of(jnp.maximum(ks, 0), LANES) for ks in starts]
    kw = jnp.concatenate([k_ref[0, pl.ds(ks, LANES), LANES:2 * LANES] for ks in clamped], axis=0)
    vws = [jnp.concatenate([v_ref[0, pl.ds(ks, LANES), (2 + g) * LANES:(3 + g) * LANES] for ks in clamped], axis=0)
           for g in range(ng)]
    sw = _dot_nt(qa, kw)
    blocks = [sw[:, c * LANES:(c + 1) * LANES]
              + jnp.where(starts[c] >= 0, ww_ref[c].reshape(arows, LANES), NEG) for c in range(nwt)]

    def finish(acc):
        out = acc / pltpu.roll(acc, HEAD_DIM, 1)
        return jnp.where(group_half, out, 0.0)

    o_s = finish(acc_s[...])
    o_w = finish(softmax_pv(blocks, vws, block_max(blocks)))
    for g in range(ng):
        for h in range(NSA_HPG):
            hd = NSA_HPG * g + h
            rs = slice(g * rows + h * TQ, g * rows + (h + 1) * TQ)
            y = (gates[:, 3 * hd:3 * hd + 1] * o_c[rs] + gates[:, 3 * hd + 1:3 * hd + 2] * o_s[rs]
                 + gates[:, 3 * hd + 2:3 * hd + 3] * o_w[rs])
            y = jnp.where(half == g, y, 0.0)
            if hd % 2 != g:
                y = pltpu.roll(y, HEAD_DIM, 1)
            y_heads.append(y)

    o_ref[0] = jnp.concatenate([y_heads[2 * c] + y_heads[2 * c + 1] for c in range(NSA_HEADS // 2)],
                               axis=1).astype(BF16)


def _nsa_attention(q, sm, kc, vc, k_sw, v_sw, tabs, statics):
    B, S, _ = q.shape
    sel_tab, win_tab, cmp_tab, dt = tabs
    ovt, expand = statics
    nq = S // TQ
    ncp = S // CMP_STRIDE
    nb = S // SEL_BLOCK
    rows = NSA_HPG * TQ
    const = lambda nd: (lambda b, i: (0,) * nd)
    return pl.pallas_call(
        functools.partial(_nsa_kernel, n_blocks=nb, n_sel=min(SEL_TOPK, nb), dt=dt),
        grid=(B, nq),
        in_specs=[pl.BlockSpec((1, TQ, NSA_Q), lambda b, i: (b, i, 0)),
                  pl.BlockSpec((1, TQ, LANES), lambda b, i: (b, i, 0)),
                  pl.BlockSpec((1, ncp, LANES), lambda b, i: (b, 0, 0)),
                  pl.BlockSpec((1, ncp, LANES), lambda b, i: (b, 0, 0)),
                  pl.BlockSpec((1, S, 256), lambda b, i: (b, 0, 0)),
                  pl.BlockSpec((1, S, 512), lambda b, i: (b, 0, 0)),
                  pl.BlockSpec((1, NSA_GROUPS, rows, ncp), lambda b, i: (i, 0, 0, 0)),
                  pl.BlockSpec(sel_tab.shape, const(4)),
                  pl.BlockSpec(win_tab.shape, const(4)),
                  pl.BlockSpec(ovt.shape, const(2)),
                  pl.BlockSpec(expand.shape, const(2))],
        out_specs=pl.BlockSpec((1, TQ, NSA_Q), lambda b, i: (b, i, 0)),
        out_shape=jax.ShapeDtypeStruct((B, S, NSA_Q), BF16),
        scratch_shapes=[pltpu.VMEM((NSA_GROUPS * rows, LANES), F32), pltpu.VMEM((NSA_GROUPS * rows, LANES), F32)],
        compiler_params=_cparams(("arbitrary", "arbitrary")),
        name="nsa_attention",
    )(q, sm, kc, vc, k_sw, v_sw, cmp_tab, sel_tab, win_tab, ovt, expand)


DN_QUAD = 4
DN_QW = DN_QUAD * HEAD_DIM


def _block_diag(a, head_masks):
    zero = jnp.zeros_like(a)
    return jnp.concatenate([jnp.where(m, a, zero) for m in head_masks], axis=0)


def _dn_kernel(qkv_ref, sm_ref, z_ref, cw_ref, alog_ref, dtb_ref, ng_ref, eb_ref, ea_ref, o_ref,
               xbuf, state, q_s, k_s, v_s, b_s, gc_s, out_s):
    j = pl.program_id(1)
    ct = CT_DN
    c = DN_CHUNK

    @pl.when(j == 0)
    def _():
        xbuf[0:8, :] = jnp.zeros((8, 3 * DN_W), F32)
        state[...] = jnp.zeros(state.shape, F32)

    xbuf[8:8 + ct, :] = qkv_ref[0]
    acc = cw_ref[0:1, :] * xbuf[5:5 + ct, :]
    for tap in range(1, DN_CONV):
        acc = acc + cw_ref[tap:tap + 1, :] * xbuf[5 + tap:5 + tap + ct, :]
    xbuf[0:8, :] = xbuf[ct:ct + 8, :]
    y = jax.nn.silu(acc)
    qh = y[:, 0:DN_W]
    kh = y[:, DN_W:2 * DN_W]
    q_s[...] = qh * lax.rsqrt(_seg64_sumsq(qh) + NORM_EPS) * (HEAD_DIM ** -0.5)
    k_s[...] = kh * lax.rsqrt(_seg64_sumsq(kh) + NORM_EPS)
    v_s[...] = y[:, 2 * DN_W:3 * DN_W]
    sm = sm_ref[0]
    b_s[...] = sum(_bdot(p, eb_ref[...]) for p in _split3(jax.nn.sigmoid(sm)))
    gdec = -jnp.exp(alog_ref[...]) * jax.nn.softplus(sm + dtb_ref[...])
    g_wide = [_bdot(p, ea_ref[...]).astype(BF16) for p in _split3(gdec)]
    row = lax.broadcasted_iota(jnp.int32, (c, DN_QW), 0)
    col = lax.broadcasted_iota(jnp.int32, (c, DN_QW), 1) % c
    causal = row >= col
    strict = row > col
    diag = row == col
    eye = diag.astype(F32)
    masks = [lax.broadcasted_iota(jnp.int32, (c, DN_QW), 1) // HEAD_DIM == h for h in range(DN_QUAD)]
    tril = (lax.broadcasted_iota(jnp.int32, (c, c), 0) >= lax.broadcasted_iota(jnp.int32, (c, c), 1)).astype(BF16)
    for ch in range(ct // c):
        rs = slice(ch * c, (ch + 1) * c)
        gc_s[rs, :] = sum(_bdot(tril, gw[rs]) for gw in g_wide)

    n_quads = DN_HEADS // DN_QUAD
    n_chunks = ct // c
    items = [(ch, qd) for ch in range(n_chunks) for qd in range(n_quads)]
    sl = lambda it: (slice(it[0] * c, (it[0] + 1) * c), slice(it[1] * DN_QW, (it[1] + 1) * DN_QW))
    every = lambda f, *lists: [f(*args) for args in zip(*lists)]
    bd = lambda a: _block_diag(a.astype(BF16), masks)
    q4 = [q_s[sl(it)] for it in items]
    k4 = [k_s[sl(it)] for it in items]
    b4 = [b_s[sl(it)] for it in items]
    gc4 = [gc_s[sl(it)] for it in items]
    vb = [v_s[sl(it)] * b for it, b in zip(items, b4)]
    g_key = every(lambda g: jnp.sum(jnp.where(diag, g, 0.0), axis=0, keepdims=True), gc4)
    decay = every(lambda g, gk: jnp.where(causal, jnp.exp(jnp.where(causal, g - gk, 0.0)), 0.0), gc4, g_key)
    eg = every(jnp.exp, gc4)
    kb = every(lambda k, b: k * b, k4, b4)
    k_bd = every(bd, k4)
    low = every(lambda a, kd, dc: jnp.where(strict, _dot_nt(a.astype(BF16), kd) * dc, 0.0), kb, k_bd, decay)
    attn = every(lambda a, kd, dc: jnp.where(causal, _dot_nt(a.astype(BF16), kd) * dc, 0.0), q4, k_bd, decay)
    x = every(lambda lo: eye - lo, low)
    p = every(lambda lo: _bdot(lo.astype(BF16), bd(lo)), low)
    steps = int(math.log2(c)) - 1
    for s in range(steps):
        pd = every(bd, p)
        x = every(lambda xi, pdi: xi + _bdot(xi.astype(BF16), pdi), x, pd)
        if s + 1 < steps:
            p = every(lambda pi, pdi: _bdot(pi.astype(BF16), pdi), p, pd)
    t_inv = every(lambda xi: xi.astype(BF16), x)
    u = every(lambda t, a: _bdot(t, bd(a)), t_inv, vb)
    w = every(lambda t, a, e: _bdot(t, bd(a * e)), t_inv, kb, eg)
    qe = every(lambda a, e: (a * e).astype(BF16), q4, eg)
    g_last = every(lambda g: g[c - 1:c, :], gc4)
    k_dec = every(lambda k, gl, g: (k * jnp.exp(gl - g)).astype(BF16), k4, g_last, gc4)

    st = [state[qd] for qd in range(n_quads)]
    for ch in range(n_chunks):
        idx = [ch * n_quads + qd for qd in range(n_quads)]
        s_bd = every(bd, st)
        v_new = [(u[i] - _bdot(w[i].astype(BF16), sb)).astype(BF16) for i, sb in zip(idx, s_bd)]
        outs = [_bdot(qe[i], sb) + _bdot(attn[i].astype(BF16), _block_diag(vn, masks))
                for i, sb, vn in zip(idx, s_bd, v_new)]
        cross = [_dot_tn(k_dec[i], vn) for i, vn in zip(idx, v_new)]
        for qd, i in enumerate(idx):
            out_s[sl(items[i])] = outs[qd]
            upd = sum(jnp.where(masks[h], cross[qd][h * HEAD_DIM:(h + 1) * HEAD_DIM, :], 0.0) for h in range(DN_QUAD))
            st[qd] = st[qd] * jnp.exp(g_last[i]) + upd
    for qd in range(n_quads):
        state[qd] = st[qd]

    o = out_s[...]
    o = o * lax.rsqrt(_seg64_sumsq(o) * (1.0 / HEAD_DIM) + NORM_EPS) * ng_ref[...]
    o_ref[0] = (o * jax.nn.silu(z_ref[0])).astype(BF16)


def _deltanet(dn_qkv, sm, z, conv_w, alog128, dtb128, ng512):
    B, S, _ = dn_qkv.shape
    ct = CT_DN
    head_of_lane = np.arange(DN_W) // HEAD_DIM
    spread = lambda off: jnp.asarray(np.arange(LANES)[:, None] == off + head_of_lane[None, :], BF16)
    return pl.pallas_call(
        _dn_kernel,
        grid=(B, S // ct),
        in_specs=[pl.BlockSpec((1, ct, 3 * DN_W), lambda b, j: (b, j, 0)),
                  pl.BlockSpec((1, ct, LANES), lambda b, j: (b, j, 0)),
                  pl.BlockSpec((1, ct, DN_W), lambda b, j: (b, j, 0)),
                  pl.BlockSpec((DN_CONV, 3 * DN_W), lambda b, j: (0, 0)),
                  pl.BlockSpec((1, LANES), lambda b, j: (0, 0)),
                  pl.BlockSpec((1, LANES), lambda b, j: (0, 0)),
                  pl.BlockSpec((1, DN_W), lambda b, j: (0, 0)),
                  pl.BlockSpec((LANES, DN_W), lambda b, j: (0, 0)),
                  pl.BlockSpec((LANES, DN_W), lambda b, j: (0, 0))],
        out_specs=pl.BlockSpec((1, ct, DN_W), lambda b, j: (b, j, 0)),
        out_shape=jax.ShapeDtypeStruct((B, S, DN_W), BF16),
        scratch_shapes=[pltpu.VMEM((ct + 8, 3 * DN_W), F32),
                        pltpu.VMEM((DN_HEADS // DN_QUAD, HEAD_DIM, DN_QW), F32),
                        pltpu.VMEM((ct, DN_W), F32), pltpu.VMEM((ct, DN_W), F32), pltpu.VMEM((ct, DN_W), F32),
                        pltpu.VMEM((ct, DN_W), F32), pltpu.VMEM((ct, DN_W), F32),
                        pltpu.VMEM((ct, DN_W), F32)],
        compiler_params=_cparams(("arbitrary", "arbitrary")),
        name="gated_deltanet",
    )(dn_qkv, sm, z, conv_w, alog128, dtb128, ng512, spread(SM_BETA), spread(SM_A))


def _merge_kernel(ya_ref, yb_ref, mg_ref, x_ref, mod_ref, wa_ref, wb_ref, wo_ref, ng_ref, rw_ref,
                  xo_ref, hp_ref, lg_ref):
    d = D_MODEL
    m = jax.nn.sigmoid(mg_ref[...])
    y = m[:, :d] * _bdot(ya_ref[...], wa_ref[...]) + m[:, d:] * _bdot(yb_ref[...], wb_ref[...])
    xn = x_ref[...] + mod_ref[0, 2:3, :] * _bdot(y.astype(BF16), wo_ref[...])
    xo_ref[...] = xn
    ms = jnp.mean(xn * xn, axis=-1, keepdims=True)
    h = xn * lax.rsqrt(ms + NORM_EPS) * ng_ref[...]
    h = h * (1.0 + mod_ref[0, 4:5, :]) + mod_ref[0, 3:4, :]
    lg_ref[...] = lax.dot_general(rw_ref[...], h, (((1,), (1,)), ((), ())), preferred_element_type=F32,
                                  precision=lax.Precision.HIGHEST)
    bits = pltpu.bitcast(h.astype(BF16).astype(F32), jnp.uint32)
    hp_ref[...] = (bits[:, :d // 2] & jnp.uint32(0xFFFF0000)) | (bits[:, d // 2:] >> 16)


def _merge(ya, yb, mg, x2, mod_l, wa, wb, wo, norm_g, router_wt, seq):
    T, D = x2.shape
    tm = TM_MERGE
    tpb = seq // tm
    row = lambda w: pl.BlockSpec((tm, w), lambda i: (i, 0))
    full = lambda a: pl.BlockSpec(a.shape, lambda i: (0,) * a.ndim)
    return pl.pallas_call(
        _merge_kernel,
        grid=(T // tm,),
        in_specs=[row(NSA_Q), row(DN_W), row(2 * D), row(D),
                  pl.BlockSpec((1, 6, D), lambda i: (i // tpb, 0, 0)),
                  full(wa), full(wb), full(wo), full(norm_g), full(router_wt)],
        out_specs=[row(D), row(D // 2), pl.BlockSpec((N_EXPERTS, tm), lambda i: (0, i))],
        out_shape=[jax.ShapeDtypeStruct((T, D), F32), jax.ShapeDtypeStruct((T, D // 2), jnp.uint32),
                   jax.ShapeDtypeStruct((N_EXPERTS, T), F32)],
        compiler_params=_cparams(("arbitrary",)),
        name="merge_out",
    )(ya, yb, mg, x2, mod_l, wa, wb, wo, norm_g, router_wt)


def _moe_rows(tm):
    return -(-(TOPK * tm + N_EXPERTS * (ROW_ALIGN - 1) + BLK_MOE) // ROW_ALIGN) * ROW_ALIGN


def _first_max(vals):
    best = vals[0]
    for v in vals[1:]:
        best = jnp.maximum(best, v)
    idx = jnp.full(best.shape, len(vals) - 1, jnp.int32)
    for j in range(len(vals) - 2, -1, -1):
        idx = jnp.where(vals[j] == best, j, idx)
    return best, idx


def _pick(vals, idx):
    out = vals[-1]
    for j in range(len(vals) - 2, -1, -1):
        out = jnp.where(idx == j, vals[j], out)
    return out


def _route_kernel(lg_ref, rb_ref, tri_ref, low_ref, slot_ref, wt_ref, seg_ref, *, tm):
    epg = EXPERTS_PER_GROUP
    scores = jax.nn.sigmoid(lg_ref[...])
    sel = scores + rb_ref[...]
    s_rows = [sel[e:e + 1, :] for e in range(N_EXPERTS)]
    p_rows = [scores[e:e + 1, :] for e in range(N_EXPERTS)]
    grp = []
    for g in range(N_GROUPS):
        a, b, c, d = s_rows[epg * g:epg * g + epg]
        hi1, lo1, hi2, lo2 = jnp.maximum(a, b), jnp.minimum(a, b), jnp.maximum(c, d), jnp.minimum(c, d)
        second = jnp.maximum(jnp.minimum(hi1, hi2), jnp.where(hi1 >= hi2, lo1, lo2))
        grp.append(jnp.maximum(hi1, hi2) + second)
    _, gidx = _first_max(grp)
    cs = [_pick([s_rows[epg * g + j] for g in range(N_GROUPS)], gidx) for j in range(epg)]
    cp = [_pick([p_rows[epg * g + j] for g in range(N_GROUPS)], gidx) for j in range(epg)]
    _, j1 = _first_max(cs)
    _, j2 = _first_max([jnp.where(j1 == j, -jnp.inf, cs[j]) for j in range(epg)])
    w1 = _pick(cp, j1)
    w2 = _pick(cp, j2)
    den = w1 + w2
    wt_ref[0] = jnp.concatenate([w1 / den, w2 / den], axis=0)
    e1 = epg * gidx + j1
    e2 = epg * gidx + j2

    erow = lax.broadcasted_iota(jnp.int32, (N_EXPERTS, tm), 0)
    oh0 = (erow == e1).astype(F32)
    oh1 = (erow == e2).astype(F32)
    cum = _bdot(jnp.concatenate([oh0, oh1], axis=0).astype(BF16), tri_ref[...])
    cum0, cum1 = cum[:N_EXPERTS], cum[N_EXPERTS:]
    cnt0 = cum0[:, tm - 1:tm]
    counts = cnt0 + cum1[:, tm - 1:tm]
    padded = jnp.floor((counts + (ROW_ALIGN - 1)) * (1.0 / ROW_ALIGN)) * ROW_ALIGN
    start = jnp.dot(low_ref[...], jnp.broadcast_to(padded, (N_EXPERTS, LANES)),
                    preferred_element_type=F32, precision=lax.Precision.HIGHEST)
    st = start[:, 0:1]
    slot0 = jnp.sum(oh0 * (st + cum0 - 1.0), axis=0, keepdims=True)
    slot1 = jnp.sum(oh1 * (st + cnt0 + cum1 - 1.0), axis=0, keepdims=True)
    slot_ref[0] = jnp.concatenate([slot0, slot1], axis=0).astype(jnp.int32)
    nch = jnp.floor((counts + (BLK_MOE - 1)) * (1.0 / BLK_MOE))
    seg_ref[0] = jnp.concatenate([start, jnp.broadcast_to(nch, (N_EXPERTS, LANES))], axis=0).astype(jnp.int32)


def _moe_route(logits_t, router_b, tm):
    E, T = logits_t.shape
    nt = T // tm
    tri = (jnp.arange(tm)[:, None] <= jnp.arange(tm)[None, :]).astype(BF16)
    low = (jnp.arange(E)[None, :] < jnp.arange(E)[:, None]).astype(F32)
    slot, wt, seg = pl.pallas_call(
        functools.partial(_route_kernel, tm=tm),
        grid=(nt,),
        in_specs=[pl.BlockSpec((E, tm), lambda i: (0, i)),
                  pl.BlockSpec((E, 1), lambda i: (0, 0)),
                  pl.BlockSpec((tm, tm), lambda i: (0, 0)),
                  pl.BlockSpec((E, E), lambda i: (0, 0))],
        out_specs=[pl.BlockSpec((1, TOPK, tm), lambda i: (i, 0, 0)),
                   pl.BlockSpec((1, TOPK, tm), lambda i: (i, 0, 0)),
                   pl.BlockSpec((1, 2 * E, LANES), lambda i: (i, 0, 0))],
        out_shape=[jax.ShapeDtypeStruct((nt, TOPK, tm), jnp.int32),
                   jax.ShapeDtypeStruct((nt, TOPK, tm), F32),
                   jax.ShapeDtypeStruct((nt, 2 * E, LANES), jnp.int32)],
        compiler_params=_cparams(("arbitrary",)),
        name="moe_route",
    )(logits_t, router_b.astype(F32)[:, None], tri, low)
    return slot, wt, seg[:, :, 0][:, None, :]


def _moe_kernel(slot_ref, wt_ref, seg_ref, hp_ref, x_ref, mod_ref, wg_ref, wu_ref, wd_ref,
                o_ref, gbuf, ybuf, *, tm):
    e = pl.program_id(1)
    half = D_MODEL // 2

    @pl.when(e == 0)
    def _():
        gbuf[...] = jnp.zeros(gbuf.shape, gbuf.dtype)

        def scatter(t, carry):
            row = hp_ref[pl.ds(t, 1), :]
            gbuf[pl.ds(slot_ref[0, 0, t], 1), :] = row
            gbuf[pl.ds(slot_ref[0, 1, t], 1), :] = row
            return carry
        lax.fori_loop(0, tm, scatter, 0, unroll=8)

    start = seg_ref[0, 0, e]

    def chunk(ci, carry):
        r0 = pl.multiple_of(start + ci * BLK_MOE, ROW_ALIGN)
        w = gbuf[pl.ds(r0, BLK_MOE), :]
        x_hi = pltpu.bitcast(w & jnp.uint32(0xFFFF0000), F32).astype(BF16)
        x_lo = pltpu.bitcast(w << 16, F32).astype(BF16)
        gt = _bdot(x_hi, wg_ref[0, 0, 0:half, :]) + _bdot(x_lo, wg_ref[0, 0, half:, :])
        up = _bdot(x_hi, wu_ref[0, 0, 0:half, :]) + _bdot(x_lo, wu_ref[0, 0, half:, :])
        act = (jax.nn.silu(gt) * up).astype(BF16)
        ybuf[pl.ds(r0, BLK_MOE), :] = _bdot(act, wd_ref[0, 0])
        return carry

    lax.fori_loop(0, seg_ref[0, 0, N_EXPERTS + e], chunk, 0)

    @pl.when(e == N_EXPERTS - 1)
    def _():
        def combine(t, carry):
            y0 = ybuf[pl.ds(slot_ref[0, 0, t], 1), :] * wt_ref[0, 0, t]
            y1 = ybuf[pl.ds(slot_ref[0, 1, t], 1), :] * wt_ref[0, 1, t]
            o_ref[pl.ds(t, 1), :] = y0 + y1
            return carry
        lax.fori_loop(0, tm, combine, 0, unroll=8)
        o_ref[...] = x_ref[...] + mod_ref[0, 5:6, :] * o_ref[...]


def _moe(hp, x2, mod_l, route, wg, wu, wd, layer, seq):
    T, D = x2.shape
    tm = TM_MOE
    nt = T // tm
    tpb = seq // tm
    n_rows = _moe_rows(tm)
    slot, wt, seg = route
    smem = lambda r, w: pl.BlockSpec((1, r, w), lambda i, e: (i, 0, 0), memory_space=pltpu.SMEM)
    return pl.pallas_call(
        functools.partial(_moe_kernel, tm=tm),
        grid=(nt, N_EXPERTS),
        in_specs=[smem(TOPK, tm), smem(TOPK, tm), smem(1, 2 * N_EXPERTS),
                  pl.BlockSpec((tm, D // 2), lambda i, e: (i, 0)),
                  pl.BlockSpec((tm, D), lambda i, e: (i, 0)),
                  pl.BlockSpec((1, 6, D), lambda i, e: (i // tpb, 0, 0)),
                  pl.BlockSpec((1, 1, D, D_EXPERT), lambda i, e: (layer, e, 0, 0)),
                  pl.BlockSpec((1, 1, D, D_EXPERT), lambda i, e: (layer, e, 0, 0)),
                  pl.BlockSpec((1, 1, D_EXPERT, D), lambda i, e: (layer, e, 0, 0))],
        out_specs=pl.BlockSpec((tm, D), lambda i, e: (i, 0)),
        out_shape=jax.ShapeDtypeStruct((T, D), F32),
        scratch_shapes=[pltpu.VMEM((n_rows, D // 2), jnp.uint32), pltpu.VMEM((n_rows, D), F32)],
        compiler_params=_cparams(("arbitrary", "arbitrary")),
        name="moe_ffn",
    )(slot, wt, seg, hp, x2, mod_l, wg, wu, wd)


def _pad_lanes(v, offset):
    return jnp.zeros((1, LANES), F32).at[0, offset:offset + v.shape[0]].set(v.astype(F32))


def kernel(x, c, rel_bias, router_w, router_b, ada_w, ada_b, norm1_g, norm2_g, w_in, qk_norm_g,
           cmp_pos, cmp_w1, cmp_w2, dn_conv_w, dn_a_log, dn_dt_bias, dn_norm_g, w_branch_a,
           w_branch_b, w_out, moe_w_gate, moe_w_up, moe_w_down):
    B, S, D = x.shape
    T = B * S
    L = ada_w.shape[0]

    mod = _ada_mod(c, ada_w, ada_b)
    w_proj = _proj_weight(w_in)
    tabs = _nsa_tables(rel_bias, S)
    statics = _nsa_static(S)
    router_wt = router_w.astype(F32).T
    wg, wu, wd = moe_w_gate.astype(BF16), moe_w_up.astype(BF16), moe_w_down.astype(BF16)

    x2 = x.reshape(T, D)
    for l in range(L):
        qkg = qk_norm_g[l].astype(F32)
        q_gain = jnp.tile(qkg[0], NSA_HEADS)[None, :]
        k_gain = jnp.concatenate([jnp.tile(qkg[2], NSA_GROUPS), jnp.tile(qkg[3], NSA_GROUPS)])[None, :]
        q, cmp_raw, k_sw, v_sw, dn_qkv, z, mg, sm = _in_proj(
            x2, mod[l], norm1_g[l][None, :], w_proj[l], q_gain, k_gain, S)

        pos128 = jnp.tile(cmp_pos[l].astype(F32), (1, 1, NSA_GROUPS))
        kc, vc = _compress(cmp_raw.reshape(B, S, 256), pos128, cmp_w1[l].astype(BF16),
                           cmp_w2[l].astype(BF16), qkg[1][None, :])
        y_a = _nsa_attention(q.reshape(B, S, NSA_Q), sm.reshape(B, S, LANES), kc, vc,
                             k_sw.reshape(B, S, 256), v_sw.reshape(B, S, 512), tabs, statics)

        y_b = _deltanet(dn_qkv.reshape(B, S, 3 * DN_W), sm.reshape(B, S, LANES), z.reshape(B, S, DN_W),
                        dn_conv_w[l].astype(F32), _pad_lanes(dn_a_log[l], SM_A),
                        _pad_lanes(dn_dt_bias[l], SM_A), jnp.tile(dn_norm_g[l].astype(F32), DN_HEADS)[None, :])

        x_mid, hp, logits = _merge(y_a.reshape(T, NSA_Q), y_b.reshape(T, DN_W), mg, x2, mod[l],
                                   w_branch_a[l].astype(BF16), w_branch_b[l].astype(BF16),
                                   w_out[l].astype(BF16), norm2_g[l][None, :], router_wt, S)
        route = _moe_route(logits, router_b, TM_MOE)
        x2 = _moe(hp, x_mid, mod[l], route, wg, wu, wd, l, S)
    return x2.reshape(B, S, D)
```

```python
import functools
import math

import numpy as np
import jax
import jax.numpy as jnp
from jax import lax
from jax.experimental import pallas as pl
from jax.experimental.pallas import tpu as pltpu

F32 = jnp.float32
BF16 = jnp.bfloat16

D_MODEL = 1024
DEPTH = 4
HEAD_DIM = 64
NSA_HEADS = 8
NSA_GROUPS = 2
NSA_HPG = NSA_HEADS // NSA_GROUPS
CMP_STRIDE = 16
CMP_BLOCK = 32
CMP_HIDDEN = 256
SEL_BLOCK = 64
SEL_TOPK = 16
SEL_LOCAL = 2
WINDOW = 512
DN_HEADS = 8
DN_CONV = 4
DN_CHUNK = 64
REL_BUCKETS = 32
REL_MAX_DIST = 1024
N_EXPERTS = 16
N_GROUPS = 4
EXPERTS_PER_GROUP = N_EXPERTS // N_GROUPS
TOPK = 2
D_EXPERT = 512
NORM_EPS = 1e-6
FORCE_SCORE = 1e9
NEG = -1e30

NSA_Q = NSA_HEADS * HEAD_DIM
NSA_KV = NSA_GROUPS * HEAD_DIM
DN_W = DN_HEADS * HEAD_DIM
IN_SIZES = (NSA_Q, 6 * NSA_KV, 3 * NSA_HEADS, 3 * DN_W, DN_HEADS, DN_HEADS, DN_W, 2 * D_MODEL)
IN_OFFS = tuple(int(v) for v in np.cumsum((0,) + IN_SIZES))

LANES = 128
VMEM_LIMIT = 56 * 1024 * 1024

TM_PROJ = 512
TM_MERGE = 512
TQ = 128
TK_SEL = 512
SEL_TILES_PER_ITER = 2
CT_DN = 256
TM_MOE = 1024
BLK_MOE = 128
MOE_EXPERTS_PER_STEP = 2
ROW_ALIGN = 8

_SEG_Q = (0, 512)
_SEG_CMP = (512, 768)
_SEG_K = (768, 1024)
_SEG_V = (1024, 1536)
_SEG_DN = (1536, 3072)
_SEG_Z = (3072, 3584)
_SEG_MG = (3584, 5632)
_SEG_SM = (5632, 5760)
N_PROJ = 5760
SM_GATE, SM_BETA, SM_A = 0, 24, 32


def _proj_weight(w_in):
    o = IN_OFFS
    kv = o[1]
    zeros = lambda n: jnp.zeros(w_in.shape[:2] + (n,), w_in.dtype)
    cut = lambda a, b: w_in[:, :, a:b]
    parts = [cut(o[0], o[1]),
             cut(kv, kv + 256),
             cut(kv + 256, kv + 384), cut(kv + 512, kv + 640)]
    for base in (kv + 384, kv + 640):
        parts += [cut(base, base + 64), zeros(128), cut(base + 64, base + 128)]
    parts += [cut(o[3], o[4]), cut(o[6], o[7]), cut(o[7], o[8]),
              cut(o[2], o[3]), cut(o[4], o[5]), cut(o[5], o[6])]
    width = sum(p.shape[2] for p in parts)
    parts.append(zeros(N_PROJ - width))
    return jnp.concatenate(parts, axis=2).astype(BF16)


def _rel_bucket_table(n):
    exact = REL_BUCKETS // 2
    d = np.arange(n, dtype=np.int64)
    far = np.maximum(d, exact).astype(np.float64)
    large = exact + (np.log(far / exact) / math.log(REL_MAX_DIST / exact) * (REL_BUCKETS - exact)).astype(np.int64)
    return np.where(d < exact, d, np.minimum(large, REL_BUCKETS - 1)).astype(np.int32)


def _cparams(sem, vmem=VMEM_LIMIT):
    return pltpu.CompilerParams(dimension_semantics=sem, vmem_limit_bytes=vmem)


def _bdot(a, b):
    return jnp.dot(a, b, preferred_element_type=F32)


def _dot_nt(a, b):
    return lax.dot_general(a, b, (((1,), (1,)), ((), ())), preferred_element_type=F32)


def _dot_tn(a, b):
    return lax.dot_general(a, b, (((0,), (0,)), ((), ())), preferred_element_type=F32)


def _split3(x):
    h = x.astype(BF16)
    r = x - h.astype(F32)
    m = r.astype(BF16)
    l = (r - m.astype(F32)).astype(BF16)
    return h, m, l


def _seg64_sumsq(x):
    rows, width = x.shape
    low = lax.broadcasted_iota(jnp.int32, (rows, LANES), 1) < 64
    outs = []
    for c in range(width // LANES):
        sq = x[:, c * LANES:(c + 1) * LANES]
        sq = sq * sq
        s_lo = jnp.sum(jnp.where(low, sq, 0.0), axis=-1, keepdims=True)
        s_hi = jnp.sum(jnp.where(low, 0.0, sq), axis=-1, keepdims=True)
        outs.append(jnp.where(low, s_lo, s_hi))
    return outs[0] if len(outs) == 1 else jnp.concatenate(outs, axis=1)


def _ada_kernel(c_ref, w_ref, b_ref, o_ref):
    ca = jax.nn.silu(c_ref[...]).astype(BF16)
    o_ref[0] = _bdot(ca, w_ref[0].astype(BF16)) + b_ref[0]


def _ada_mod(c, ada_w, ada_b):
    L, D, N = ada_w.shape
    B = c.shape[0]
    tn = 1536
    out = pl.pallas_call(
        _ada_kernel,
        grid=(L, N // tn),
        in_specs=[pl.BlockSpec((B, D), lambda l, j: (0, 0)),
                  pl.BlockSpec((1, D, tn), lambda l, j: (l, 0, j)),
                  pl.BlockSpec((1, 1, tn), lambda l, j: (l, 0, j))],
        out_specs=pl.BlockSpec((1, B, tn), lambda l, j: (l, 0, j)),
        out_shape=jax.ShapeDtypeStruct((L, B, N), F32),
        compiler_params=_cparams(("arbitrary", "arbitrary")),
        name="ada_mod",
    )(c, ada_w, ada_b.reshape(L, 1, N))
    return out.reshape(L, B, 6, D)


def _inproj_kernel(x_ref, mod_ref, ng_ref, w_ref, qg_ref, kg_ref,
                   q_out, cmp_out, k_out, v_out, dn_out, z_out, mg_out, sm_out):
    x = x_ref[...]
    ms = jnp.mean(x * x, axis=-1, keepdims=True)
    h = x * lax.rsqrt(ms + NORM_EPS) * ng_ref[...]
    h = h * (1.0 + mod_ref[0, 1:2, :]) + mod_ref[0, 0:1, :]
    hb = h.astype(BF16)

    def seg(ab):
        return _bdot(hb, w_ref[:, ab[0]:ab[1]])

    q = seg(_SEG_Q)
    q = q * lax.rsqrt(_seg64_sumsq(q) * (1.0 / HEAD_DIM) + NORM_EPS) * qg_ref[...]
    q_out[...] = (q * (HEAD_DIM ** -0.5)).astype(BF16)
    cmp_out[...] = seg(_SEG_CMP)
    k = seg(_SEG_K)
    k = k * lax.rsqrt(_seg64_sumsq(k) * (1.0 / HEAD_DIM) + NORM_EPS) * kg_ref[...]
    k_out[...] = k.astype(BF16)
    v = seg(_SEG_V)
    lane = lax.broadcasted_iota(jnp.int32, v.shape, 1) % 256
    ones = jnp.where((lane >= 64) & (lane < 192), 1.0, 0.0)
    v_out[...] = (v + ones).astype(BF16)
    dn_out[...] = seg(_SEG_DN)
    z_out[...] = seg(_SEG_Z)
    mg_out[...] = seg(_SEG_MG)
    sm_out[...] = seg(_SEG_SM)


def _in_proj(x2, mod_l, norm_g, w_proj, q_gain, k_gain, seq):
    T, D = x2.shape
    tm = TM_PROJ
    tpb = seq // tm
    widths = [(512, BF16), (256, F32), (256, BF16), (512, BF16), (1536, F32), (512, F32), (2048, F32), (128, F32)]
    return pl.pallas_call(
        _inproj_kernel,
        grid=(T // tm,),
        in_specs=[pl.BlockSpec((tm, D), lambda i: (i, 0)),
                  pl.BlockSpec((1, 6, D), lambda i: (i // tpb, 0, 0)),
                  pl.BlockSpec((1, D), lambda i: (0, 0)),
                  pl.BlockSpec((D, N_PROJ), lambda i: (0, 0)),
                  pl.BlockSpec((1, 512), lambda i: (0, 0)),
                  pl.BlockSpec((1, 256), lambda i: (0, 0))],
        out_specs=[pl.BlockSpec((tm, w), lambda i: (i, 0)) for w, _ in widths],
        out_shape=[jax.ShapeDtypeStruct((T, w), dt) for w, dt in widths],
        compiler_params=_cparams(("arbitrary",)),
        name="in_proj",
    )(x2, mod_l, norm_g, w_proj, q_gain, k_gain)


def _compress_kernel(kraw_ref, vraw_ref, pos_ref, w1_ref, w2_ref, kg_ref, kc_out, vc_out, *, n_chunks):
    for j, raw_ref, out_ref in ((0, kraw_ref, kc_out), (1, vraw_ref, vc_out)):
        top = [jnp.zeros((n_chunks, CMP_HIDDEN), F32) for _ in range(NSA_GROUPS)]
        bot = [jnp.zeros((n_chunks, CMP_HIDDEN), F32) for _ in range(NSA_GROUPS)]
        for r in range(CMP_STRIDE):
            xr = raw_ref[0, pl.ds(r, n_chunks, stride=CMP_STRIDE), :]
            x_top = (xr + pos_ref[j, r:r + 1, :]).astype(BF16)
            x_bot = (xr + pos_ref[j, CMP_STRIDE + r:CMP_STRIDE + r + 1, :]).astype(BF16)
            for g in range(NSA_GROUPS):
                ls = slice(g * HEAD_DIM, (g + 1) * HEAD_DIM)
                top[g] = top[g] + _bdot(x_top[:, ls], w1_ref[j, r * HEAD_DIM:(r + 1) * HEAD_DIM, :])
                bot[g] = bot[g] + _bdot(x_bot[:, ls], w1_ref[j, (CMP_STRIDE + r) * HEAD_DIM:(CMP_STRIDE + r + 1) * HEAD_DIM, :])
        outs = []
        for g in range(NSA_GROUPS):
            hid = top[g] + pltpu.roll(bot[g], n_chunks - 1, 0)
            o = _bdot(jax.nn.gelu(hid).astype(BF16), w2_ref[j])
            if j == 0:
                ms = jnp.mean(o * o, axis=-1, keepdims=True)
                o = o * lax.rsqrt(ms + NORM_EPS) * kg_ref[...]
            outs.append(o)
        out_ref[0] = jnp.concatenate(outs, axis=1).astype(BF16)


def _compress(cmp_raw, pos128, w1, w2, k_gain):
    B, S, _ = cmp_raw.shape
    nc = S // CMP_STRIDE
    return pl.pallas_call(
        functools.partial(_compress_kernel, n_chunks=nc),
        grid=(B,),
        in_specs=[pl.BlockSpec((1, S, LANES), lambda b: (b, 0, 0)),
                  pl.BlockSpec((1, S, LANES), lambda b: (b, 0, 1)),
                  pl.BlockSpec((2, CMP_BLOCK, LANES), lambda b: (0, 0, 0)),
                  pl.BlockSpec((2, CMP_BLOCK * HEAD_DIM, CMP_HIDDEN), lambda b: (0, 0, 0)),
                  pl.BlockSpec((2, CMP_HIDDEN, HEAD_DIM), lambda b: (0, 0, 0)),
                  pl.BlockSpec((1, HEAD_DIM), lambda b: (0, 0))],
        out_specs=[pl.BlockSpec((1, nc, LANES), lambda b: (b, 0, 0))] * 2,
        out_shape=[jax.ShapeDtypeStruct((B, nc, LANES), BF16)] * 2,
        compiler_params=_cparams(("arbitrary",)),
        name="nsa_compress",
    )(cmp_raw, cmp_raw, pos128, w1, w2, k_gain)


def _nsa_tables(rel_bias, seq):
    nq = seq // TQ
    ncp = seq // CMP_STRIDE
    buckets = _rel_bucket_table(seq + WINDOW + TQ)
    rb = rel_bias.astype(F32)

    def lookup(dist, valid):
        ids = jnp.asarray(np.where(valid, buckets[np.clip(dist, 0, None)], -1).astype(np.int8))
        out = jnp.full((NSA_HEADS,) + dist.shape, NEG, F32)
        for b in range(REL_BUCKETS):
            out = jnp.where(ids[None] == b, rb[b].reshape((NSA_HEADS,) + (1,) * dist.ndim), out)
        return out

    def tiles(dist, valid):
        t = lookup(dist, valid)
        return t.transpose(1, 0, 2, 3).reshape(dist.shape[0], NSA_GROUPS, NSA_HPG * TQ, LANES)

    q = np.arange(TQ)[None, :, None]
    k = np.arange(LANES)[None, None, :]
    dlim = int(np.argmax(buckets == REL_BUCKETS - 1))
    dt = min(nq - 1, -(-(dlim + TQ - 1) // TQ))
    dist = TQ * np.arange(-1, dt + 1)[:, None, None] + q - k
    sel_tab = tiles(dist, (dist >= 0) & (np.arange(-1, dt + 1)[:, None, None] >= 0))
    dist = WINDOW - LANES * np.arange(WINDOW // LANES + 1)[:, None, None] + q - k
    win_tab = tiles(dist, (dist >= 0) & (dist < WINDOW))
    n = np.arange(ncp)[None, :]
    dist = np.arange(seq)[:, None] - (n * CMP_STRIDE + CMP_BLOCK - 1)
    cmp_tab = lookup(dist, (dist >= 0) & (n < ncp - 1))
    cmp_tab = cmp_tab.reshape(NSA_GROUPS, NSA_HPG, nq, TQ, ncp).transpose(2, 0, 1, 3, 4)
    cmp_tab = cmp_tab.reshape(nq, NSA_GROUPS, NSA_HPG * TQ, ncp)
    return sel_tab, win_tab, cmp_tab, dt


def _nsa_static(seq):
    nb = seq // SEL_BLOCK
    ncp = seq // CMP_STRIDE
    ratio, nsub = SEL_BLOCK // CMP_STRIDE, CMP_BLOCK // CMP_STRIDE
    delta = np.arange(ncp)[:, None] - ratio * np.arange(nb)[None, :]
    m_idx = delta[..., None] + np.arange(nsub)
    overlap = np.sum((m_idx >= 0) & (m_idx < ratio), axis=-1).astype(np.float32)
    overlap[ncp - 1, :] = 0.0
    expand = (np.arange(nb)[:, None] == (np.arange(seq)[None, :] // SEL_BLOCK)).astype(np.float32)
    return jnp.asarray(overlap.T, BF16), jnp.asarray(expand, BF16)


def _nsa_kernel(q_ref, sm_ref, kc_ref, vc_ref, k_ref, v_ref, bc_ref, ws_ref, ww_ref, ovt_ref, ex_ref,
                o_ref, m_s, acc_s, *, n_blocks, n_sel, dt):
    i = pl.program_id(1)
    t0 = i * TQ
    rows = NSA_HPG * TQ
    half = lax.broadcasted_iota(jnp.int32, (TQ, LANES), 1) // HEAD_DIM
    q = q_ref[0].astype(F32)
    gates = jax.nn.sigmoid(sm_ref[0])
    ng = NSA_GROUPS
    arows = ng * rows
    lane2 = lax.broadcasted_iota(jnp.int32, (n_blocks, ng * TQ), 1)
    jb = lax.broadcasted_iota(jnp.int32, (n_blocks, ng * TQ), 0)
    cur = (t0 + lane2 % TQ) // SEL_BLOCK
    valid = jb <= cur
    forced = valid & ((jb == 0) | (jb > cur - SEL_LOCAL))
    sub = TK_SEL // LANES
    y_heads = []

    def block_max(blocks):
        bm = blocks[0]
        for b in blocks[1:]:
            bm = jnp.maximum(bm, b)
        return jnp.max(bm, axis=-1, keepdims=True)

    def softmax_pv(blocks, v_tiles, m):
        p = jnp.concatenate([jnp.exp(b - m) for b in blocks], axis=1).astype(BF16)
        return jnp.concatenate([_bdot(p[g * rows:(g + 1) * rows], v_tiles[g]) for g in range(ng)], axis=0)

    def online_update(blocks, v_tiles, m_ref, acc_ref):
        m_old = m_ref[...]
        m_new = jnp.maximum(m_old, block_max(blocks))
        acc_ref[...] = jnp.exp(m_old - m_new) * acc_ref[...] + softmax_pv(blocks, v_tiles, m_new)
        m_ref[...] = m_new

    stack = []
    for g in range(ng):
        for h in range(NSA_HPG):
            hd = NSA_HPG * g + h
            blk = q[:, (hd // 2) * LANES:(hd // 2 + 1) * LANES]
            if hd % 2 != g:
                blk = pltpu.roll(blk, HEAD_DIM, 1)
            stack.append(jnp.where(half == g, blk, 0.0))
    qa = jnp.concatenate(stack, axis=0).astype(BF16)
    group_half = jnp.concatenate([half == g for g in range(ng) for _ in range(NSA_HPG)], axis=0)

    bc = bc_ref[0].reshape(arows, bc_ref.shape[-1])
    sc = _dot_nt(qa, kc_ref[0]) + bc
    ec = jnp.exp(sc - jnp.max(sc, axis=-1, keepdims=True))
    ec = jnp.where(bc > 0.5 * NEG, ec, 0.0)
    lc = jnp.sum(ec, axis=-1, keepdims=True)
    pc = ec * (1.0 / jnp.where(lc > 0.0, lc, 1.0))
    o_c = _bdot(pc.astype(BF16), vc_ref[0])

    psum = jnp.concatenate(
        [sum(pc[g * rows + h * TQ:g * rows + (h + 1) * TQ] for h in range(NSA_HPG)) for g in range(ng)], axis=0)
    p_hi = psum.astype(BF16)
    p_lo = (psum - p_hi.astype(F32)).astype(BF16)
    imp = _dot_nt(ovt_ref[...], p_hi) + _dot_nt(ovt_ref[...], p_lo)
    score = jnp.where(forced, FORCE_SCORE, jnp.where(valid, imp, -1.0))

    def pick(_, carry):
        sc_, sel_ = carry
        top = jnp.max(sc_, axis=0, keepdims=True)
        first = jnp.min(jnp.where(sc_ == top, jb, n_blocks), axis=0, keepdims=True)
        hit = jb == first
        return jnp.where(hit, -3e38, sc_), jnp.where(hit, 1.0, sel_)

    _, sel = lax.fori_loop(0, n_sel, pick, (score, jnp.zeros((n_blocks, ng * TQ), F32)))
    sel_q = jnp.transpose(sel).astype(BF16)

    m_s[...] = jnp.full((arows, LANES), -3e38, F32)
    acc_s[...] = jnp.zeros((arows, LANES), F32)

    def sel_scores(jt):
        ks = pl.multiple_of(jt * TK_SEL, TK_SEL)
        s = _dot_nt(qa, k_ref[0, pl.ds(ks, TK_SEL), 0:LANES])
        selx = _bdot(sel_q, ex_ref[:, pl.ds(ks, TK_SEL)])
        madd = (selx - 1.0) * (-NEG)
        parts = []
        for c in range(sub):
            d = i - (jt * sub + c)
            b = ws_ref[jnp.clip(d, -1, dt) + 1].reshape(arows, LANES)
            sc_ = (s[:, c * LANES:(c + 1) * LANES] + b).reshape(ng, NSA_HPG, TQ, LANES)
            sc_ = sc_ + madd[:, c * LANES:(c + 1) * LANES].reshape(ng, 1, TQ, LANES)
            parts.append(sc_.reshape(arows, LANES))
        return parts

    def sel_update(jt, parts):
        ks = pl.multiple_of(jt * TK_SEL, TK_SEL)
        online_update(parts, [v_ref[0, pl.ds(ks, TK_SEL), g * LANES:(g + 1) * LANES] for g in range(ng)],
                      m_s, acc_s)

    def sel_group(jp, carry):
        tiles = [SEL_TILES_PER_ITER * jp + u for u in range(SEL_TILES_PER_ITER)]
        scores = [sel_scores(jt) for jt in tiles]
        for jt, parts in zip(tiles, scores):
            sel_update(jt, parts)
        return carry

    def sel_single(jt, carry):
        sel_update(jt, sel_scores(jt))
        return carry

    n_tiles = (t0 + TQ + TK_SEL - 1) // TK_SEL
    n_groups = n_tiles // SEL_TILES_PER_ITER
    lax.fori_loop(0, n_groups, sel_group, 0)
    lax.fori_loop(n_groups * SEL_TILES_PER_ITER, n_tiles, sel_single, 0)

    nwt = WINDOW // LANES + 1
    starts = [t0 - WINDOW + c * LANES for c in range(nwt)]
    clamped = [pl.multiple_of(jnp.maximum(ks, 0), LANES) for ks in starts]
    kw = jnp.concatenate([k_ref[0, pl.ds(ks, LANES), LANES:2 * LANES] for ks in clamped], axis=0)
    vws = [jnp.concatenate([v_ref[0, pl.ds(ks, LANES), (2 + g) * LANES:(3 + g) * LANES] for ks in clamped], axis=0)
           for g in range(ng)]
    sw = _dot_nt(qa, kw)
    blocks = [sw[:, c * LANES:(c + 1) * LANES]
              + jnp.where(starts[c] >= 0, ww_ref[c].reshape(arows, LANES), NEG) for c in range(nwt)]

    def finish(acc):
        out = acc / pltpu.roll(acc, HEAD_DIM, 1)
        return jnp.where(group_half, out, 0.0)

    o_s = finish(acc_s[...])
    o_w = finish(softmax_pv(blocks, vws, block_max(blocks)))
    for g in range(ng):
        for h in range(NSA_HPG):
            hd = NSA_HPG * g + h
            rs = slice(g * rows + h * TQ, g * rows + (h + 1) * TQ)
            y = (gates[:, 3 * hd:3 * hd + 1] * o_c[rs] + gates[:, 3 * hd + 1:3 * hd + 2] * o_s[rs]
                 + gates[:, 3 * hd + 2:3 * hd + 3] * o_w[rs])
            y = jnp.where(half == g, y, 0.0)
            if hd % 2 != g:
                y = pltpu.roll(y, HEAD_DIM, 1)
            y_heads.append(y)

    o_ref[0] = jnp.concatenate([y_heads[2 * c] + y_heads[2 * c + 1] for c in range(NSA_HEADS // 2)],
                               axis=1).astype(BF16)


def _nsa_attention(q, sm, kc, vc, k_sw, v_sw, tabs, statics):
    B, S, _ = q.shape
    sel_tab, win_tab, cmp_tab, dt = tabs
    ovt, expand = statics
    nq = S // TQ
    ncp = S // CMP_STRIDE
    nb = S // SEL_BLOCK
    rows = NSA_HPG * TQ
    const = lambda nd: (lambda b, i: (0,) * nd)
    return pl.pallas_call(
        functools.partial(_nsa_kernel, n_blocks=nb, n_sel=min(SEL_TOPK, nb), dt=dt),
        grid=(B, nq),
        in_specs=[pl.BlockSpec((1, TQ, NSA_Q), lambda b, i: (b, i, 0)),
                  pl.BlockSpec((1, TQ, LANES), lambda b, i: (b, i, 0)),
                  pl.BlockSpec((1, ncp, LANES), lambda b, i: (b, 0, 0)),
                  pl.BlockSpec((1, ncp, LANES), lambda b, i: (b, 0, 0)),
                  pl.BlockSpec((1, S, 256), lambda b, i: (b, 0, 0)),
                  pl.BlockSpec((1, S, 512), lambda b, i: (b, 0, 0)),
                  pl.BlockSpec((1, NSA_GROUPS, rows, ncp), lambda b, i: (i, 0, 0, 0)),
                  pl.BlockSpec(sel_tab.shape, const(4)),
                  pl.BlockSpec(win_tab.shape, const(4)),
                  pl.BlockSpec(ovt.shape, const(2)),
                  pl.BlockSpec(expand.shape, const(2))],
        out_specs=pl.BlockSpec((1, TQ, NSA_Q), lambda b, i: (b, i, 0)),
        out_shape=jax.ShapeDtypeStruct((B, S, NSA_Q), BF16),
        scratch_shapes=[pltpu.VMEM((NSA_GROUPS * rows, LANES), F32), pltpu.VMEM((NSA_GROUPS * rows, LANES), F32)],
        compiler_params=_cparams(("arbitrary", "arbitrary")),
        name="nsa_attention",
    )(q, sm, kc, vc, k_sw, v_sw, cmp_tab, sel_tab, win_tab, ovt, expand)


DN_QUAD = 4
DN_QW = DN_QUAD * HEAD_DIM


def _block_diag(a, head_masks):
    zero = jnp.zeros_like(a)
    return jnp.concatenate([jnp.where(m, a, zero) for m in head_masks], axis=0)


def _dn_kernel(qkv_ref, sm_ref, z_ref, cw_ref, alog_ref, dtb_ref, ng_ref, eb_ref, ea_ref, o_ref,
               xbuf, state, q_s, k_s, v_s, b_s, gc_s, out_s):
    j = pl.program_id(1)
    ct = CT_DN
    c = DN_CHUNK

    @pl.when(j == 0)
    def _():
        xbuf[0:8, :] = jnp.zeros((8, 3 * DN_W), F32)
        state[...] = jnp.zeros(state.shape, F32)

    xbuf[8:8 + ct, :] = qkv_ref[0]
    acc = cw_ref[0:1, :] * xbuf[5:5 + ct, :]
    for tap in range(1, DN_CONV):
        acc = acc + cw_ref[tap:tap + 1, :] * xbuf[5 + tap:5 + tap + ct, :]
    xbuf[0:8, :] = xbuf[ct:ct + 8, :]
    y = jax.nn.silu(acc)
    qh = y[:, 0:DN_W]
    kh = y[:, DN_W:2 * DN_W]
    q_s[...] = qh * lax.rsqrt(_seg64_sumsq(qh) + NORM_EPS) * (HEAD_DIM ** -0.5)
    k_s[...] = kh * lax.rsqrt(_seg64_sumsq(kh) + NORM_EPS)
    v_s[...] = y[:, 2 * DN_W:3 * DN_W]
    sm = sm_ref[0]
    b_s[...] = sum(_bdot(p, eb_ref[...]) for p in _split3(jax.nn.sigmoid(sm)))
    gdec = -jnp.exp(alog_ref[...]) * jax.nn.softplus(sm + dtb_ref[...])
    g_wide = [_bdot(p, ea_ref[...]).astype(BF16) for p in _split3(gdec)]
    row = lax.broadcasted_iota(jnp.int32, (c, DN_QW), 0)
    col = lax.broadcasted_iota(jnp.int32, (c, DN_QW), 1) % c
    causal = row >= col
    strict = row > col
    diag = row == col
    eye = diag.astype(F32)
    masks = [lax.broadcasted_iota(jnp.int32, (c, DN_QW), 1) // HEAD_DIM == h for h in range(DN_QUAD)]
    tril = (lax.broadcasted_iota(jnp.int32, (c, c), 0) >= lax.broadcasted_iota(jnp.int32, (c, c), 1)).astype(BF16)
    for ch in range(ct // c):
        rs = slice(ch * c, (ch + 1) * c)
        gc_s[rs, :] = sum(_bdot(tril, gw[rs]) for gw in g_wide)

    n_quads = DN_HEADS // DN_QUAD
    n_chunks = ct // c
    items = [(ch, qd) for ch in range(n_chunks) for qd in range(n_quads)]
    sl = lambda it: (slice(it[0] * c, (it[0] + 1) * c), slice(it[1] * DN_QW, (it[1] + 1) * DN_QW))
    every = lambda f, *lists: [f(*args) for args in zip(*lists)]
    bd = lambda a: _block_diag(a.astype(BF16), masks)
    q4 = [q_s[sl(it)] for it in items]
    k4 = [k_s[sl(it)] for it in items]
    b4 = [b_s[sl(it)] for it in items]
    gc4 = [gc_s[sl(it)] for it in items]
    vb = [v_s[sl(it)] * b for it, b in zip(items, b4)]
    g_key = every(lambda g: jnp.sum(jnp.where(diag, g, 0.0), axis=0, keepdims=True), gc4)
    decay = every(lambda g, gk: jnp.where(causal, jnp.exp(jnp.where(causal, g - gk, 0.0)), 0.0), gc4, g_key)
    eg = every(jnp.exp, gc4)
    kb = every(lambda k, b: k * b, k4, b4)
    k_bd = every(bd, k4)
    low = every(lambda a, kd, dc: jnp.where(strict, _dot_nt(a.astype(BF16), kd) * dc, 0.0), kb, k_bd, decay)
    attn = every(lambda a, kd, dc: jnp.where(causal, _dot_nt(a.astype(BF16), kd) * dc, 0.0), q4, k_bd, decay)
    x = every(lambda lo: eye - lo, low)
    p = every(lambda lo: _bdot(lo.astype(BF16), bd(lo)), low)
    steps = int(math.log2(c)) - 1
    for s in range(steps):
        pd = every(bd, p)
        x = every(lambda xi, pdi: xi + _bdot(xi.astype(BF16), pdi), x, pd)
        if s + 1 < steps:
            p = every(lambda pi, pdi: _bdot(pi.astype(BF16), pdi), p, pd)
    t_inv = every(lambda xi: xi.astype(BF16), x)
    u = every(lambda t, a: _bdot(t, bd(a)), t_inv, vb)
    w = every(lambda t, a, e: _bdot(t, bd(a * e)), t_inv, kb, eg)
    qe = every(lambda a, e: (a * e).astype(BF16), q4, eg)
    g_last = every(lambda g: g[c - 1:c, :], gc4)
    k_dec = every(lambda k, gl, g: (k * jnp.exp(gl - g)).astype(BF16), k4, g_last, gc4)

    st = [state[qd] for qd in range(n_quads)]
    for ch in range(n_chunks):
        idx = [ch * n_quads + qd for qd in range(n_quads)]
        s_bd = every(bd, st)
        v_new = [(u[i] - _bdot(w[i].astype(BF16), sb)).astype(BF16) for i, sb in zip(idx, s_bd)]
        outs = [_bdot(qe[i], sb) + _bdot(attn[i].astype(BF16), _block_diag(vn, masks))
                for i, sb, vn in zip(idx, s_bd, v_new)]
        cross = [_dot_tn(k_dec[i], vn) for i, vn in zip(idx, v_new)]
        for qd, i in enumerate(idx):
            out_s[sl(items[i])] = outs[qd]
            upd = sum(jnp.where(masks[h], cross[qd][h * HEAD_DIM:(h + 1) * HEAD_DIM, :], 0.0) for h in range(DN_QUAD))
            st[qd] = st[qd] * jnp.exp(g_last[i]) + upd
    for qd in range(n_quads):
        state[qd] = st[qd]

    o = out_s[...]
    o = o * lax.rsqrt(_seg64_sumsq(o) * (1.0 / HEAD_DIM) + NORM_EPS) * ng_ref[...]
    o_ref[0] = (o * jax.nn.silu(z_ref[0])).astype(BF16)


def _deltanet(dn_qkv, sm, z, conv_w, alog128, dtb128, ng512):
    B, S, _ = dn_qkv.shape
    ct = CT_DN
    head_of_lane = np.arange(DN_W) // HEAD_DIM
    spread = lambda off: jnp.asarray(np.arange(LANES)[:, None] == off + head_of_lane[None, :], BF16)
    return pl.pallas_call(
        _dn_kernel,
        grid=(B, S // ct),
        in_specs=[pl.BlockSpec((1, ct, 3 * DN_W), lambda b, j: (b, j, 0)),
                  pl.BlockSpec((1, ct, LANES), lambda b, j: (b, j, 0)),
                  pl.BlockSpec((1, ct, DN_W), lambda b, j: (b, j, 0)),
                  pl.BlockSpec((DN_CONV, 3 * DN_W), lambda b, j: (0, 0)),
                  pl.BlockSpec((1, LANES), lambda b, j: (0, 0)),
                  pl.BlockSpec((1, LANES), lambda b, j: (0, 0)),
                  pl.BlockSpec((1, DN_W), lambda b, j: (0, 0)),
                  pl.BlockSpec((LANES, DN_W), lambda b, j: (0, 0)),
                  pl.BlockSpec((LANES, DN_W), lambda b, j: (0, 0))],
        out_specs=pl.BlockSpec((1, ct, DN_W), lambda b, j: (b, j, 0)),
        out_shape=jax.ShapeDtypeStruct((B, S, DN_W), BF16),
        scratch_shapes=[pltpu.VMEM((ct + 8, 3 * DN_W), F32),
                        pltpu.VMEM((DN_HEADS // DN_QUAD, HEAD_DIM, DN_QW), F32),
                        pltpu.VMEM((ct, DN_W), F32), pltpu.VMEM((ct, DN_W), F32), pltpu.VMEM((ct, DN_W), F32),
                        pltpu.VMEM((ct, DN_W), F32), pltpu.VMEM((ct, DN_W), F32),
                        pltpu.VMEM((ct, DN_W), F32)],
        compiler_params=_cparams(("arbitrary", "arbitrary")),
        name="gated_deltanet",
    )(dn_qkv, sm, z, conv_w, alog128, dtb128, ng512, spread(SM_BETA), spread(SM_A))


def _merge_kernel(ya_ref, yb_ref, mg_ref, x_ref, mod_ref, wa_ref, wb_ref, wo_ref, ng_ref, rw_ref,
                  xo_ref, hp_ref, lg_ref):
    d = D_MODEL
    m = jax.nn.sigmoid(mg_ref[...])
    y = m[:, :d] * _bdot(ya_ref[...], wa_ref[...]) + m[:, d:] * _bdot(yb_ref[...], wb_ref[...])
    xn = x_ref[...] + mod_ref[0, 2:3, :] * _bdot(y.astype(BF16), wo_ref[...])
    xo_ref[...] = xn
    ms = jnp.mean(xn * xn, axis=-1, keepdims=True)
    h = xn * lax.rsqrt(ms + NORM_EPS) * ng_ref[...]
    h = h * (1.0 + mod_ref[0, 4:5, :]) + mod_ref[0, 3:4, :]
    lg_ref[...] = lax.dot_general(rw_ref[...], h, (((1,), (1,)), ((), ())), preferred_element_type=F32,
                                  precision=lax.Precision.HIGHEST)
    bits = pltpu.bitcast(h.astype(BF16).astype(F32), jnp.uint32)
    hp_ref[...] = (bits[:, :d // 2] & jnp.uint32(0xFFFF0000)) | (bits[:, d // 2:] >> 16)


def _merge(ya, yb, mg, x2, mod_l, wa, wb, wo, norm_g, router_wt, seq):
    T, D = x2.shape
    tm = TM_MERGE
    tpb = seq // tm
    row = lambda w: pl.BlockSpec((tm, w), lambda i: (i, 0))
    full = lambda a: pl.BlockSpec(a.shape, lambda i: (0,) * a.ndim)
    return pl.pallas_call(
        _merge_kernel,
        grid=(T // tm,),
        in_specs=[row(NSA_Q), row(DN_W), row(2 * D), row(D),
                  pl.BlockSpec((1, 6, D), lambda i: (i // tpb, 0, 0)),
                  full(wa), full(wb), full(wo), full(norm_g), full(router_wt)],
        out_specs=[row(D), row(D // 2), pl.BlockSpec((N_EXPERTS, tm), lambda i: (0, i))],
        out_shape=[jax.ShapeDtypeStruct((T, D), F32), jax.ShapeDtypeStruct((T, D // 2), jnp.uint32),
                   jax.ShapeDtypeStruct((N_EXPERTS, T), F32)],
        compiler_params=_cparams(("arbitrary",)),
        name="merge_out",
    )(ya, yb, mg, x2, mod_l, wa, wb, wo, norm_g, router_wt)


def _moe_rows(tm):
    return -(-(TOPK * tm + N_EXPERTS * (ROW_ALIGN - 1) + BLK_MOE) // ROW_ALIGN) * ROW_ALIGN


def _first_max(vals):
    best = vals[0]
    for v in vals[1:]:
        best = jnp.maximum(best, v)
    idx = jnp.full(best.shape, len(vals) - 1, jnp.int32)
    for j in range(len(vals) - 2, -1, -1):
        idx = jnp.where(vals[j] == best, j, idx)
    return best, idx


def _pick(vals, idx):
    out = vals[-1]
    for j in range(len(vals) - 2, -1, -1):
        out = jnp.where(idx == j, vals[j], out)
    return out


def _route_kernel(lg_ref, rb_ref, tri_ref, low_ref, slot_ref, wt_ref, seg_ref, *, tm):
    epg = EXPERTS_PER_GROUP
    scores = jax.nn.sigmoid(lg_ref[...])
    sel = scores + rb_ref[...]
    s_rows = [sel[e:e + 1, :] for e in range(N_EXPERTS)]
    p_rows = [scores[e:e + 1, :] for e in range(N_EXPERTS)]
    grp = []
    for g in range(N_GROUPS):
        a, b, c, d = s_rows[epg * g:epg * g + epg]
        hi1, lo1, hi2, lo2 = jnp.maximum(a, b), jnp.minimum(a, b), jnp.maximum(c, d), jnp.minimum(c, d)
        second = jnp.maximum(jnp.minimum(hi1, hi2), jnp.where(hi1 >= hi2, lo1, lo2))
        grp.append(jnp.maximum(hi1, hi2) + second)
    _, gidx = _first_max(grp)
    cs = [_pick([s_rows[epg * g + j] for g in range(N_GROUPS)], gidx) for j in range(epg)]
    cp = [_pick([p_rows[epg * g + j] for g in range(N_GROUPS)], gidx) for j in range(epg)]
    _, j1 = _first_max(cs)
    _, j2 = _first_max([jnp.where(j1 == j, -jnp.inf, cs[j]) for j in range(epg)])
    w1 = _pick(cp, j1)
    w2 = _pick(cp, j2)
    den = w1 + w2
    wt_ref[0] = jnp.concatenate([w1 / den, w2 / den], axis=0)
    e1 = epg * gidx + j1
    e2 = epg * gidx + j2

    erow = lax.broadcasted_iota(jnp.int32, (N_EXPERTS, tm), 0)
    oh0 = (erow == e1).astype(F32)
    oh1 = (erow == e2).astype(F32)
    cum = _bdot(jnp.concatenate([oh0, oh1], axis=0).astype(BF16), tri_ref[...])
    cum0, cum1 = cum[:N_EXPERTS], cum[N_EXPERTS:]
    cnt0 = cum0[:, tm - 1:tm]
    counts = cnt0 + cum1[:, tm - 1:tm]
    padded = jnp.floor((counts + (ROW_ALIGN - 1)) * (1.0 / ROW_ALIGN)) * ROW_ALIGN
    start = jnp.dot(low_ref[...], jnp.broadcast_to(padded, (N_EXPERTS, LANES)),
                    preferred_element_type=F32, precision=lax.Precision.HIGHEST)
    st = start[:, 0:1]
    slot0 = jnp.sum(oh0 * (st + cum0 - 1.0), axis=0, keepdims=True)
    slot1 = jnp.sum(oh1 * (st + cnt0 + cum1 - 1.0), axis=0, keepdims=True)
    slot_ref[0] = jnp.concatenate([slot0, slot1], axis=0).astype(jnp.int32)
    nch = jnp.floor((counts + (BLK_MOE - 1)) * (1.0 / BLK_MOE))
    seg_ref[0] = jnp.concatenate([start, jnp.broadcast_to(nch, (N_EXPERTS, LANES))], axis=0).astype(jnp.int32)


def _moe_route(logits_t, router_b, tm):
    E, T = logits_t.shape
    nt = T // tm
    tri = (jnp.arange(tm)[:, None] <= jnp.arange(tm)[None, :]).astype(BF16)
    low = (jnp.arange(E)[None, :] < jnp.arange(E)[:, None]).astype(F32)
    slot, wt, seg = pl.pallas_call(
        functools.partial(_route_kernel, tm=tm),
        grid=(nt,),
        in_specs=[pl.BlockSpec((E, tm), lambda i: (0, i)),
                  pl.BlockSpec((E, 1), lambda i: (0, 0)),
                  pl.BlockSpec((tm, tm), lambda i: (0, 0)),
                  pl.BlockSpec((E, E), lambda i: (0, 0))],
        out_specs=[pl.BlockSpec((1, TOPK, tm), lambda i: (i, 0, 0)),
                   pl.BlockSpec((1, TOPK, tm), lambda i: (i, 0, 0)),
                   pl.BlockSpec((1, 2 * E, LANES), lambda i: (i, 0, 0))],
        out_shape=[jax.ShapeDtypeStruct((nt, TOPK, tm), jnp.int32),
                   jax.ShapeDtypeStruct((nt, TOPK, tm), F32),
                   jax.ShapeDtypeStruct((nt, 2 * E, LANES), jnp.int32)],
        compiler_params=_cparams(("arbitrary",)),
        name="moe_route",
    )(logits_t, router_b.astype(F32)[:, None], tri, low)
    return slot, wt, seg[:, :, 0][:, None, :]


def _moe_kernel(slot_ref, wt_ref, seg_ref, hp_ref, x_ref, mod_ref, wg_ref, wu_ref, wd_ref,
                o_ref, gbuf, ybuf, *, tm):
    e = pl.program_id(1)
    half = D_MODEL // 2

    @pl.when(e == 0)
    def _():
        gbuf[...] = jnp.zeros(gbuf.shape, gbuf.dtype)

        def scatter(t, carry):
            row = hp_ref[pl.ds(t, 1), :]
            gbuf[pl.ds(slot_ref[0, 0, t], 1), :] = row
            gbuf[pl.ds(slot_ref[0, 1, t], 1), :] = row
            return carry
        lax.fori_loop(0, tm, scatter, 0, unroll=8)

    for sub in range(MOE_EXPERTS_PER_STEP):
        ex = e * MOE_EXPERTS_PER_STEP + sub
        start = seg_ref[0, 0, ex]

        def chunk(ci, carry, start=start, sub=sub):
            r0 = pl.multiple_of(start + ci * BLK_MOE, ROW_ALIGN)
            w = gbuf[pl.ds(r0, BLK_MOE), :]
            x_hi = pltpu.bitcast(w & jnp.uint32(0xFFFF0000), F32).astype(BF16)
            x_lo = pltpu.bitcast(w << 16, F32).astype(BF16)
            gt = _bdot(x_hi, wg_ref[0, sub, 0:half, :]) + _bdot(x_lo, wg_ref[0, sub, half:, :])
            up = _bdot(x_hi, wu_ref[0, sub, 0:half, :]) + _bdot(x_lo, wu_ref[0, sub, half:, :])
            act = (jax.nn.silu(gt) * up).astype(BF16)
            ybuf[pl.ds(r0, BLK_MOE), :] = _bdot(act, wd_ref[0, sub])
            return carry

        lax.fori_loop(0, seg_ref[0, 0, N_EXPERTS + ex], chunk, 0)

    @pl.when(e == N_EXPERTS // MOE_EXPERTS_PER_STEP - 1)
    def _():
        def combine(t, carry):
            y0 = ybuf[pl.ds(slot_ref[0, 0, t], 1), :] * wt_ref[0, 0, t]
            y1 = ybuf[pl.ds(slot_ref[0, 1, t], 1), :] * wt_ref[0, 1, t]
            o_ref[pl.ds(t, 1), :] = y0 + y1
            return carry
        lax.fori_loop(0, tm, combine, 0, unroll=8)
        o_ref[...] = x_ref[...] + mod_ref[0, 5:6, :] * o_ref[...]


def _moe(hp, x2, mod_l, route, wg, wu, wd, layer, seq):
    T, D = x2.shape
    tm = TM_MOE
    nt = T // tm
    tpb = seq // tm
    n_rows = _moe_rows(tm)
    slot, wt, seg = route
    smem = lambda r, w: pl.BlockSpec((1, r, w), lambda i, e: (i, 0, 0), memory_space=pltpu.SMEM)
    return pl.pallas_call(
        functools.partial(_moe_kernel, tm=tm),
        grid=(nt, N_EXPERTS // MOE_EXPERTS_PER_STEP),
        in_specs=[smem(TOPK, tm), smem(TOPK, tm), smem(1, 2 * N_EXPERTS),
                  pl.BlockSpec((tm, D // 2), lambda i, e: (i, 0)),
                  pl.BlockSpec((tm, D), lambda i, e: (i, 0)),
                  pl.BlockSpec((1, 6, D), lambda i, e: (i // tpb, 0, 0)),
                  pl.BlockSpec((1, MOE_EXPERTS_PER_STEP, D, D_EXPERT), lambda i, e: (layer, e, 0, 0)),
                  pl.BlockSpec((1, MOE_EXPERTS_PER_STEP, D, D_EXPERT), lambda i, e: (layer, e, 0, 0)),
                  pl.BlockSpec((1, MOE_EXPERTS_PER_STEP, D_EXPERT, D), lambda i, e: (layer, e, 0, 0))],
        out_specs=pl.BlockSpec((tm, D), lambda i, e: (i, 0)),
        out_shape=jax.ShapeDtypeStruct((T, D), F32),
        scratch_shapes=[pltpu.VMEM((n_rows, D // 2), jnp.uint32), pltpu.VMEM((n_rows, D), F32)],
        compiler_params=_cparams(("arbitrary", "arbitrary")),
        name="moe_ffn",
    )(slot, wt, seg, hp, x2, mod_l, wg, wu, wd)


def _pad_lanes(v, offset):
    return jnp.zeros((1, LANES), F32).at[0, offset:offset + v.shape[0]].set(v.astype(F32))


def kernel(x, c, rel_bias, router_w, router_b, ada_w, ada_b, norm1_g, norm2_g, w_in, qk_norm_g,
           cmp_pos, cmp_w1, cmp_w2, dn_conv_w, dn_a_log, dn_dt_bias, dn_norm_g, w_branch_a,
           w_branch_b, w_out, moe_w_gate, moe_w_up, moe_w_down):
    B, S, D = x.shape
    T = B * S
    L = ada_w.shape[0]

    mod = _ada_mod(c, ada_w, ada_b)
    w_proj = _proj_weight(w_in)
    tabs = _nsa_tables(rel_bias, S)
    statics = _nsa_static(S)
    router_wt = router_w.astype(F32).T
    wg, wu, wd = moe_w_gate.astype(BF16), moe_w_up.astype(BF16), moe_w_down.astype(BF16)

    x2 = x.reshape(T, D)
    for l in range(L):
        qkg = qk_norm_g[l].astype(F32)
        q_gain = jnp.tile(qkg[0], NSA_HEADS)[None, :]
        k_gain = jnp.concatenate([jnp.tile(qkg[2], NSA_GROUPS), jnp.tile(qkg[3], NSA_GROUPS)])[None, :]
        q, cmp_raw, k_sw, v_sw, dn_qkv, z, mg, sm = _in_proj(
            x2, mod[l], norm1_g[l][None, :], w_proj[l], q_gain, k_gain, S)

        pos128 = jnp.tile(cmp_pos[l].astype(F32), (1, 1, NSA_GROUPS))
        kc, vc = _compress(cmp_raw.reshape(B, S, 256), pos128, cmp_w1[l].astype(BF16),
                           cmp_w2[l].astype(BF16), qkg[1][None, :])
        y_a = _nsa_attention(q.reshape(B, S, NSA_Q), sm.reshape(B, S, LANES), kc, vc,
                             k_sw.reshape(B, S, 256), v_sw.reshape(B, S, 512), tabs, statics)

        y_b = _deltanet(dn_qkv.reshape(B, S, 3 * DN_W), sm.reshape(B, S, LANES), z.reshape(B, S, DN_W),
                        dn_conv_w[l].astype(F32), _pad_lanes(dn_a_log[l], SM_A),
                        _pad_lanes(dn_dt_bias[l], SM_A), jnp.tile(dn_norm_g[l].astype(F32), DN_HEADS)[None, :])

        x_mid, hp, logits = _merge(y_a.reshape(T, NSA_Q), y_b.reshape(T, DN_W), mg, x2, mod[l],
                                   w_branch_a[l].astype(BF16), w_branch_b[l].astype(BF16),
                                   w_out[l].astype(BF16), norm2_g[l][None, :], router_wt, S)
        route = _moe_route(logits, router_b, TM_MOE)
        x2 = _moe(hp, x_mid, mod[l], route, wg, wu, wd, l, S)
    return x2.reshape(B, S, D)
```

```python
import functools
import math

import numpy as np
import jax
import jax.numpy as jnp
from jax import lax
from jax.experimental import pallas as pl
from jax.experimental.pallas import tpu as pltpu

F32 = jnp.float32
BF16 = jnp.bfloat16

D_MODEL = 1024
DEPTH = 4
HEAD_DIM = 64
NSA_HEADS = 8
NSA_GROUPS = 2
NSA_HPG = NSA_HEADS // NSA_GROUPS
CMP_STRIDE = 16
CMP_BLOCK = 32
CMP_HIDDEN = 256
SEL_BLOCK = 64
SEL_TOPK = 16
SEL_LOCAL = 2
WINDOW = 512
DN_HEADS = 8
DN_CONV = 4
DN_CHUNK = 64
REL_BUCKETS = 32
REL_MAX_DIST = 1024
N_EXPERTS = 16
N_GROUPS = 4
EXPERTS_PER_GROUP = N_EXPERTS // N_GROUPS
TOPK = 2
D_EXPERT = 512
NORM_EPS = 1e-6
FORCE_SCORE = 1e9
NEG = -1e30

NSA_Q = NSA_HEADS * HEAD_DIM
NSA_KV = NSA_GROUPS * HEAD_DIM
DN_W = DN_HEADS * HEAD_DIM
IN_SIZES = (NSA_Q, 6 * NSA_KV, 3 * NSA_HEADS, 3 * DN_W, DN_HEADS, DN_HEADS, DN_W, 2 * D_MODEL)
IN_OFFS = tuple(int(v) for v in np.cumsum((0,) + IN_SIZES))

LANES = 128
VMEM_LIMIT = 56 * 1024 * 1024

TM_PROJ = 512
TM_MERGE = 512
TQ = 128
TK_SEL = 512
SEL_TILES_PER_ITER = 2
CT_DN = 256
TM_MOE = 1024
BLK_MOE = 128
MOE_EXPERTS_PER_STEP = 4
ROW_ALIGN = 8

_SEG_Q = (0, 512)
_SEG_CMP = (512, 768)
_SEG_K = (768, 1024)
_SEG_V = (1024, 1536)
_SEG_DN = (1536, 3072)
_SEG_Z = (3072, 3584)
_SEG_MG = (3584, 5632)
_SEG_SM = (5632, 5760)
N_PROJ = 5760
SM_GATE, SM_BETA, SM_A = 0, 24, 32


def _proj_weight(w_in):
    o = IN_OFFS
    kv = o[1]
    zeros = lambda n: jnp.zeros(w_in.shape[:2] + (n,), w_in.dtype)
    cut = lambda a, b: w_in[:, :, a:b]
    parts = [cut(o[0], o[1]),
             cut(kv, kv + 256),
             cut(kv + 256, kv + 384), cut(kv + 512, kv + 640)]
    for base in (kv + 384, kv + 640):
        parts += [cut(base, base + 64), zeros(128), cut(base + 64, base + 128)]
    parts += [cut(o[3], o[4]), cut(o[6], o[7]), cut(o[7], o[8]),
              cut(o[2], o[3]), cut(o[4], o[5]), cut(o[5], o[6])]
    width = sum(p.shape[2] for p in parts)
    parts.append(zeros(N_PROJ - width))
    return jnp.concatenate(parts, axis=2).astype(BF16)


def _rel_bucket_table(n):
    exact = REL_BUCKETS // 2
    d = np.arange(n, dtype=np.int64)
    far = np.maximum(d, exact).astype(np.float64)
    large = exact + (np.log(far / exact) / math.log(REL_MAX_DIST / exact) * (REL_BUCKETS - exact)).astype(np.int64)
    return np.where(d < exact, d, np.minimum(large, REL_BUCKETS - 1)).astype(np.int32)


def _cparams(sem, vmem=VMEM_LIMIT):
    return pltpu.CompilerParams(dimension_semantics=sem, vmem_limit_bytes=vmem)


def _bdot(a, b):
    return jnp.dot(a, b, preferred_element_type=F32)


def _dot_nt(a, b):
    return lax.dot_general(a, b, (((1,), (1,)), ((), ())), preferred_element_type=F32)


def _dot_tn(a, b):
    return lax.dot_general(a, b, (((0,), (0,)), ((), ())), preferred_element_type=F32)


def _split3(x):
    h = x.astype(BF16)
    r = x - h.astype(F32)
    m = r.astype(BF16)
    l = (r - m.astype(F32)).astype(BF16)
    return h, m, l


def _seg64_sumsq(x):
    rows, width = x.shape
    low = lax.broadcasted_iota(jnp.int32, (rows, LANES), 1) < 64
    outs = []
    for c in range(width // LANES):
        sq = x[:, c * LANES:(c + 1) * LANES]
        sq = sq * sq
        s_lo = jnp.sum(jnp.where(low, sq, 0.0), axis=-1, keepdims=True)
        s_hi = jnp.sum(jnp.where(low, 0.0, sq), axis=-1, keepdims=True)
        outs.append(jnp.where(low, s_lo, s_hi))
    return outs[0] if len(outs) == 1 else jnp.concatenate(outs, axis=1)


def _ada_kernel(c_ref, w_ref, b_ref, o_ref):
    ca = jax.nn.silu(c_ref[...]).astype(BF16)
    o_ref[0] = _bdot(ca, w_ref[0].astype(BF16)) + b_ref[0]


def _ada_mod(c, ada_w, ada_b):
    L, D, N = ada_w.shape
    B = c.shape[0]
    tn = 1536
    out = pl.pallas_call(
        _ada_kernel,
        grid=(L, N // tn),
        in_specs=[pl.BlockSpec((B, D), lambda l, j: (0, 0)),
                  pl.BlockSpec((1, D, tn), lambda l, j: (l, 0, j)),
                  pl.BlockSpec((1, 1, tn), lambda l, j: (l, 0, j))],
        out_specs=pl.BlockSpec((1, B, tn), lambda l, j: (l, 0, j)),
        out_shape=jax.ShapeDtypeStruct((L, B, N), F32),
        compiler_params=_cparams(("arbitrary", "arbitrary")),
        name="ada_mod",
    )(c, ada_w, ada_b.reshape(L, 1, N))
    return out.reshape(L, B, 6, D)


def _inproj_kernel(x_ref, mod_ref, ng_ref, w_ref, qg_ref, kg_ref,
                   q_out, cmp_out, k_out, v_out, dn_out, z_out, mg_out, sm_out):
    x = x_ref[...]
    ms = jnp.mean(x * x, axis=-1, keepdims=True)
    h = x * lax.rsqrt(ms + NORM_EPS) * ng_ref[...]
    h = h * (1.0 + mod_ref[0, 1:2, :]) + mod_ref[0, 0:1, :]
    hb = h.astype(BF16)

    def seg(ab):
        return _bdot(hb, w_ref[:, ab[0]:ab[1]])

    q = seg(_SEG_Q)
    q = q * lax.rsqrt(_seg64_sumsq(q) * (1.0 / HEAD_DIM) + NORM_EPS) * qg_ref[...]
    q_out[...] = (q * (HEAD_DIM ** -0.5)).astype(BF16)
    cmp_out[...] = seg(_SEG_CMP)
    k = seg(_SEG_K)
    k = k * lax.rsqrt(_seg64_sumsq(k) * (1.0 / HEAD_DIM) + NORM_EPS) * kg_ref[...]
    k_out[...] = k.astype(BF16)
    v = seg(_SEG_V)
    lane = lax.broadcasted_iota(jnp.int32, v.shape, 1) % 256
    ones = jnp.where((lane >= 64) & (lane < 192), 1.0, 0.0)
    v_out[...] = (v + ones).astype(BF16)
    dn_out[...] = seg(_SEG_DN)
    z_out[...] = seg(_SEG_Z)
    mg_out[...] = seg(_SEG_MG)
    sm_out[...] = seg(_SEG_SM)


def _in_proj(x2, mod_l, norm_g, w_proj, q_gain, k_gain, seq):
    T, D = x2.shape
    tm = TM_PROJ
    tpb = seq // tm
    widths = [(512, BF16), (256, F32), (256, BF16), (512, BF16), (1536, F32), (512, F32), (2048, F32), (128, F32)]
    return pl.pallas_call(
        _inproj_kernel,
        grid=(T // tm,),
        in_specs=[pl.BlockSpec((tm, D), lambda i: (i, 0)),
                  pl.BlockSpec((1, 6, D), lambda i: (i // tpb, 0, 0)),
                  pl.BlockSpec((1, D), lambda i: (0, 0)),
                  pl.BlockSpec((D, N_PROJ), lambda i: (0, 0)),
                  pl.BlockSpec((1, 512), lambda i: (0, 0)),
                  pl.BlockSpec((1, 256), lambda i: (0, 0))],
        out_specs=[pl.BlockSpec((tm, w), lambda i: (i, 0)) for w, _ in widths],
        out_shape=[jax.ShapeDtypeStruct((T, w), dt) for w, dt in widths],
        compiler_params=_cparams(("arbitrary",)),
        name="in_proj",
    )(x2, mod_l, norm_g, w_proj, q_gain, k_gain)


def _compress_kernel(kraw_ref, vraw_ref, pos_ref, w1_ref, w2_ref, kg_ref, kc_out, vc_out, *, n_chunks):
    for j, raw_ref, out_ref in ((0, kraw_ref, kc_out), (1, vraw_ref, vc_out)):
        top = [jnp.zeros((n_chunks, CMP_HIDDEN), F32) for _ in range(NSA_GROUPS)]
        bot = [jnp.zeros((n_chunks, CMP_HIDDEN), F32) for _ in range(NSA_GROUPS)]
        for r in range(CMP_STRIDE):
            xr = raw_ref[0, pl.ds(r, n_chunks, stride=CMP_STRIDE), :]
            x_top = (xr + pos_ref[j, r:r + 1, :]).astype(BF16)
            x_bot = (xr + pos_ref[j, CMP_STRIDE + r:CMP_STRIDE + r + 1, :]).astype(BF16)
            for g in range(NSA_GROUPS):
                ls = slice(g * HEAD_DIM, (g + 1) * HEAD_DIM)
                top[g] = top[g] + _bdot(x_top[:, ls], w1_ref[j, r * HEAD_DIM:(r + 1) * HEAD_DIM, :])
                bot[g] = bot[g] + _bdot(x_bot[:, ls], w1_ref[j, (CMP_STRIDE + r) * HEAD_DIM:(CMP_STRIDE + r + 1) * HEAD_DIM, :])
        outs = []
        for g in range(NSA_GROUPS):
            hid = top[g] + pltpu.roll(bot[g], n_chunks - 1, 0)
            o = _bdot(jax.nn.gelu(hid).astype(BF16), w2_ref[j])
            if j == 0:
                ms = jnp.mean(o * o, axis=-1, keepdims=True)
                o = o * lax.rsqrt(ms + NORM_EPS) * kg_ref[...]
            outs.append(o)
        out_ref[0] = jnp.concatenate(outs, axis=1).astype(BF16)


def _compress(cmp_raw, pos128, w1, w2, k_gain):
    B, S, _ = cmp_raw.shape
    nc = S // CMP_STRIDE
    return pl.pallas_call(
        functools.partial(_compress_kernel, n_chunks=nc),
        grid=(B,),
        in_specs=[pl.BlockSpec((1, S, LANES), lambda b: (b, 0, 0)),
                  pl.BlockSpec((1, S, LANES), lambda b: (b, 0, 1)),
                  pl.BlockSpec((2, CMP_BLOCK, LANES), lambda b: (0, 0, 0)),
                  pl.BlockSpec((2, CMP_BLOCK * HEAD_DIM, CMP_HIDDEN), lambda b: (0, 0, 0)),
                  pl.BlockSpec((2, CMP_HIDDEN, HEAD_DIM), lambda b: (0, 0, 0)),
                  pl.BlockSpec((1, HEAD_DIM), lambda b: (0, 0))],
        out_specs=[pl.BlockSpec((1, nc, LANES), lambda b: (b, 0, 0))] * 2,
        out_shape=[jax.ShapeDtypeStruct((B, nc, LANES), BF16)] * 2,
        compiler_params=_cparams(("arbitrary",)),
        name="nsa_compress",
    )(cmp_raw, cmp_raw, pos128, w1, w2, k_gain)


def _nsa_tables(rel_bias, seq):
    nq = seq // TQ
    ncp = seq // CMP_STRIDE
    buckets = _rel_bucket_table(seq + WINDOW + TQ)
    rb = rel_bias.astype(F32)

    def lookup(dist, valid):
        ids = jnp.asarray(np.where(valid, buckets[np.clip(dist, 0, None)], -1).astype(np.int8))
        out = jnp.full((NSA_HEADS,) + dist.shape, NEG, F32)
        for b in range(REL_BUCKETS):
            out = jnp.where(ids[None] == b, rb[b].reshape((NSA_HEADS,) + (1,) * dist.ndim), out)
        return out

    def tiles(dist, valid):
        t = lookup(dist, valid)
        return t.transpose(1, 0, 2, 3).reshape(dist.shape[0], NSA_GROUPS, NSA_HPG * TQ, LANES)

    q = np.arange(TQ)[None, :, None]
    k = np.arange(LANES)[None, None, :]
    dlim = int(np.argmax(buckets == REL_BUCKETS - 1))
    dt = min(nq - 1, -(-(dlim + TQ - 1) // TQ))
    dist = TQ * np.arange(-1, dt + 1)[:, None, None] + q - k
    sel_tab = tiles(dist, (dist >= 0) & (np.arange(-1, dt + 1)[:, None, None] >= 0))
    dist = WINDOW - LANES * np.arange(WINDOW // LANES + 1)[:, None, None] + q - k
    win_tab = tiles(dist, (dist >= 0) & (dist < WINDOW))
    n = np.arange(ncp)[None, :]
    dist = np.arange(seq)[:, None] - (n * CMP_STRIDE + CMP_BLOCK - 1)
    cmp_tab = lookup(dist, (dist >= 0) & (n < ncp - 1))
    cmp_tab = cmp_tab.reshape(NSA_GROUPS, NSA_HPG, nq, TQ, ncp).transpose(2, 0, 1, 3, 4)
    cmp_tab = cmp_tab.reshape(nq, NSA_GROUPS, NSA_HPG * TQ, ncp)
    return sel_tab, win_tab, cmp_tab, dt


def _nsa_static(seq):
    nb = seq // SEL_BLOCK
    ncp = seq // CMP_STRIDE
    ratio, nsub = SEL_BLOCK // CMP_STRIDE, CMP_BLOCK // CMP_STRIDE
    delta = np.arange(ncp)[:, None] - ratio * np.arange(nb)[None, :]
    m_idx = delta[..., None] + np.arange(nsub)
    overlap = np.sum((m_idx >= 0) & (m_idx < ratio), axis=-1).astype(np.float32)
    overlap[ncp - 1, :] = 0.0
    expand = (np.arange(nb)[:, None] == (np.arange(seq)[None, :] // SEL_BLOCK)).astype(np.float32)
    return jnp.asarray(overlap.T, BF16), jnp.asarray(expand, BF16)


def _nsa_kernel(q_ref, sm_ref, kc_ref, vc_ref, k_ref, v_ref, bc_ref, ws_ref, ww_ref, ovt_ref, ex_ref,
                o_ref, m_s, acc_s, *, n_blocks, n_sel, dt):
    i = pl.program_id(1)
    t0 = i * TQ
    rows = NSA_HPG * TQ
    half = lax.broadcasted_iota(jnp.int32, (TQ, LANES), 1) // HEAD_DIM
    q = q_ref[0].astype(F32)
    gates = jax.nn.sigmoid(sm_ref[0])
    ng = NSA_GROUPS
    arows = ng * rows
    lane2 = lax.broadcasted_iota(jnp.int32, (n_blocks, ng * TQ), 1)
    jb = lax.broadcasted_iota(jnp.int32, (n_blocks, ng * TQ), 0)
    cur = (t0 + lane2 % TQ) // SEL_BLOCK
    valid = jb <= cur
    forced = valid & ((jb == 0) | (jb > cur - SEL_LOCAL))
    sub = TK_SEL // LANES
    y_heads = []

    def block_max(blocks):
        bm = blocks[0]
        for b in blocks[1:]:
            bm = jnp.maximum(bm, b)
        return jnp.max(bm, axis=-1, keepdims=True)

    def softmax_pv(blocks, v_tiles, m):
        p = jnp.concatenate([jnp.exp(b - m) for b in blocks], axis=1).astype(BF16)
        return jnp.concatenate([_bdot(p[g * rows:(g + 1) * rows], v_tiles[g]) for g in range(ng)], axis=0)

    def online_update(blocks, v_tiles, m_ref, acc_ref):
        m_old = m_ref[...]
        m_new = jnp.maximum(m_old, block_max(blocks))
        acc_ref[...] = jnp.exp(m_old - m_new) * acc_ref[...] + softmax_pv(blocks, v_tiles, m_new)
        m_ref[...] = m_new

    stack = []
    for g in range(ng):
        for h in range(NSA_HPG):
            hd = NSA_HPG * g + h
            blk = q[:, (hd // 2) * LANES:(hd // 2 + 1) * LANES]
            if hd % 2 != g:
                blk = pltpu.roll(blk, HEAD_DIM, 1)
            stack.append(jnp.where(half == g, blk, 0.0))
    qa = jnp.concatenate(stack, axis=0).astype(BF16)
    group_half = jnp.concatenate([half == g for g in range(ng) for _ in range(NSA_HPG)], axis=0)

    bc = bc_ref[0].reshape(arows, bc_ref.shape[-1])
    sc = _dot_nt(qa, kc_ref[0]) + bc
    ec = jnp.exp(sc - jnp.max(sc, axis=-1, keepdims=True))
    ec = jnp.where(bc > 0.5 * NEG, ec, 0.0)
    lc = jnp.sum(ec, axis=-1, keepdims=True)
    pc = ec * (1.0 / jnp.where(lc > 0.0, lc, 1.0))
    o_c = _bdot(pc.astype(BF16), vc_ref[0])

    psum = jnp.concatenate(
        [sum(pc[g * rows + h * TQ:g * rows + (h + 1) * TQ] for h in range(NSA_HPG)) for g in range(ng)], axis=0)
    p_hi = psum.astype(BF16)
    p_lo = (psum - p_hi.astype(F32)).astype(BF16)
    imp = _dot_nt(ovt_ref[...], p_hi) + _dot_nt(ovt_ref[...], p_lo)
    score = jnp.where(forced, FORCE_SCORE, jnp.where(valid, imp, -1.0))

    def pick(_, carry):
        sc_, sel_ = carry
        top = jnp.max(sc_, axis=0, keepdims=True)
        first = jnp.min(jnp.where(sc_ == top, jb, n_blocks), axis=0, keepdims=True)
        hit = jb == first
        return jnp.where(hit, -3e38, sc_), jnp.where(hit, 1.0, sel_)

    _, sel = lax.fori_loop(0, n_sel, pick, (score, jnp.zeros((n_blocks, ng * TQ), F32)))
    sel_q = jnp.transpose(sel).astype(BF16)

    m_s[...] = jnp.full((arows, LANES), -3e38, F32)
    acc_s[...] = jnp.zeros((arows, LANES), F32)

    def sel_scores(jt):
        ks = pl.multiple_of(jt * TK_SEL, TK_SEL)
        s = _dot_nt(qa, k_ref[0, pl.ds(ks, TK_SEL), 0:LANES])
        selx = _bdot(sel_q, ex_ref[:, pl.ds(ks, TK_SEL)])
        madd = (selx - 1.0) * (-NEG)
        parts = []
        for c in range(sub):
            d = i - (jt * sub + c)
            b = ws_ref[jnp.clip(d, -1, dt) + 1].reshape(arows, LANES)
            sc_ = (s[:, c * LANES:(c + 1) * LANES] + b).reshape(ng, NSA_HPG, TQ, LANES)
            sc_ = sc_ + madd[:, c * LANES:(c + 1) * LANES].reshape(ng, 1, TQ, LANES)
            parts.append(sc_.reshape(arows, LANES))
        return parts

    def sel_update(jt, parts):
        ks = pl.multiple_of(jt * TK_SEL, TK_SEL)
        online_update(parts, [v_ref[0, pl.ds(ks, TK_SEL), g * LANES:(g + 1) * LANES] for g in range(ng)],
                      m_s, acc_s)

    def sel_group(jp, carry):
        tiles = [SEL_TILES_PER_ITER * jp + u for u in range(SEL_TILES_PER_ITER)]
        scores = [sel_scores(jt) for jt in tiles]
        for jt, parts in zip(tiles, scores):
            sel_update(jt, parts)
        return carry

    def sel_single(jt, carry):
        sel_update(jt, sel_scores(jt))
        return carry

    n_tiles = (t0 + TQ + TK_SEL - 1) // TK_SEL
    n_groups = n_tiles // SEL_TILES_PER_ITER
    lax.fori_loop(0, n_groups, sel_group, 0)
    lax.fori_loop(n_groups * SEL_TILES_PER_ITER, n_tiles, sel_single, 0)

    nwt = WINDOW // LANES + 1
    starts = [t0 - WINDOW + c * LANES for c in range(nwt)]
    clamped = [pl.multiple_of(jnp.maximum(ks, 0), LANES) for ks in starts]
    kw = jnp.concatenate([k_ref[0, pl.ds(ks, LANES), LANES:2 * LANES] for ks in clamped], axis=0)
    vws = [jnp.concatenate([v_ref[0, pl.ds(ks, LANES), (2 + g) * LANES:(3 + g) * LANES] for ks in clamped], axis=0)
           for g in range(ng)]
    sw = _dot_nt(qa, kw)
    blocks = [sw[:, c * LANES:(c + 1) * LANES]
              + jnp.where(starts[c] >= 0, ww_ref[c].reshape(arows, LANES), NEG) for c in range(nwt)]

    def finish(acc):
        out = acc / pltpu.roll(acc, HEAD_DIM, 1)
        return jnp.where(group_half, out, 0.0)

    o_s = finish(acc_s[...])
    o_w = finish(softmax_pv(blocks, vws, block_max(blocks)))
    for g in range(ng):
        for h in range(NSA_HPG):
            hd = NSA_HPG * g + h
            rs = slice(g * rows + h * TQ, g * rows + (h + 1) * TQ)
            y = (gates[:, 3 * hd:3 * hd + 1] * o_c[rs] + gates[:, 3 * hd + 1:3 * hd + 2] * o_s[rs]
                 + gates[:, 3 * hd + 2:3 * hd + 3] * o_w[rs])
            y = jnp.where(half == g, y, 0.0)
            if hd % 2 != g:
                y = pltpu.roll(y, HEAD_DIM, 1)
            y_heads.append(y)

    o_ref[0] = jnp.concatenate([y_heads[2 * c] + y_heads[2 * c + 1] for c in range(NSA_HEADS // 2)],
                               axis=1).astype(BF16)


def _nsa_attention(q, sm, kc, vc, k_sw, v_sw, tabs, statics):
    B, S, _ = q.shape
    sel_tab, win_tab, cmp_tab, dt = tabs
    ovt, expand = statics
    nq = S // TQ
    ncp = S // CMP_STRIDE
    nb = S // SEL_BLOCK
    rows = NSA_HPG * TQ
    const = lambda nd: (lambda b, i: (0,) * nd)
    return pl.pallas_call(
        functools.partial(_nsa_kernel, n_blocks=nb, n_sel=min(SEL_TOPK, nb), dt=dt),
        grid=(B, nq),
        in_specs=[pl.BlockSpec((1, TQ, NSA_Q), lambda b, i: (b, i, 0)),
                  pl.BlockSpec((1, TQ, LANES), lambda b, i: (b, i, 0)),
                  pl.BlockSpec((1, ncp, LANES), lambda b, i: (b, 0, 0)),
                  pl.BlockSpec((1, ncp, LANES), lambda b, i: (b, 0, 0)),
                  pl.BlockSpec((1, S, 256), lambda b, i: (b, 0, 0)),
                  pl.BlockSpec((1, S, 512), lambda b, i: (b, 0, 0)),
                  pl.BlockSpec((1, NSA_GROUPS, rows, ncp), lambda b, i: (i, 0, 0, 0)),
                  pl.BlockSpec(sel_tab.shape, const(4)),
                  pl.BlockSpec(win_tab.shape, const(4)),
                  pl.BlockSpec(ovt.shape, const(2)),
                  pl.BlockSpec(expand.shape, const(2))],
        out_specs=pl.BlockSpec((1, TQ, NSA_Q), lambda b, i: (b, i, 0)),
        out_shape=jax.ShapeDtypeStruct((B, S, NSA_Q), BF16),
        scratch_shapes=[pltpu.VMEM((NSA_GROUPS * rows, LANES), F32), pltpu.VMEM((NSA_GROUPS * rows, LANES), F32)],
        compiler_params=_cparams(("arbitrary", "arbitrary")),
        name="nsa_attention",
    )(q, sm, kc, vc, k_sw, v_sw, cmp_tab, sel_tab, win_tab, ovt, expand)


DN_QUAD = 4
DN_QW = DN_QUAD * HEAD_DIM


def _block_diag(a, head_masks):
    zero = jnp.zeros_like(a)
    return jnp.concatenate([jnp.where(m, a, zero) for m in head_masks], axis=0)


def _dn_kernel(qkv_ref, sm_ref, z_ref, cw_ref, alog_ref, dtb_ref, ng_ref, eb_ref, ea_ref, o_ref,
               xbuf, state, q_s, k_s, v_s, b_s, gc_s, out_s):
    j = pl.program_id(1)
    ct = CT_DN
    c = DN_CHUNK

    @pl.when(j == 0)
    def _():
        xbuf[0:8, :] = jnp.zeros((8, 3 * DN_W), F32)
        state[...] = jnp.zeros(state.shape, F32)

    xbuf[8:8 + ct, :] = qkv_ref[0]
    acc = cw_ref[0:1, :] * xbuf[5:5 + ct, :]
    for tap in range(1, DN_CONV):
        acc = acc + cw_ref[tap:tap + 1, :] * xbuf[5 + tap:5 + tap + ct, :]
    xbuf[0:8, :] = xbuf[ct:ct + 8, :]
    y = jax.nn.silu(acc)
    qh = y[:, 0:DN_W]
    kh = y[:, DN_W:2 * DN_W]
    q_s[...] = qh * lax.rsqrt(_seg64_sumsq(qh) + NORM_EPS) * (HEAD_DIM ** -0.5)
    k_s[...] = kh * lax.rsqrt(_seg64_sumsq(kh) + NORM_EPS)
    v_s[...] = y[:, 2 * DN_W:3 * DN_W]
    sm = sm_ref[0]
    b_s[...] = sum(_bdot(p, eb_ref[...]) for p in _split3(jax.nn.sigmoid(sm)))
    gdec = -jnp.exp(alog_ref[...]) * jax.nn.softplus(sm + dtb_ref[...])
    g_wide = [_bdot(p, ea_ref[...]).astype(BF16) for p in _split3(gdec)]
    row = lax.broadcasted_iota(jnp.int32, (c, DN_QW), 0)
    col = lax.broadcasted_iota(jnp.int32, (c, DN_QW), 1) % c
    causal = row >= col
    strict = row > col
    diag = row == col
    eye = diag.astype(F32)
    masks = [lax.broadcasted_iota(jnp.int32, (c, DN_QW), 1) // HEAD_DIM == h for h in range(DN_QUAD)]
    tril = (lax.broadcasted_iota(jnp.int32, (c, c), 0) >= lax.broadcasted_iota(jnp.int32, (c, c), 1)).astype(BF16)
    for ch in range(ct // c):
        rs = slice(ch * c, (ch + 1) * c)
        gc_s[rs, :] = sum(_bdot(tril, gw[rs]) for gw in g_wide)

    n_quads = DN_HEADS // DN_QUAD
    n_chunks = ct // c
    items = [(ch, qd) for ch in range(n_chunks) for qd in range(n_quads)]
    sl = lambda it: (slice(it[0] * c, (it[0] + 1) * c), slice(it[1] * DN_QW, (it[1] + 1) * DN_QW))
    every = lambda f, *lists: [f(*args) for args in zip(*lists)]
    bd = lambda a: _block_diag(a.astype(BF16), masks)
    q4 = [q_s[sl(it)] for it in items]
    k4 = [k_s[sl(it)] for it in items]
    b4 = [b_s[sl(it)] for it in items]
    gc4 = [gc_s[sl(it)] for it in items]
    vb = [v_s[sl(it)] * b for it, b in zip(items, b4)]
    g_key = every(lambda g: jnp.sum(jnp.where(diag, g, 0.0), axis=0, keepdims=True), gc4)
    decay = every(lambda g, gk: jnp.where(causal, jnp.exp(jnp.where(causal, g - gk, 0.0)), 0.0), gc4, g_key)
    eg = every(jnp.exp, gc4)
    kb = every(lambda k, b: k * b, k4, b4)
    k_bd = every(bd, k4)
    low = every(lambda a, kd, dc: jnp.where(strict, _dot_nt(a.astype(BF16), kd) * dc, 0.0), kb, k_bd, decay)
    attn = every(lambda a, kd, dc: jnp.where(causal, _dot_nt(a.astype(BF16), kd) * dc, 0.0), q4, k_bd, decay)
    x = every(lambda lo: eye - lo, low)
    p = every(lambda lo: _bdot(lo.astype(BF16), bd(lo)), low)
    steps = int(math.log2(c)) - 1
    for s in range(steps):
        pd = every(bd, p)
        x = every(lambda xi, pdi: xi + _bdot(xi.astype(BF16), pdi), x, pd)
        if s + 1 < steps:
            p = every(lambda pi, pdi: _bdot(pi.astype(BF16), pdi), p, pd)
    t_inv = every(lambda xi: xi.astype(BF16), x)
    u = every(lambda t, a: _bdot(t, bd(a)), t_inv, vb)
    w = every(lambda t, a, e: _bdot(t, bd(a * e)), t_inv, kb, eg)
    qe = every(lambda a, e: (a * e).astype(BF16), q4, eg)
    g_last = every(lambda g: g[c - 1:c, :], gc4)
    k_dec = every(lambda k, gl, g: (k * jnp.exp(gl - g)).astype(BF16), k4, g_last, gc4)

    st = [state[qd] for qd in range(n_quads)]
    for ch in range(n_chunks):
        idx = [ch * n_quads + qd for qd in range(n_quads)]
        s_bd = every(bd, st)
        v_new = [(u[i] - _bdot(w[i].astype(BF16), sb)).astype(BF16) for i, sb in zip(idx, s_bd)]
        outs = [_bdot(qe[i], sb) + _bdot(attn[i].astype(BF16), _block_diag(vn, masks))
                for i, sb, vn in zip(idx, s_bd, v_new)]
        cross = [_dot_tn(k_dec[i], vn) for i, vn in zip(idx, v_new)]
        for qd, i in enumerate(idx):
            out_s[sl(items[i])] = outs[qd]
            upd = sum(jnp.where(masks[h], cross[qd][h * HEAD_DIM:(h + 1) * HEAD_DIM, :], 0.0) for h in range(DN_QUAD))
            st[qd] = st[qd] * jnp.exp(g_last[i]) + upd
    for qd in range(n_quads):
        state[qd] = st[qd]

    o = out_s[...]
    o = o * lax.rsqrt(_seg64_sumsq(o) * (1.0 / HEAD_DIM) + NORM_EPS) * ng_ref[...]
    o_ref[0] = (o * jax.nn.silu(z_ref[0])).astype(BF16)


def _deltanet(dn_qkv, sm, z, conv_w, alog128, dtb128, ng512):
    B, S, _ = dn_qkv.shape
    ct = CT_DN
    head_of_lane = np.arange(DN_W) // HEAD_DIM
    spread = lambda off: jnp.asarray(np.arange(LANES)[:, None] == off + head_of_lane[None, :], BF16)
    return pl.pallas_call(
        _dn_kernel,
        grid=(B, S // ct),
        in_specs=[pl.BlockSpec((1, ct, 3 * DN_W), lambda b, j: (b, j, 0)),
                  pl.BlockSpec((1, ct, LANES), lambda b, j: (b, j, 0)),
                  pl.BlockSpec((1, ct, DN_W), lambda b, j: (b, j, 0)),
                  pl.BlockSpec((DN_CONV, 3 * DN_W), lambda b, j: (0, 0)),
                  pl.BlockSpec((1, LANES), lambda b, j: (0, 0)),
                  pl.BlockSpec((1, LANES), lambda b, j: (0, 0)),
                  pl.BlockSpec((1, DN_W), lambda b, j: (0, 0)),
                  pl.BlockSpec((LANES, DN_W), lambda b, j: (0, 0)),
                  pl.BlockSpec((LANES, DN_W), lambda b, j: (0, 0))],
        out_specs=pl.BlockSpec((1, ct, DN_W), lambda b, j: (b, j, 0)),
        out_shape=jax.ShapeDtypeStruct((B, S, DN_W), BF16),
        scratch_shapes=[pltpu.VMEM((ct + 8, 3 * DN_W), F32),
                        pltpu.VMEM((DN_HEADS // DN_QUAD, HEAD_DIM, DN_QW), F32),
                        pltpu.VMEM((ct, DN_W), F32), pltpu.VMEM((ct, DN_W), F32), pltpu.VMEM((ct, DN_W), F32),
                        pltpu.VMEM((ct, DN_W), F32), pltpu.VMEM((ct, DN_W), F32),
                        pltpu.VMEM((ct, DN_W), F32)],
        compiler_params=_cparams(("arbitrary", "arbitrary")),
        name="gated_deltanet",
    )(dn_qkv, sm, z, conv_w, alog128, dtb128, ng512, spread(SM_BETA), spread(SM_A))


def _merge_kernel(ya_ref, yb_ref, mg_ref, x_ref, mod_ref, wa_ref, wb_ref, wo_ref, ng_ref, rw_ref,
                  xo_ref, hp_ref, lg_ref):
    d = D_MODEL
    m = jax.nn.sigmoid(mg_ref[...])
    y = m[:, :d] * _bdot(ya_ref[...], wa_ref[...]) + m[:, d:] * _bdot(yb_ref[...], wb_ref[...])
    xn = x_ref[...] + mod_ref[0, 2:3, :] * _bdot(y.astype(BF16), wo_ref[...])
    xo_ref[...] = xn
    ms = jnp.mean(xn * xn, axis=-1, keepdims=True)
    h = xn * lax.rsqrt(ms + NORM_EPS) * ng_ref[...]
    h = h * (1.0 + mod_ref[0, 4:5, :]) + mod_ref[0, 3:4, :]
    lg_ref[...] = lax.dot_general(rw_ref[...], h, (((1,), (1,)), ((), ())), preferred_element_type=F32,
                                  precision=lax.Precision.HIGHEST)
    bits = pltpu.bitcast(h.astype(BF16).astype(F32), jnp.uint32)
    hp_ref[...] = (bits[:, :d // 2] & jnp.uint32(0xFFFF0000)) | (bits[:, d // 2:] >> 16)


def _merge(ya, yb, mg, x2, mod_l, wa, wb, wo, norm_g, router_wt, seq):
    T, D = x2.shape
    tm = TM_MERGE
    tpb = seq // tm
    row = lambda w: pl.BlockSpec((tm, w), lambda i: (i, 0))
    full = lambda a: pl.BlockSpec(a.shape, lambda i: (0,) * a.ndim)
    return pl.pallas_call(
        _merge_kernel,
        grid=(T // tm,),
        in_specs=[row(NSA_Q), row(DN_W), row(2 * D), row(D),
                  pl.BlockSpec((1, 6, D), lambda i: (i // tpb, 0, 0)),
                  full(wa), full(wb), full(wo), full(norm_g), full(router_wt)],
        out_specs=[row(D), row(D // 2), pl.BlockSpec((N_EXPERTS, tm), lambda i: (0, i))],
        out_shape=[jax.ShapeDtypeStruct((T, D), F32), jax.ShapeDtypeStruct((T, D // 2), jnp.uint32),
                   jax.ShapeDtypeStruct((N_EXPERTS, T), F32)],
        compiler_params=_cparams(("arbitrary",)),
        name="merge_out",
    )(ya, yb, mg, x2, mod_l, wa, wb, wo, norm_g, router_wt)


def _moe_rows(tm):
    return -(-(TOPK * tm + N_EXPERTS * (ROW_ALIGN - 1) + BLK_MOE) // ROW_ALIGN) * ROW_ALIGN


def _first_max(vals):
    best = vals[0]
    for v in vals[1:]:
        best = jnp.maximum(best, v)
    idx = jnp.full(best.shape, len(vals) - 1, jnp.int32)
    for j in range(len(vals) - 2, -1, -1):
        idx = jnp.where(vals[j] == best, j, idx)
    return best, idx


def _pick(vals, idx):
    out = vals[-1]
    for j in range(len(vals) - 2, -1, -1):
        out = jnp.where(idx == j, vals[j], out)
    return out


def _route_kernel(lg_ref, rb_ref, tri_ref, low_ref, slot_ref, wt_ref, seg_ref, *, tm):
    epg = EXPERTS_PER_GROUP
    scores = jax.nn.sigmoid(lg_ref[...])
    sel = scores + rb_ref[...]
    s_rows = [sel[e:e + 1, :] for e in range(N_EXPERTS)]
    p_rows = [scores[e:e + 1, :] for e in range(N_EXPERTS)]
    grp = []
    for g in range(N_GROUPS):
        a, b, c, d = s_rows[epg * g:epg * g + epg]
        hi1, lo1, hi2, lo2 = jnp.maximum(a, b), jnp.minimum(a, b), jnp.maximum(c, d), jnp.minimum(c, d)
        second = jnp.maximum(jnp.minimum(hi1, hi2), jnp.where(hi1 >= hi2, lo1, lo2))
        grp.append(jnp.maximum(hi1, hi2) + second)
    _, gidx = _first_max(grp)
    cs = [_pick([s_rows[epg * g + j] for g in range(N_GROUPS)], gidx) for j in range(epg)]
    cp = [_pick([p_rows[epg * g + j] for g in range(N_GROUPS)], gidx) for j in range(epg)]
    _, j1 = _first_max(cs)
    _, j2 = _first_max([jnp.where(j1 == j, -jnp.inf, cs[j]) for j in range(epg)])
    w1 = _pick(cp, j1)
    w2 = _pick(cp, j2)
    den = w1 + w2
    wt_ref[0] = jnp.concatenate([w1 / den, w2 / den], axis=0)
    e1 = epg * gidx + j1
    e2 = epg * gidx + j2

    erow = lax.broadcasted_iota(jnp.int32, (N_EXPERTS, tm), 0)
    oh0 = (erow == e1).astype(F32)
    oh1 = (erow == e2).astype(F32)
    cum = _bdot(jnp.concatenate([oh0, oh1], axis=0).astype(BF16), tri_ref[...])
    cum0, cum1 = cum[:N_EXPERTS], cum[N_EXPERTS:]
    cnt0 = cum0[:, tm - 1:tm]
    counts = cnt0 + cum1[:, tm - 1:tm]
    padded = jnp.floor((counts + (ROW_ALIGN - 1)) * (1.0 / ROW_ALIGN)) * ROW_ALIGN
    start = jnp.dot(low_ref[...], jnp.broadcast_to(padded, (N_EXPERTS, LANES)),
                    preferred_element_type=F32, precision=lax.Precision.HIGHEST)
    st = start[:, 0:1]
    slot0 = jnp.sum(oh0 * (st + cum0 - 1.0), axis=0, keepdims=True)
    slot1 = jnp.sum(oh1 * (st + cnt0 + cum1 - 1.0), axis=0, keepdims=True)
    slot_ref[0] = jnp.concatenate([slot0, slot1], axis=0).astype(jnp.int32)
    nch = jnp.floor((counts + (BLK_MOE - 1)) * (1.0 / BLK_MOE))
    seg_ref[0] = jnp.concatenate([start, jnp.broadcast_to(nch, (N_EXPERTS, LANES))], axis=0).astype(jnp.int32)


def _moe_route(logits_t, router_b, tm):
    E, T = logits_t.shape
    nt = T // tm
    tri = (jnp.arange(tm)[:, None] <= jnp.arange(tm)[None, :]).astype(BF16)
    low = (jnp.arange(E)[None, :] < jnp.arange(E)[:, None]).astype(F32)
    slot, wt, seg = pl.pallas_call(
        functools.partial(_route_kernel, tm=tm),
        grid=(nt,),
        in_specs=[pl.BlockSpec((E, tm), lambda i: (0, i)),
                  pl.BlockSpec((E, 1), lambda i: (0, 0)),
                  pl.BlockSpec((tm, tm), lambda i: (0, 0)),
                  pl.BlockSpec((E, E), lambda i: (0, 0))],
        out_specs=[pl.BlockSpec((1, TOPK, tm), lambda i: (i, 0, 0)),
                   pl.BlockSpec((1, TOPK, tm), lambda i: (i, 0, 0)),
                   pl.BlockSpec((1, 2 * E, LANES), lambda i: (i, 0, 0))],
        out_shape=[jax.ShapeDtypeStruct((nt, TOPK, tm), jnp.int32),
                   jax.ShapeDtypeStruct((nt, TOPK, tm), F32),
                   jax.ShapeDtypeStruct((nt, 2 * E, LANES), jnp.int32)],
        compiler_params=_cparams(("arbitrary",)),
        name="moe_route",
    )(logits_t, router_b.astype(F32)[:, None], tri, low)
    return slot, wt, seg[:, :, 0][:, None, :]


def _moe_kernel(slot_ref, wt_ref, seg_ref, hp_ref, x_ref, mod_ref, wg_ref, wu_ref, wd_ref,
                o_ref, gbuf, ybuf, *, tm):
    e = pl.program_id(1)
    half = D_MODEL // 2

    @pl.when(e == 0)
    def _():
        gbuf[...] = jnp.zeros(gbuf.shape, gbuf.dtype)

        def scatter(t, carry):
            row = hp_ref[pl.ds(t, 1), :]
            gbuf[pl.ds(slot_ref[0, 0, t], 1), :] = row
            gbuf[pl.ds(slot_ref[0, 1, t], 1), :] = row
            return carry
        lax.fori_loop(0, tm, scatter, 0, unroll=8)

    for sub in range(MOE_EXPERTS_PER_STEP):
        ex = e * MOE_EXPERTS_PER_STEP + sub
        start = seg_ref[0, 0, ex]

        def chunk(ci, carry, start=start, sub=sub):
            r0 = pl.multiple_of(start + ci * BLK_MOE, ROW_ALIGN)
            w = gbuf[pl.ds(r0, BLK_MOE), :]
            x_hi = pltpu.bitcast(w & jnp.uint32(0xFFFF0000), F32).astype(BF16)
            x_lo = pltpu.bitcast(w << 16, F32).astype(BF16)
            gt = _bdot(x_hi, wg_ref[0, sub, 0:half, :]) + _bdot(x_lo, wg_ref[0, sub, half:, :])
            up = _bdot(x_hi, wu_ref[0, sub, 0:half, :]) + _bdot(x_lo, wu_ref[0, sub, half:, :])
            act = (jax.nn.silu(gt) * up).astype(BF16)
            ybuf[pl.ds(r0, BLK_MOE), :] = _bdot(act, wd_ref[0, sub])
            return carry

        lax.fori_loop(0, seg_ref[0, 0, N_EXPERTS + ex], chunk, 0)

    @pl.when(e == N_EXPERTS // MOE_EXPERTS_PER_STEP - 1)
    def _():
        def combine(t, carry):
            y0 = ybuf[pl.ds(slot_ref[0, 0, t], 1), :] * wt_ref[0, 0, t]
            y1 = ybuf[pl.ds(slot_ref[0, 1, t], 1), :] * wt_ref[0, 1, t]
            o_ref[pl.ds(t, 1), :] = y0 + y1
            return carry
        lax.fori_loop(0, tm, combine, 0, unroll=8)
        o_ref[...] = x_ref[...] + mod_ref[0, 5:6, :] * o_ref[...]


def _moe(hp, x2, mod_l, route, wg, wu, wd, layer, seq):
    T, D = x2.shape
    tm = TM_MOE
    nt = T // tm
    tpb = seq // tm
    n_rows = _moe_rows(tm)
    slot, wt, seg = route
    smem = lambda r, w: pl.BlockSpec((1, r, w), lambda i, e: (i, 0, 0), memory_space=pltpu.SMEM)
    return pl.pallas_call(
        functools.partial(_moe_kernel, tm=tm),
        grid=(nt, N_EXPERTS // MOE_EXPERTS_PER_STEP),
        in_specs=[smem(TOPK, tm), smem(TOPK, tm), smem(1, 2 * N_EXPERTS),
                  pl.BlockSpec((tm, D // 2), lambda i, e: (i, 0)),
                  pl.BlockSpec((tm, D), lambda i, e: (i, 0), pipeline_mode=pl.Buffered(1)),
                  pl.BlockSpec((1, 6, D), lambda i, e: (i // tpb, 0, 0)),
                  pl.BlockSpec((1, MOE_EXPERTS_PER_STEP, D, D_EXPERT), lambda i, e: (layer, e, 0, 0)),
                  pl.BlockSpec((1, MOE_EXPERTS_PER_STEP, D, D_EXPERT), lambda i, e: (layer, e, 0, 0)),
                  pl.BlockSpec((1, MOE_EXPERTS_PER_STEP, D_EXPERT, D), lambda i, e: (layer, e, 0, 0))],
        out_specs=pl.BlockSpec((tm, D), lambda i, e: (i, 0)),
        out_shape=jax.ShapeDtypeStruct((T, D), F32),
        scratch_shapes=[pltpu.VMEM((n_rows, D // 2), jnp.uint32), pltpu.VMEM((n_rows, D), F32)],
        compiler_params=_cparams(("arbitrary", "arbitrary")),
        name="moe_ffn",
    )(slot, wt, seg, hp, x2, mod_l, wg, wu, wd)


def _pad_lanes(v, offset):
    return jnp.zeros((1, LANES), F32).at[0, offset:offset + v.shape[0]].set(v.astype(F32))


def kernel(x, c, rel_bias, router_w, router_b, ada_w, ada_b, norm1_g, norm2_g, w_in, qk_norm_g,
           cmp_pos, cmp_w1, cmp_w2, dn_conv_w, dn_a_log, dn_dt_bias, dn_norm_g, w_branch_a,
           w_branch_b, w_out, moe_w_gate, moe_w_up, moe_w_down):
    B, S, D = x.shape
    T = B * S
    L = ada_w.shape[0]

    mod = _ada_mod(c, ada_w, ada_b)
    w_proj = _proj_weight(w_in)
    tabs = _nsa_tables(rel_bias, S)
    statics = _nsa_static(S)
    router_wt = router_w.astype(F32).T
    wg, wu, wd = moe_w_gate.astype(BF16), moe_w_up.astype(BF16), moe_w_down.astype(BF16)

    x2 = x.reshape(T, D)
    for l in range(L):
        qkg = qk_norm_g[l].astype(F32)
        q_gain = jnp.tile(qkg[0], NSA_HEADS)[None, :]
        k_gain = jnp.concatenate([jnp.tile(qkg[2], NSA_GROUPS), jnp.tile(qkg[3], NSA_GROUPS)])[None, :]
        q, cmp_raw, k_sw, v_sw, dn_qkv, z, mg, sm = _in_proj(
            x2, mod[l], norm1_g[l][None, :], w_proj[l], q_gain, k_gain, S)

        pos128 = jnp.tile(cmp_pos[l].astype(F32), (1, 1, NSA_GROUPS))
        kc, vc = _compress(cmp_raw.reshape(B, S, 256), pos128, cmp_w1[l].astype(BF16),
                           cmp_w2[l].astype(BF16), qkg[1][None, :])
        y_a = _nsa_attention(q.reshape(B, S, NSA_Q), sm.reshape(B, S, LANES), kc, vc,
                             k_sw.reshape(B, S, 256), v_sw.reshape(B, S, 512), tabs, statics)

        y_b = _deltanet(dn_qkv.reshape(B, S, 3 * DN_W), sm.reshape(B, S, LANES), z.reshape(B, S, DN_W),
                        dn_conv_w[l].astype(F32), _pad_lanes(dn_a_log[l], SM_A),
                        _pad_lanes(dn_dt_bias[l], SM_A), jnp.tile(dn_norm_g[l].astype(F32), DN_HEADS)[None, :])

        x_mid, hp, logits = _merge(y_a.reshape(T, NSA_Q), y_b.reshape(T, DN_W), mg, x2, mod[l],
                                   w_branch_a[l].astype(BF16), w_branch_b[l].astype(BF16),
                                   w_out[l].astype(BF16), norm2_g[l][None, :], router_wt, S)
        route = _moe_route(logits, router_b, TM_MOE)
        x2 = _moe(hp, x_mid, mod[l], route, wg, wu, wd, l, S)
    return x2.reshape(B, S, D)
```

```python
import functools
import math

import numpy as np
import jax
import jax.numpy as jnp
from jax import lax
from jax.experimental import pallas as pl
from jax.experimental.pallas import tpu as pltpu

F32 = jnp.float32
BF16 = jnp.bfloat16

D_MODEL = 1024
DEPTH = 4
HEAD_DIM = 64
NSA_HEADS = 8
NSA_GROUPS = 2
NSA_HPG = NSA_HEADS // NSA_GROUPS
CMP_STRIDE = 16
CMP_BLOCK = 32
CMP_HIDDEN = 256
SEL_BLOCK = 64
SEL_TOPK = 16
SEL_LOCAL = 2
WINDOW = 512
DN_HEADS = 8
DN_CONV = 4
DN_CHUNK = 64
REL_BUCKETS = 32
REL_MAX_DIST = 1024
N_EXPERTS = 16
N_GROUPS = 4
EXPERTS_PER_GROUP = N_EXPERTS // N_GROUPS
TOPK = 2
D_EXPERT = 512
NORM_EPS = 1e-6
FORCE_SCORE = 1e9
NEG = -1e30

NSA_Q = NSA_HEADS * HEAD_DIM
NSA_KV = NSA_GROUPS * HEAD_DIM
DN_W = DN_HEADS * HEAD_DIM
IN_SIZES = (NSA_Q, 6 * NSA_KV, 3 * NSA_HEADS, 3 * DN_W, DN_HEADS, DN_HEADS, DN_W, 2 * D_MODEL)
IN_OFFS = tuple(int(v) for v in np.cumsum((0,) + IN_SIZES))

LANES = 128
VMEM_LIMIT = 56 * 1024 * 1024

TM_PROJ = 512
TM_MERGE = 512
TQ = 128
TK_SEL = 512
SEL_TILES_PER_ITER = 2
CT_DN = 256
TM_MOE = 1024
BLK_MOE = 128
MOE_EXPERTS_PER_STEP = 2
ROW_ALIGN = 8

_SEG_Q = (0, 512)
_SEG_CMP = (512, 768)
_SEG_K = (768, 1024)
_SEG_V = (1024, 1536)
_SEG_DN = (1536, 3072)
_SEG_Z = (3072, 3584)
_SEG_MG = (3584, 5632)
_SEG_SM = (5632, 5760)
N_PROJ = 5760
SM_GATE, SM_BETA, SM_A = 0, 24, 32


def _proj_weight(w_in):
    o = IN_OFFS
    kv = o[1]
    zeros = lambda n: jnp.zeros(w_in.shape[:2] + (n,), w_in.dtype)
    cut = lambda a, b: w_in[:, :, a:b]
    parts = [cut(o[0], o[1]),
             cut(kv, kv + 256),
             cut(kv + 256, kv + 384), cut(kv + 512, kv + 640)]
    for base in (kv + 384, kv + 640):
        parts += [cut(base, base + 64), zeros(128), cut(base + 64, base + 128)]
    parts += [cut(o[3], o[4]), cut(o[6], o[7]), cut(o[7], o[8]),
              cut(o[2], o[3]), cut(o[4], o[5]), cut(o[5], o[6])]
    width = sum(p.shape[2] for p in parts)
    parts.append(zeros(N_PROJ - width))
    return jnp.concatenate(parts, axis=2).astype(BF16)


def _rel_bucket_table(n):
    exact = REL_BUCKETS // 2
    d = np.arange(n, dtype=np.int64)
    far = np.maximum(d, exact).astype(np.float64)
    large = exact + (np.log(far / exact) / math.log(REL_MAX_DIST / exact) * (REL_BUCKETS - exact)).astype(np.int64)
    return np.where(d < exact, d, np.minimum(large, REL_BUCKETS - 1)).astype(np.int32)


def _cparams(sem, vmem=VMEM_LIMIT):
    return pltpu.CompilerParams(dimension_semantics=sem, vmem_limit_bytes=vmem)


def _bdot(a, b):
    return jnp.dot(a, b, preferred_element_type=F32)


def _dot_nt(a, b):
    return lax.dot_general(a, b, (((1,), (1,)), ((), ())), preferred_element_type=F32)


def _dot_tn(a, b):
    return lax.dot_general(a, b, (((0,), (0,)), ((), ())), preferred_element_type=F32)


def _split3(x):
    h = x.astype(BF16)
    r = x - h.astype(F32)
    m = r.astype(BF16)
    l = (r - m.astype(F32)).astype(BF16)
    return h, m, l


def _seg64_sumsq(x):
    rows, width = x.shape
    low = lax.broadcasted_iota(jnp.int32, (rows, LANES), 1) < 64
    outs = []
    for c in range(width // LANES):
        sq = x[:, c * LANES:(c + 1) * LANES]
        sq = sq * sq
        s_lo = jnp.sum(jnp.where(low, sq, 0.0), axis=-1, keepdims=True)
        s_hi = jnp.sum(jnp.where(low, 0.0, sq), axis=-1, keepdims=True)
        outs.append(jnp.where(low, s_lo, s_hi))
    return outs[0] if len(outs) == 1 else jnp.concatenate(outs, axis=1)


def _ada_kernel(c_ref, w_ref, b_ref, o_ref):
    ca = jax.nn.silu(c_ref[...]).astype(BF16)
    o_ref[0] = _bdot(ca, w_ref[0].astype(BF16)) + b_ref[0]


def _ada_mod(c, ada_w, ada_b):
    L, D, N = ada_w.shape
    B = c.shape[0]
    tn = 1536
    out = pl.pallas_call(
        _ada_kernel,
        grid=(L, N // tn),
        in_specs=[pl.BlockSpec((B, D), lambda l, j: (0, 0)),
                  pl.BlockSpec((1, D, tn), lambda l, j: (l, 0, j)),
                  pl.BlockSpec((1, 1, tn), lambda l, j: (l, 0, j))],
        out_specs=pl.BlockSpec((1, B, tn), lambda l, j: (l, 0, j)),
        out_shape=jax.ShapeDtypeStruct((L, B, N), F32),
        compiler_params=_cparams(("arbitrary", "arbitrary")),
        name="ada_mod",
    )(c, ada_w, ada_b.reshape(L, 1, N))
    return out.reshape(L, B, 6, D)


def _inproj_kernel(x_ref, mod_ref, ng_ref, w_ref, qg_ref, kg_ref,
                   q_out, cmp_out, k_out, v_out, dn_out, z_out, mg_out, sm_out):
    x = x_ref[...]
    ms = jnp.mean(x * x, axis=-1, keepdims=True)
    h = x * lax.rsqrt(ms + NORM_EPS) * ng_ref[...]
    h = h * (1.0 + mod_ref[0, 1:2, :]) + mod_ref[0, 0:1, :]
    hb = h.astype(BF16)

    def seg(ab):
        return _bdot(hb, w_ref[:, ab[0]:ab[1]])

    q = seg(_SEG_Q)
    q = q * lax.rsqrt(_seg64_sumsq(q) * (1.0 / HEAD_DIM) + NORM_EPS) * qg_ref[...]
    q_out[...] = (q * (HEAD_DIM ** -0.5)).astype(BF16)
    cmp_out[...] = seg(_SEG_CMP)
    k = seg(_SEG_K)
    k = k * lax.rsqrt(_seg64_sumsq(k) * (1.0 / HEAD_DIM) + NORM_EPS) * kg_ref[...]
    k_out[...] = k.astype(BF16)
    v = seg(_SEG_V)
    lane = lax.broadcasted_iota(jnp.int32, v.shape, 1) % 256
    ones = jnp.where((lane >= 64) & (lane < 192), 1.0, 0.0)
    v_out[...] = (v + ones).astype(BF16)
    dn_out[...] = seg(_SEG_DN)
    z_out[...] = seg(_SEG_Z)
    mg_out[...] = seg(_SEG_MG)
    sm_out[...] = seg(_SEG_SM)


def _in_proj(x2, mod_l, norm_g, w_proj, q_gain, k_gain, seq):
    T, D = x2.shape
    tm = TM_PROJ
    tpb = seq // tm
    widths = [(512, BF16), (256, F32), (256, BF16), (512, BF16), (1536, F32), (512, F32), (2048, F32), (128, F32)]
    return pl.pallas_call(
        _inproj_kernel,
        grid=(T // tm,),
        in_specs=[pl.BlockSpec((tm, D), lambda i: (i, 0)),
                  pl.BlockSpec((1, 6, D), lambda i: (i // tpb, 0, 0)),
                  pl.BlockSpec((1, D), lambda i: (0, 0)),
                  pl.BlockSpec((D, N_PROJ), lambda i: (0, 0)),
                  pl.BlockSpec((1, 512), lambda i: (0, 0)),
                  pl.BlockSpec((1, 256), lambda i: (0, 0))],
        out_specs=[pl.BlockSpec((tm, w), lambda i: (i, 0)) for w, _ in widths],
        out_shape=[jax.ShapeDtypeStruct((T, w), dt) for w, dt in widths],
        compiler_params=_cparams(("arbitrary",)),
        name="in_proj",
    )(x2, mod_l, norm_g, w_proj, q_gain, k_gain)


def _compress_kernel(kraw_ref, vraw_ref, pos_ref, w1_ref, w2_ref, kg_ref, kc_out, vc_out, *, n_chunks):
    for j, raw_ref, out_ref in ((0, kraw_ref, kc_out), (1, vraw_ref, vc_out)):
        top = [jnp.zeros((n_chunks, CMP_HIDDEN), F32) for _ in range(NSA_GROUPS)]
        bot = [jnp.zeros((n_chunks, CMP_HIDDEN), F32) for _ in range(NSA_GROUPS)]
        for r in range(CMP_STRIDE):
            xr = raw_ref[0, pl.ds(r, n_chunks, stride=CMP_STRIDE), :]
            x_top = (xr + pos_ref[j, r:r + 1, :]).astype(BF16)
            x_bot = (xr + pos_ref[j, CMP_STRIDE + r:CMP_STRIDE + r + 1, :]).astype(BF16)
            for g in range(NSA_GROUPS):
                ls = slice(g * HEAD_DIM, (g + 1) * HEAD_DIM)
                top[g] = top[g] + _bdot(x_top[:, ls], w1_ref[j, r * HEAD_DIM:(r + 1) * HEAD_DIM, :])
                bot[g] = bot[g] + _bdot(x_bot[:, ls], w1_ref[j, (CMP_STRIDE + r) * HEAD_DIM:(CMP_STRIDE + r + 1) * HEAD_DIM, :])
        outs = []
        for g in range(NSA_GROUPS):
            hid = top[g] + pltpu.roll(bot[g], n_chunks - 1, 0)
            o = _bdot(jax.nn.gelu(hid).astype(BF16), w2_ref[j])
            if j == 0:
                ms = jnp.mean(o * o, axis=-1, keepdims=True)
                o = o * lax.rsqrt(ms + NORM_EPS) * kg_ref[...]
            outs.append(o)
        out_ref[0] = jnp.concatenate(outs, axis=1).astype(BF16)


def _compress(cmp_raw, pos128, w1, w2, k_gain):
    B, S, _ = cmp_raw.shape
    nc = S // CMP_STRIDE
    return pl.pallas_call(
        functools.partial(_compress_kernel, n_chunks=nc),
        grid=(B,),
        in_specs=[pl.BlockSpec((1, S, LANES), lambda b: (b, 0, 0)),
                  pl.BlockSpec((1, S, LANES), lambda b: (b, 0, 1)),
                  pl.BlockSpec((2, CMP_BLOCK, LANES), lambda b: (0, 0, 0)),
                  pl.BlockSpec((2, CMP_BLOCK * HEAD_DIM, CMP_HIDDEN), lambda b: (0, 0, 0)),
                  pl.BlockSpec((2, CMP_HIDDEN, HEAD_DIM), lambda b: (0, 0, 0)),
                  pl.BlockSpec((1, HEAD_DIM), lambda b: (0, 0))],
        out_specs=[pl.BlockSpec((1, nc, LANES), lambda b: (b, 0, 0))] * 2,
        out_shape=[jax.ShapeDtypeStruct((B, nc, LANES), BF16)] * 2,
        compiler_params=_cparams(("arbitrary",)),
        name="nsa_compress",
    )(cmp_raw, cmp_raw, pos128, w1, w2, k_gain)


def _nsa_tables(rel_bias, seq):
    nq = seq // TQ
    ncp = seq // CMP_STRIDE
    buckets = _rel_bucket_table(seq + WINDOW + TQ)
    rb = rel_bias.astype(F32)

    def lookup(dist, valid):
        ids = jnp.asarray(np.where(valid, buckets[np.clip(dist, 0, None)], -1).astype(np.int8))
        out = jnp.full((NSA_HEADS,) + dist.shape, NEG, F32)
        for b in range(REL_BUCKETS):
            out = jnp.where(ids[None] == b, rb[b].reshape((NSA_HEADS,) + (1,) * dist.ndim), out)
        return out

    def tiles(dist, valid):
        t = lookup(dist, valid)
        return t.transpose(1, 0, 2, 3).reshape(dist.shape[0], NSA_GROUPS, NSA_HPG * TQ, LANES)

    q = np.arange(TQ)[None, :, None]
    k = np.arange(LANES)[None, None, :]
    dlim = int(np.argmax(buckets == REL_BUCKETS - 1))
    dt = min(nq - 1, -(-(dlim + TQ - 1) // TQ))
    dist = TQ * np.arange(-1, dt + 1)[:, None, None] + q - k
    sel_tab = tiles(dist, (dist >= 0) & (np.arange(-1, dt + 1)[:, None, None] >= 0))
    dist = WINDOW - LANES * np.arange(WINDOW // LANES + 1)[:, None, None] + q - k
    win_tab = tiles(dist, (dist >= 0) & (dist < WINDOW))
    n = np.arange(ncp)[None, :]
    dist = np.arange(seq)[:, None] - (n * CMP_STRIDE + CMP_BLOCK - 1)
    cmp_tab = lookup(dist, (dist >= 0) & (n < ncp - 1))
    cmp_tab = cmp_tab.reshape(NSA_GROUPS, NSA_HPG, nq, TQ, ncp).transpose(2, 0, 1, 3, 4)
    cmp_tab = cmp_tab.reshape(nq, NSA_GROUPS, NSA_HPG * TQ, ncp)
    return sel_tab, win_tab, cmp_tab, dt


def _nsa_static(seq):
    nb = seq // SEL_BLOCK
    ncp = seq // CMP_STRIDE
    ratio, nsub = SEL_BLOCK // CMP_STRIDE, CMP_BLOCK // CMP_STRIDE
    delta = np.arange(ncp)[:, None] - ratio * np.arange(nb)[None, :]
    m_idx = delta[..., None] + np.arange(nsub)
    overlap = np.sum((m_idx >= 0) & (m_idx < ratio), axis=-1).astype(np.float32)
    overlap[ncp - 1, :] = 0.0
    expand = (np.arange(nb)[:, None] == (np.arange(seq)[None, :] // SEL_BLOCK)).astype(np.float32)
    return jnp.asarray(overlap.T, BF16), jnp.asarray(expand, BF16)


def _nsa_kernel(q_ref, sm_ref, kc_ref, vc_ref, k_ref, v_ref, bc_ref, ws_ref, ww_ref, ovt_ref, ex_ref,
                o_ref, m_s, acc_s, *, n_blocks, n_sel, dt):
    i = pl.program_id(1)
    t0 = i * TQ
    rows = NSA_HPG * TQ
    half = lax.broadcasted_iota(jnp.int32, (TQ, LANES), 1) // HEAD_DIM
    q = q_ref[0].astype(F32)
    gates = jax.nn.sigmoid(sm_ref[0])
    ng = NSA_GROUPS
    arows = ng * rows
    lane2 = lax.broadcasted_iota(jnp.int32, (n_blocks, ng * TQ), 1)
    jb = lax.broadcasted_iota(jnp.int32, (n_blocks, ng * TQ), 0)
    cur = (t0 + lane2 % TQ) // SEL_BLOCK
    valid = jb <= cur
    forced = valid & ((jb == 0) | (jb > cur - SEL_LOCAL))
    sub = TK_SEL // LANES
    y_heads = []

    def block_max(blocks):
        bm = blocks[0]
        for b in blocks[1:]:
            bm = jnp.maximum(bm, b)
        return jnp.max(bm, axis=-1, keepdims=True)

    def softmax_pv(blocks, v_tiles, m):
        p = jnp.concatenate([jnp.exp(b - m) for b in blocks], axis=1).astype(BF16)
        return jnp.concatenate([_bdot(p[g * rows:(g + 1) * rows], v_tiles[g]) for g in range(ng)], axis=0)

    def online_update(blocks, v_tiles, m_ref, acc_ref):
        m_old = m_ref[...]
        m_new = jnp.maximum(m_old, block_max(blocks))
        acc_ref[...] = jnp.exp(m_old - m_new) * acc_ref[...] + softmax_pv(blocks, v_tiles, m_new)
        m_ref[...] = m_new

    stack = []
    for g in range(ng):
        for h in range(NSA_HPG):
            hd = NSA_HPG * g + h
            blk = q[:, (hd // 2) * LANES:(hd // 2 + 1) * LANES]
            if hd % 2 != g:
                blk = pltpu.roll(blk, HEAD_DIM, 1)
            stack.append(jnp.where(half == g, blk, 0.0))
    qa = jnp.concatenate(stack, axis=0).astype(BF16)
    group_half = jnp.concatenate([half == g for g in range(ng) for _ in range(NSA_HPG)], axis=0)

    bc = bc_ref[0].reshape(arows, bc_ref.shape[-1])
    sc = _dot_nt(qa, kc_ref[0]) + bc
    ec = jnp.exp(sc - jnp.max(sc, axis=-1, keepdims=True))
    ec = jnp.where(bc > 0.5 * NEG, ec, 0.0)
    lc = jnp.sum(ec, axis=-1, keepdims=True)
    pc = ec * (1.0 / jnp.where(lc > 0.0, lc, 1.0))
    o_c = _bdot(pc.astype(BF16), vc_ref[0])

    psum = jnp.concatenate(
        [sum(pc[g * rows + h * TQ:g * rows + (h + 1) * TQ] for h in range(NSA_HPG)) for g in range(ng)], axis=0)
    p_hi = psum.astype(BF16)
    p_lo = (psum - p_hi.astype(F32)).astype(BF16)
    imp = _dot_nt(ovt_ref[...], p_hi) + _dot_nt(ovt_ref[...], p_lo)
    score = jnp.where(forced, FORCE_SCORE, jnp.where(valid, imp, -1.0))

    def pick(_, carry):
        sc_, sel_ = carry
        top = jnp.max(sc_, axis=0, keepdims=True)
        first = jnp.min(jnp.where(sc_ == top, jb, n_blocks), axis=0, keepdims=True)
        hit = jb == first
        return jnp.where(hit, -3e38, sc_), jnp.where(hit, 1.0, sel_)

    all_fit = (t0 + TQ) // SEL_BLOCK <= n_sel
    _, sel = lax.fori_loop(0, jnp.where(all_fit, 0, n_sel), pick,
                           (score, jnp.where(all_fit & valid, 1.0, 0.0)))
    sel_q = jnp.transpose(sel).astype(BF16)

    m_s[...] = jnp.full((arows, LANES), -3e38, F32)
    acc_s[...] = jnp.zeros((arows, LANES), F32)

    def sel_scores(jt):
        ks = pl.multiple_of(jt * TK_SEL, TK_SEL)
        s = _dot_nt(qa, k_ref[0, pl.ds(ks, TK_SEL), 0:LANES])
        selx = _bdot(sel_q, ex_ref[:, pl.ds(ks, TK_SEL)])
        madd = (selx - 1.0) * (-NEG)
        parts = []
        for c in range(sub):
            d = i - (jt * sub + c)
            b = ws_ref[jnp.clip(d, -1, dt) + 1].reshape(arows, LANES)
            sc_ = (s[:, c * LANES:(c + 1) * LANES] + b).reshape(ng, NSA_HPG, TQ, LANES)
            sc_ = sc_ + madd[:, c * LANES:(c + 1) * LANES].reshape(ng, 1, TQ, LANES)
            parts.append(sc_.reshape(arows, LANES))
        return parts

    def sel_update(jt, parts):
        ks = pl.multiple_of(jt * TK_SEL, TK_SEL)
        online_update(parts, [v_ref[0, pl.ds(ks, TK_SEL), g * LANES:(g + 1) * LANES] for g in range(ng)],
                      m_s, acc_s)

    def sel_group(jp, carry):
        tiles = [SEL_TILES_PER_ITER * jp + u for u in range(SEL_TILES_PER_ITER)]
        scores = [sel_scores(jt) for jt in tiles]
        for jt, parts in zip(tiles, scores):
            sel_update(jt, parts)
        return carry

    def sel_single(jt, carry):
        sel_update(jt, sel_scores(jt))
        return carry

    n_tiles = (t0 + TQ + TK_SEL - 1) // TK_SEL
    n_groups = n_tiles // SEL_TILES_PER_ITER
    lax.fori_loop(0, n_groups, sel_group, 0)
    lax.fori_loop(n_groups * SEL_TILES_PER_ITER, n_tiles, sel_single, 0)

    nwt = WINDOW // LANES + 1
    starts = [t0 - WINDOW + c * LANES for c in range(nwt)]
    clamped = [pl.multiple_of(jnp.maximum(ks, 0), LANES) for ks in starts]
    kw = jnp.concatenate([k_ref[0, pl.ds(ks, LANES), LANES:2 * LANES] for ks in clamped], axis=0)
    vws = [jnp.concatenate([v_ref[0, pl.ds(ks, LANES), (2 + g) * LANES:(3 + g) * LANES] for ks in clamped], axis=0)
           for g in range(ng)]
    sw = _dot_nt(qa, kw)
    blocks = [sw[:, c * LANES:(c + 1) * LANES]
              + jnp.where(starts[c] >= 0, ww_ref[c].reshape(arows, LANES), NEG) for c in range(nwt)]

    def finish(acc):
        out = acc / pltpu.roll(acc, HEAD_DIM, 1)
        return jnp.where(group_half, out, 0.0)

    o_s = finish(acc_s[...])
    o_w = finish(softmax_pv(blocks, vws, block_max(blocks)))
    for g in range(ng):
        for h in range(NSA_HPG):
            hd = NSA_HPG * g + h
            rs = slice(g * rows + h * TQ, g * rows + (h + 1) * TQ)
            y = (gates[:, 3 * hd:3 * hd + 1] * o_c[rs] + gates[:, 3 * hd + 1:3 * hd + 2] * o_s[rs]
                 + gates[:, 3 * hd + 2:3 * hd + 3] * o_w[rs])
            y = jnp.where(half == g, y, 0.0)
            if hd % 2 != g:
                y = pltpu.roll(y, HEAD_DIM, 1)
            y_heads.append(y)

    o_ref[0] = jnp.concatenate([y_heads[2 * c] + y_heads[2 * c + 1] for c in range(NSA_HEADS // 2)],
                               axis=1).astype(BF16)


def _nsa_attention(q, sm, kc, vc, k_sw, v_sw, tabs, statics):
    B, S, _ = q.shape
    sel_tab, win_tab, cmp_tab, dt = tabs
    ovt, expand = statics
    nq = S // TQ
    ncp = S // CMP_STRIDE
    nb = S // SEL_BLOCK
    rows = NSA_HPG * TQ
    const = lambda nd: (lambda b, i: (0,) * nd)
    return pl.pallas_call(
        functools.partial(_nsa_kernel, n_blocks=nb, n_sel=min(SEL_TOPK, nb), dt=dt),
        grid=(B, nq),
        in_specs=[pl.BlockSpec((1, TQ, NSA_Q), lambda b, i: (b, i, 0)),
                  pl.BlockSpec((1, TQ, LANES), lambda b, i: (b, i, 0)),
                  pl.BlockSpec((1, ncp, LANES), lambda b, i: (b, 0, 0)),
                  pl.BlockSpec((1, ncp, LANES), lambda b, i: (b, 0, 0)),
                  pl.BlockSpec((1, S, 256), lambda b, i: (b, 0, 0)),
                  pl.BlockSpec((1, S, 512), lambda b, i: (b, 0, 0)),
                  pl.BlockSpec((1, NSA_GROUPS, rows, ncp), lambda b, i: (i, 0, 0, 0)),
                  pl.BlockSpec(sel_tab.shape, const(4)),
                  pl.BlockSpec(win_tab.shape, const(4)),
                  pl.BlockSpec(ovt.shape, const(2)),
                  pl.BlockSpec(expand.shape, const(2))],
        out_specs=pl.BlockSpec((1, TQ, NSA_Q), lambda b, i: (b, i, 0)),
        out_shape=jax.ShapeDtypeStruct((B, S, NSA_Q), BF16),
        scratch_shapes=[pltpu.VMEM((NSA_GROUPS * rows, LANES), F32), pltpu.VMEM((NSA_GROUPS * rows, LANES), F32)],
        compiler_params=_cparams(("arbitrary", "arbitrary")),
        name="nsa_attention",
    )(q, sm, kc, vc, k_sw, v_sw, cmp_tab, sel_tab, win_tab, ovt, expand)


DN_QUAD = 4
DN_QW = DN_QUAD * HEAD_DIM


def _block_diag(a, head_masks):
    zero = jnp.zeros_like(a)
    return jnp.concatenate([jnp.where(m, a, zero) for m in head_masks], axis=0)


def _dn_kernel(qkv_ref, sm_ref, z_ref, cw_ref, alog_ref, dtb_ref, ng_ref, eb_ref, ea_ref, o_ref,
               xbuf, state, q_s, k_s, v_s, b_s, gc_s, out_s):
    j = pl.program_id(1)
    ct = CT_DN
    c = DN_CHUNK

    @pl.when(j == 0)
    def _():
        xbuf[0:8, :] = jnp.zeros((8, 3 * DN_W), F32)
        state[...] = jnp.zeros(state.shape, F32)

    xbuf[8:8 + ct, :] = qkv_ref[0]
    acc = cw_ref[0:1, :] * xbuf[5:5 + ct, :]
    for tap in range(1, DN_CONV):
        acc = acc + cw_ref[tap:tap + 1, :] * xbuf[5 + tap:5 + tap + ct, :]
    xbuf[0:8, :] = xbuf[ct:ct + 8, :]
    y = jax.nn.silu(acc)
    qh = y[:, 0:DN_W]
    kh = y[:, DN_W:2 * DN_W]
    q_s[...] = qh * lax.rsqrt(_seg64_sumsq(qh) + NORM_EPS) * (HEAD_DIM ** -0.5)
    k_s[...] = kh * lax.rsqrt(_seg64_sumsq(kh) + NORM_EPS)
    v_s[...] = y[:, 2 * DN_W:3 * DN_W]
    sm = sm_ref[0]
    b_s[...] = sum(_bdot(p, eb_ref[...]) for p in _split3(jax.nn.sigmoid(sm)))
    gdec = -jnp.exp(alog_ref[...]) * jax.nn.softplus(sm + dtb_ref[...])
    g_wide = [_bdot(p, ea_ref[...]).astype(BF16) for p in _split3(gdec)]
    row = lax.broadcasted_iota(jnp.int32, (c, DN_QW), 0)
    col = lax.broadcasted_iota(jnp.int32, (c, DN_QW), 1) % c
    causal = row >= col
    strict = row > col
    diag = row == col
    eye = diag.astype(F32)
    masks = [lax.broadcasted_iota(jnp.int32, (c, DN_QW), 1) // HEAD_DIM == h for h in range(DN_QUAD)]
    tril = (lax.broadcasted_iota(jnp.int32, (c, c), 0) >= lax.broadcasted_iota(jnp.int32, (c, c), 1)).astype(BF16)
    for ch in range(ct // c):
        rs = slice(ch * c, (ch + 1) * c)
        gc_s[rs, :] = sum(_bdot(tril, gw[rs]) for gw in g_wide)

    n_quads = DN_HEADS // DN_QUAD
    n_chunks = ct // c
    items = [(ch, qd) for ch in range(n_chunks) for qd in range(n_quads)]
    sl = lambda it: (slice(it[0] * c, (it[0] + 1) * c), slice(it[1] * DN_QW, (it[1] + 1) * DN_QW))
    every = lambda f, *lists: [f(*args) for args in zip(*lists)]
    bd = lambda a: _block_diag(a.astype(BF16), masks)
    q4 = [q_s[sl(it)] for it in items]
    k4 = [k_s[sl(it)] for it in items]
    b4 = [b_s[sl(it)] for it in items]
    gc4 = [gc_s[sl(it)] for it in items]
    vb = [v_s[sl(it)] * b for it, b in zip(items, b4)]
    g_key = every(lambda g: jnp.sum(jnp.where(diag, g, 0.0), axis=0, keepdims=True), gc4)
    decay = every(lambda g, gk: jnp.where(causal, jnp.exp(jnp.where(causal, g - gk, 0.0)), 0.0), gc4, g_key)
    eg = every(jnp.exp, gc4)
    kb = every(lambda k, b: k * b, k4, b4)
    k_bd = every(bd, k4)
    low = every(lambda a, kd, dc: jnp.where(strict, _dot_nt(a.astype(BF16), kd) * dc, 0.0), kb, k_bd, decay)
    attn = every(lambda a, kd, dc: jnp.where(causal, _dot_nt(a.astype(BF16), kd) * dc, 0.0), q4, k_bd, decay)
    x = every(lambda lo: eye - lo, low)
    p = every(lambda lo: _bdot(lo.astype(BF16), bd(lo)), low)
    steps = int(math.log2(c)) - 1
    for s in range(steps):
        pd = every(bd, p)
        x = every(lambda xi, pdi: xi + _bdot(xi.astype(BF16), pdi), x, pd)
        if s + 1 < steps:
            p = every(lambda pi, pdi: _bdot(pi.astype(BF16), pdi), p, pd)
    t_inv = every(lambda xi: xi.astype(BF16), x)
    u = every(lambda t, a: _bdot(t, bd(a)), t_inv, vb)
    w = every(lambda t, a, e: _bdot(t, bd(a * e)), t_inv, kb, eg)
    qe = every(lambda a, e: (a * e).astype(BF16), q4, eg)
    g_last = every(lambda g: g[c - 1:c, :], gc4)
    k_dec = every(lambda k, gl, g: (k * jnp.exp(gl - g)).astype(BF16), k4, g_last, gc4)

    st = [state[qd] for qd in range(n_quads)]
    for ch in range(n_chunks):
        idx = [ch * n_quads + qd for qd in range(n_quads)]
        s_bd = every(bd, st)
        v_new = [(u[i] - _bdot(w[i].astype(BF16), sb)).astype(BF16) for i, sb in zip(idx, s_bd)]
        outs = [_bdot(qe[i], sb) + _bdot(attn[i].astype(BF16), _block_diag(vn, masks))
                for i, sb, vn in zip(idx, s_bd, v_new)]
        cross = [_dot_tn(k_dec[i], vn) for i, vn in zip(idx, v_new)]
        for qd, i in enumerate(idx):
            out_s[sl(items[i])] = outs[qd]
            upd = sum(jnp.where(masks[h], cross[qd][h * HEAD_DIM:(h + 1) * HEAD_DIM, :], 0.0) for h in range(DN_QUAD))
            st[qd] = st[qd] * jnp.exp(g_last[i]) + upd
    for qd in range(n_quads):
        state[qd] = st[qd]

    o = out_s[...]
    o = o * lax.rsqrt(_seg64_sumsq(o) * (1.0 / HEAD_DIM) + NORM_EPS) * ng_ref[...]
    o_ref[0] = (o * jax.nn.silu(z_ref[0])).astype(BF16)


def _deltanet(dn_qkv, sm, z, conv_w, alog128, dtb128, ng512):
    B, S, _ = dn_qkv.shape
    ct = CT_DN
    head_of_lane = np.arange(DN_W) // HEAD_DIM
    spread = lambda off: jnp.asarray(np.arange(LANES)[:, None] == off + head_of_lane[None, :], BF16)
    return pl.pallas_call(
        _dn_kernel,
        grid=(B, S // ct),
        in_specs=[pl.BlockSpec((1, ct, 3 * DN_W), lambda b, j: (b, j, 0)),
                  pl.BlockSpec((1, ct, LANES), lambda b, j: (b, j, 0)),
                  pl.BlockSpec((1, ct, DN_W), lambda b, j: (b, j, 0)),
                  pl.BlockSpec((DN_CONV, 3 * DN_W), lambda b, j: (0, 0)),
                  pl.BlockSpec((1, LANES), lambda b, j: (0, 0)),
                  pl.BlockSpec((1, LANES), lambda b, j: (0, 0)),
                  pl.BlockSpec((1, DN_W), lambda b, j: (0, 0)),
                  pl.BlockSpec((LANES, DN_W), lambda b, j: (0, 0)),
                  pl.BlockSpec((LANES, DN_W), lambda b, j: (0, 0))],
        out_specs=pl.BlockSpec((1, ct, DN_W), lambda b, j: (b, j, 0)),
        out_shape=jax.ShapeDtypeStruct((B, S, DN_W), BF16),
        scratch_shapes=[pltpu.VMEM((ct + 8, 3 * DN_W), F32),
                        pltpu.VMEM((DN_HEADS // DN_QUAD, HEAD_DIM, DN_QW), F32),
                        pltpu.VMEM((ct, DN_W), F32), pltpu.VMEM((ct, DN_W), F32), pltpu.VMEM((ct, DN_W), F32),
                        pltpu.VMEM((ct, DN_W), F32), pltpu.VMEM((ct, DN_W), F32),
                        pltpu.VMEM((ct, DN_W), F32)],
        compiler_params=_cparams(("arbitrary", "arbitrary")),
        name="gated_deltanet",
    )(dn_qkv, sm, z, conv_w, alog128, dtb128, ng512, spread(SM_BETA), spread(SM_A))


def _merge_kernel(ya_ref, yb_ref, mg_ref, x_ref, mod_ref, wa_ref, wb_ref, wo_ref, ng_ref, rw_ref,
                  xo_ref, hp_ref, lg_ref):
    d = D_MODEL
    m = jax.nn.sigmoid(mg_ref[...])
    y = m[:, :d] * _bdot(ya_ref[...], wa_ref[...]) + m[:, d:] * _bdot(yb_ref[...], wb_ref[...])
    xn = x_ref[...] + mod_ref[0, 2:3, :] * _bdot(y.astype(BF16), wo_ref[...])
    xo_ref[...] = xn
    ms = jnp.mean(xn * xn, axis=-1, keepdims=True)
    h = xn * lax.rsqrt(ms + NORM_EPS) * ng_ref[...]
    h = h * (1.0 + mod_ref[0, 4:5, :]) + mod_ref[0, 3:4, :]
    lg_ref[...] = lax.dot_general(rw_ref[...], h, (((1,), (1,)), ((), ())), preferred_element_type=F32,
                                  precision=lax.Precision.HIGHEST)
    bits = pltpu.bitcast(h.astype(BF16).astype(F32), jnp.uint32)
    hp_ref[...] = (bits[:, :d // 2] & jnp.uint32(0xFFFF0000)) | (bits[:, d // 2:] >> 16)


def _merge(ya, yb, mg, x2, mod_l, wa, wb, wo, norm_g, router_wt, seq):
    T, D = x2.shape
    tm = TM_MERGE
    tpb = seq // tm
    row = lambda w: pl.BlockSpec((tm, w), lambda i: (i, 0))
    full = lambda a: pl.BlockSpec(a.shape, lambda i: (0,) * a.ndim)
    return pl.pallas_call(
        _merge_kernel,
        grid=(T // tm,),
        in_specs=[row(NSA_Q), row(DN_W), row(2 * D), row(D),
                  pl.BlockSpec((1, 6, D), lambda i: (i // tpb, 0, 0)),
                  full(wa), full(wb), full(wo), full(norm_g), full(router_wt)],
        out_specs=[row(D), row(D // 2), pl.BlockSpec((N_EXPERTS, tm), lambda i: (0, i))],
        out_shape=[jax.ShapeDtypeStruct((T, D), F32), jax.ShapeDtypeStruct((T, D // 2), jnp.uint32),
                   jax.ShapeDtypeStruct((N_EXPERTS, T), F32)],
        compiler_params=_cparams(("arbitrary",)),
        name="merge_out",
    )(ya, yb, mg, x2, mod_l, wa, wb, wo, norm_g, router_wt)


def _moe_rows(tm):
    return -(-(TOPK * tm + N_EXPERTS * (ROW_ALIGN - 1) + BLK_MOE) // ROW_ALIGN) * ROW_ALIGN


def _first_max(vals):
    best = vals[0]
    for v in vals[1:]:
        best = jnp.maximum(best, v)
    idx = jnp.full(best.shape, len(vals) - 1, jnp.int32)
    for j in range(len(vals) - 2, -1, -1):
        idx = jnp.where(vals[j] == best, j, idx)
    return best, idx


def _pick(vals, idx):
    out = vals[-1]
    for j in range(len(vals) - 2, -1, -1):
        out = jnp.where(idx == j, vals[j], out)
    return out


def _route_kernel(lg_ref, rb_ref, tri_ref, low_ref, slot_ref, wt_ref, seg_ref, *, tm):
    epg = EXPERTS_PER_GROUP
    scores = jax.nn.sigmoid(lg_ref[...])
    sel = scores + rb_ref[...]
    s_rows = [sel[e:e + 1, :] for e in range(N_EXPERTS)]
    p_rows = [scores[e:e + 1, :] for e in range(N_EXPERTS)]
    grp = []
    for g in range(N_GROUPS):
        a, b, c, d = s_rows[epg * g:epg * g + epg]
        hi1, lo1, hi2, lo2 = jnp.maximum(a, b), jnp.minimum(a, b), jnp.maximum(c, d), jnp.minimum(c, d)
        second = jnp.maximum(jnp.minimum(hi1, hi2), jnp.where(hi1 >= hi2, lo1, lo2))
        grp.append(jnp.maximum(hi1, hi2) + second)
    _, gidx = _first_max(grp)
    cs = [_pick([s_rows[epg * g + j] for g in range(N_GROUPS)], gidx) for j in range(epg)]
    cp = [_pick([p_rows[epg * g + j] for g in range(N_GROUPS)], gidx) for j in range(epg)]
    _, j1 = _first_max(cs)
    _, j2 = _first_max([jnp.where(j1 == j, -jnp.inf, cs[j]) for j in range(epg)])
    w1 = _pick(cp, j1)
    w2 = _pick(cp, j2)
    den = w1 + w2
    wt_ref[0] = jnp.concatenate([w1 / den, w2 / den], axis=0)
    e1 = epg * gidx + j1
    e2 = epg * gidx + j2

    erow = lax.broadcasted_iota(jnp.int32, (N_EXPERTS, tm), 0)
    oh0 = (erow == e1).astype(F32)
    oh1 = (erow == e2).astype(F32)
    cum = _bdot(jnp.concatenate([oh0, oh1], axis=0).astype(BF16), tri_ref[...])
    cum0, cum1 = cum[:N_EXPERTS], cum[N_EXPERTS:]
    cnt0 = cum0[:, tm - 1:tm]
    counts = cnt0 + cum1[:, tm - 1:tm]
    padded = jnp.floor((counts + (ROW_ALIGN - 1)) * (1.0 / ROW_ALIGN)) * ROW_ALIGN
    start = jnp.dot(low_ref[...], jnp.broadcast_to(padded, (N_EXPERTS, LANES)),
                    preferred_element_type=F32, precision=lax.Precision.HIGHEST)
    st = start[:, 0:1]
    slot0 = jnp.sum(oh0 * (st + cum0 - 1.0), axis=0, keepdims=True)
    slot1 = jnp.sum(oh1 * (st + cnt0 + cum1 - 1.0), axis=0, keepdims=True)
    slot_ref[0] = jnp.concatenate([slot0, slot1], axis=0).astype(jnp.int32)
    nch = jnp.floor((counts + (BLK_MOE - 1)) * (1.0 / BLK_MOE))
    seg_ref[0] = jnp.concatenate([start, jnp.broadcast_to(nch, (N_EXPERTS, LANES))], axis=0).astype(jnp.int32)


def _moe_route(logits_t, router_b, tm):
    E, T = logits_t.shape
    nt = T // tm
    tri = (jnp.arange(tm)[:, None] <= jnp.arange(tm)[None, :]).astype(BF16)
    low = (jnp.arange(E)[None, :] < jnp.arange(E)[:, None]).astype(F32)
    slot, wt, seg = pl.pallas_call(
        functools.partial(_route_kernel, tm=tm),
        grid=(nt,),
        in_specs=[pl.BlockSpec((E, tm), lambda i: (0, i)),
                  pl.BlockSpec((E, 1), lambda i: (0, 0)),
                  pl.BlockSpec((tm, tm), lambda i: (0, 0)),
                  pl.BlockSpec((E, E), lambda i: (0, 0))],
        out_specs=[pl.BlockSpec((1, TOPK, tm), lambda i: (i, 0, 0)),
                   pl.BlockSpec((1, TOPK, tm), lambda i: (i, 0, 0)),
                   pl.BlockSpec((1, 2 * E, LANES), lambda i: (i, 0, 0))],
        out_shape=[jax.ShapeDtypeStruct((nt, TOPK, tm), jnp.int32),
                   jax.ShapeDtypeStruct((nt, TOPK, tm), F32),
                   jax.ShapeDtypeStruct((nt, 2 * E, LANES), jnp.int32)],
        compiler_params=_cparams(("arbitrary",)),
        name="moe_route",
    )(logits_t, router_b.astype(F32)[:, None], tri, low)
    return slot, wt, seg[:, :, 0][:, None, :]


def _moe_kernel(slot_ref, wt_ref, seg_ref, hp_ref, x_ref, mod_ref, wg_ref, wu_ref, wd_ref,
                o_ref, gbuf, ybuf, *, tm):
    e = pl.program_id(1)
    half = D_MODEL // 2

    @pl.when(e == 0)
    def _():
        gbuf[...] = jnp.zeros(gbuf.shape, gbuf.dtype)

        def scatter(t, carry):
            row = hp_ref[pl.ds(t, 1), :]
            gbuf[pl.ds(slot_ref[0, 0, t], 1), :] = row
            gbuf[pl.ds(slot_ref[0, 1, t], 1), :] = row
            return carry
        lax.fori_loop(0, tm, scatter, 0, unroll=8)

    for sub in range(MOE_EXPERTS_PER_STEP):
        ex = e * MOE_EXPERTS_PER_STEP + sub
        start = seg_ref[0, 0, ex]

        def chunk(ci, carry, start=start, sub=sub):
            r0 = pl.multiple_of(start + ci * BLK_MOE, ROW_ALIGN)
            w = gbuf[pl.ds(r0, BLK_MOE), :]
            x_hi = pltpu.bitcast(w & jnp.uint32(0xFFFF0000), F32).astype(BF16)
            x_lo = pltpu.bitcast(w << 16, F32).astype(BF16)
            gt = _bdot(x_hi, wg_ref[0, sub, 0:half, :]) + _bdot(x_lo, wg_ref[0, sub, half:, :])
            up = _bdot(x_hi, wu_ref[0, sub, 0:half, :]) + _bdot(x_lo, wu_ref[0, sub, half:, :])
            act = (jax.nn.silu(gt) * up).astype(BF16)
            ybuf[pl.ds(r0, BLK_MOE), :] = _bdot(act, wd_ref[0, sub])
            return carry

        lax.fori_loop(0, seg_ref[0, 0, N_EXPERTS + ex], chunk, 0)

    @pl.when(e == N_EXPERTS // MOE_EXPERTS_PER_STEP - 1)
    def _():
        def combine(t, carry):
            y0 = ybuf[pl.ds(slot_ref[0, 0, t], 1), :] * wt_ref[0, 0, t]
            y1 = ybuf[pl.ds(slot_ref[0, 1, t], 1), :] * wt_ref[0, 1, t]
            o_ref[pl.ds(t, 1), :] = y0 + y1
            return carry
        lax.fori_loop(0, tm, combine, 0, unroll=8)
        o_ref[...] = x_ref[...] + mod_ref[0, 5:6, :] * o_ref[...]


def _moe(hp, x2, mod_l, route, wg, wu, wd, layer, seq):
    T, D = x2.shape
    tm = TM_MOE
    nt = T // tm
    tpb = seq // tm
    n_rows = _moe_rows(tm)
    slot, wt, seg = route
    smem = lambda r, w: pl.BlockSpec((1, r, w), lambda i, e: (i, 0, 0), memory_space=pltpu.SMEM)
    return pl.pallas_call(
        functools.partial(_moe_kernel, tm=tm),
        grid=(nt, N_EXPERTS // MOE_EXPERTS_PER_STEP),
        in_specs=[smem(TOPK, tm), smem(TOPK, tm), smem(1, 2 * N_EXPERTS),
                  pl.BlockSpec((tm, D // 2), lambda i, e: (i, 0)),
                  pl.BlockSpec((tm, D), lambda i, e: (i, 0)),
                  pl.BlockSpec((1, 6, D), lambda i, e: (i // tpb, 0, 0)),
                  pl.BlockSpec((1, MOE_EXPERTS_PER_STEP, D, D_EXPERT), lambda i, e: (layer, e, 0, 0)),
                  pl.BlockSpec((1, MOE_EXPERTS_PER_STEP, D, D_EXPERT), lambda i, e: (layer, e, 0, 0)),
                  pl.BlockSpec((1, MOE_EXPERTS_PER_STEP, D_EXPERT, D), lambda i, e: (layer, e, 0, 0))],
        out_specs=pl.BlockSpec((tm, D), lambda i, e: (i, 0)),
        out_shape=jax.ShapeDtypeStruct((T, D), F32),
        scratch_shapes=[pltpu.VMEM((n_rows, D // 2), jnp.uint32), pltpu.VMEM((n_rows, D), F32)],
        compiler_params=_cparams(("arbitrary", "arbitrary")),
        name="moe_ffn",
    )(slot, wt, seg, hp, x2, mod_l, wg, wu, wd)


def _pad_lanes(v, offset):
    return jnp.zeros((1, LANES), F32).at[0, offset:offset + v.shape[0]].set(v.astype(F32))


def kernel(x, c, rel_bias, router_w, router_b, ada_w, ada_b, norm1_g, norm2_g, w_in, qk_norm_g,
           cmp_pos, cmp_w1, cmp_w2, dn_conv_w, dn_a_log, dn_dt_bias, dn_norm_g, w_branch_a,
           w_branch_b, w_out, moe_w_gate, moe_w_up, moe_w_down):
    B, S, D = x.shape
    T = B * S
    L = ada_w.shape[0]

    mod = _ada_mod(c, ada_w, ada_b)
    w_proj = _proj_weight(w_in)
    tabs = _nsa_tables(rel_bias, S)
    statics = _nsa_static(S)
    router_wt = router_w.astype(F32).T
    wg, wu, wd = moe_w_gate.astype(BF16), moe_w_up.astype(BF16), moe_w_down.astype(BF16)

    x2 = x.reshape(T, D)
    for l in range(L):
        qkg = qk_norm_g[l].astype(F32)
        q_gain = jnp.tile(qkg[0], NSA_HEADS)[None, :]
        k_gain = jnp.concatenate([jnp.tile(qkg[2], NSA_GROUPS), jnp.tile(qkg[3], NSA_GROUPS)])[None, :]
        q, cmp_raw, k_sw, v_sw, dn_qkv, z, mg, sm = _in_proj(
            x2, mod[l], norm1_g[l][None, :], w_proj[l], q_gain, k_gain, S)

        pos128 = jnp.tile(cmp_pos[l].astype(F32), (1, 1, NSA_GROUPS))
        kc, vc = _compress(cmp_raw.reshape(B, S, 256), pos128, cmp_w1[l].astype(BF16),
                           cmp_w2[l].astype(BF16), qkg[1][None, :])
        y_a = _nsa_attention(q.reshape(B, S, NSA_Q), sm.reshape(B, S, LANES), kc, vc,
                             k_sw.reshape(B, S, 256), v_sw.reshape(B, S, 512), tabs, statics)

        y_b = _deltanet(dn_qkv.reshape(B, S, 3 * DN_W), sm.reshape(B, S, LANES), z.reshape(B, S, DN_W),
                        dn_conv_w[l].astype(F32), _pad_lanes(dn_a_log[l], SM_A),
                        _pad_lanes(dn_dt_bias[l], SM_A), jnp.tile(dn_norm_g[l].astype(F32), DN_HEADS)[None, :])

        x_mid, hp, logits = _merge(y_a.reshape(T, NSA_Q), y_b.reshape(T, DN_W), mg, x2, mod[l],
                                   w_branch_a[l].astype(BF16), w_branch_b[l].astype(BF16),
                                   w_out[l].astype(BF16), norm2_g[l][None, :], router_wt, S)
        route = _moe_route(logits, router_b, TM_MOE)
        x2 = _moe(hp, x_mid, mod[l], route, wg, wu, wd, l, S)
    return x2.reshape(B, S, D)
```
